```python
import jax, jax.numpy as jnp
from jax import lax
import numpy as np

D_MODEL = 1024
BATCH = 1
SEQ = 16384
DEPTH = 4
DEC_BATCH = 32
DEC_SEQ = 2048
PAST_LEN = 128

HEAD_DIM = 64
HEADS_A = 8
KV_HEADS_A = 2
GROUP_A = HEADS_A // KV_HEADS_A
HEADS_B = 8
KV_HEADS_B = 2
GROUP_B = HEADS_B // KV_HEADS_B
WINDOW = 128
BLOCK = 128
N_META = 16
GRID_W = 64
ROPE_BASE = 10000.0
ROPE_FREQS = HEAD_DIM // 4
D_FF = ((8 * D_MODEL // 3 + 255) // 256) * 256
EPS = 1e-6
NEG_INF = -1e30
Q_A = HEADS_A * HEAD_DIM
KV_A = KV_HEADS_A * HEAD_DIM
Q_B = HEADS_B * HEAD_DIM
KV_B = KV_HEADS_B * HEAD_DIM
IN_COLS = Q_A + 2 * KV_A + Q_B + 2 * KV_B + 2 * D_MODEL

kernel_name = "hybrid_window_axial_gqa_encoder"


def rmsnorm(x, g):
    xf = x.astype(jnp.float32)
    y = xf * lax.rsqrt(jnp.mean(xf * xf, axis=-1, keepdims=True) + EPS) * g.astype(jnp.float32)
    return y.astype(x.dtype)


def alibi_slopes(n_heads):
    return 2.0 ** (-8.0 * jnp.arange(1, n_heads + 1, dtype=jnp.float32) / n_heads)


def axial_rope_tables(n_real):
    rows_n = n_real // GRID_W
    rows = jnp.repeat(jnp.arange(rows_n, dtype=jnp.int32), GRID_W)
    cols = jnp.tile(jnp.arange(GRID_W, dtype=jnp.int32), rows_n)
    zeros = jnp.zeros((N_META,), jnp.int32)
    rows = jnp.concatenate([zeros, rows]).astype(jnp.float32)
    cols = jnp.concatenate([zeros, cols]).astype(jnp.float32)
    freqs = ROPE_BASE ** (-jnp.arange(ROPE_FREQS, dtype=jnp.float32) / ROPE_FREQS)
    ang_r = rows[:, None] * freqs[None, :]
    ang_c = cols[:, None] * freqs[None, :]
    return jnp.cos(ang_r), jnp.sin(ang_r), jnp.cos(ang_c), jnp.sin(ang_c)


def apply_axial_rope(x, tabs):
    T = x.shape[1]
    shp = (1, T) + (1,) * (x.ndim - 3) + (ROPE_FREQS,)
    cr, sr, cc, sc = [t.reshape(shp) for t in tabs]
    xf = x.astype(jnp.float32)
    r1, r2, c1, c2 = jnp.split(xf, 4, axis=-1)
    out = jnp.concatenate([r1 * cr - r2 * sr, r2 * cr + r1 * sr,
                           c1 * cc - c2 * sc, c2 * cc + c1 * sc], axis=-1)
    return out.astype(x.dtype)


def softmax_with_sink(s, sink):
    sink_b = jnp.broadcast_to(sink[None, :, :, None, None], s.shape[:-1] + (1,))
    return jax.nn.softmax(jnp.concatenate([s, sink_b], axis=-1), axis=-1)[..., :-1]


def window_attention(q, k, v, sink, slopes):
    B, T = q.shape[0], q.shape[1]
    n = T - N_META
    nb = n // BLOCK
    scale = HEAD_DIM ** -0.5
    qm, qr = q[:, :N_META], q[:, N_META:]
    km, kr = k[:, :N_META], k[:, N_META:]
    vm, vr = v[:, :N_META], v[:, N_META:]
    sink_l = sink.astype(jnp.float32).reshape(KV_HEADS_A, GROUP_A)
    slope = slopes.reshape(KV_HEADS_A, GROUP_A)[:, :, None, None]

    k0, v0 = k[:, :N_META + BLOCK], v[:, :N_META + BLOCK]
    s0 = jnp.einsum('bqkgd,bskd->bkgqs', qm, k0).astype(jnp.float32) * scale
    p0 = softmax_with_sink(s0, sink_l).astype(v.dtype)
    out_m = jnp.einsum('bkgqs,bskd->bqkgd', p0, v0)

    kpad = jnp.pad(kr, ((0, 0), (BLOCK, BLOCK), (0, 0), (0, 0)))
    vpad = jnp.pad(vr, ((0, 0), (BLOCK, BLOCK), (0, 0), (0, 0)))
    qb = jnp.moveaxis(qr.reshape(B, nb, BLOCK, KV_HEADS_A, GROUP_A, HEAD_DIM), 1, 0)
    key_off = jnp.arange(3 * BLOCK) - BLOCK
    rel = jnp.arange(BLOCK)[:, None] - key_off[None, :]
    dist = jnp.abs(rel)
    band = dist <= WINDOW
    penalty = slope * dist.astype(jnp.float32)

    def one_block(args):
        i, qi = args
        kk = lax.dynamic_slice_in_dim(kpad, i * BLOCK, 3 * BLOCK, axis=1)
        vv = lax.dynamic_slice_in_dim(vpad, i * BLOCK, 3 * BLOCK, axis=1)
        s_idx = i * BLOCK + key_off
        valid = band & ((s_idx >= 0) & (s_idx < n))[None, :]
        s_loc = jnp.einsum('bqkgd,bskd->bkgqs', qi, kk).astype(jnp.float32) * scale - penalty
        s_loc = jnp.where(valid, s_loc, NEG_INF)
        s_met = jnp.einsum('bqkgd,bmkd->bkgqm', qi, km).astype(jnp.float32) * scale
        p = softmax_with_sink(jnp.concatenate([s_met, s_loc], axis=-1), sink_l).astype(v.dtype)
        return (jnp.einsum('bkgqm,bmkd->bqkgd', p[..., :N_META], vm)
                + jnp.einsum('bkgqs,bskd->bqkgd', p[..., N_META:], vv))

    out_r = lax.map(one_block, (jnp.arange(nb), qb))
    out_r = jnp.moveaxis(out_r, 0, 1).reshape(B, n, KV_HEADS_A, GROUP_A, HEAD_DIM)
    return jnp.concatenate([out_m, out_r], axis=1)


def global_attention(q, k, v):
    B, T = q.shape[0], q.shape[1]
    n = T - N_META
    nb = n // BLOCK
    scale = HEAD_DIM ** -0.5

    def attend(qi):
        s = jnp.einsum('bqkgd,bskd->bkgqs', qi, k).astype(jnp.float32) * scale
        p = jax.nn.softmax(s, axis=-1).astype(v.dtype)
        return jnp.einsum('bkgqs,bskd->bqkgd', p, v)

    out_m = attend(q[:, :N_META])
    qb = jnp.moveaxis(q[:, N_META:].reshape(B, nb, BLOCK, KV_HEADS_B, GROUP_B, HEAD_DIM), 1, 0)
    out_r = lax.map(attend, qb)
    out_r = jnp.moveaxis(out_r, 0, 1).reshape(B, n, KV_HEADS_B, GROUP_B, HEAD_DIM)
    return jnp.concatenate([out_m, out_r], axis=1)


def encoder_layer(h, tabs, slopes, g_mix_pre, g_mix_post, g_ffn_pre, g_ffn_post,
                  w_in, q_norm_b, k_norm_b, sink_a, w_branch_a, w_branch_b, w_out,
                  w_ffn_up, w_ffn_down):
    B, T = h.shape[0], h.shape[1]
    xn = rmsnorm(h, g_mix_pre)
    proj = xn @ w_in
    o1 = Q_A; o2 = o1 + KV_A; o3 = o2 + KV_A; o4 = o3 + Q_B; o5 = o4 + KV_B; o6 = o5 + KV_B
    o7 = o6 + D_MODEL
    qa, ka, va, qb, kb, vb, ga, gb = jnp.split(proj, [o1, o2, o3, o4, o5, o6, o7], axis=-1)

    qa = qa.reshape(B, T, KV_HEADS_A, GROUP_A, HEAD_DIM)
    ka = ka.reshape(B, T, KV_HEADS_A, HEAD_DIM)
    va = va.reshape(B, T, KV_HEADS_A, HEAD_DIM)
    out_a = window_attention(qa, ka, va, sink_a, slopes).reshape(B, T, Q_A)

    qb = apply_axial_rope(rmsnorm(qb.reshape(B, T, KV_HEADS_B, GROUP_B, HEAD_DIM), q_norm_b), tabs)
    kb = apply_axial_rope(rmsnorm(kb.reshape(B, T, KV_HEADS_B, HEAD_DIM), k_norm_b), tabs)
    vb = vb.reshape(B, T, KV_HEADS_B, HEAD_DIM)
    out_b = global_attention(qb, kb, vb).reshape(B, T, Q_B)

    mix = jax.nn.sigmoid(ga) * (out_a @ w_branch_a) + jax.nn.sigmoid(gb) * (out_b @ w_branch_b)
    h = h + rmsnorm(mix @ w_out, g_mix_post)

    xn = rmsnorm(h, g_ffn_pre)
    a, b = jnp.split(xn @ w_ffn_up, 2, axis=-1)
    h = h + rmsnorm((jax.nn.silu(a) * b) @ w_ffn_down, g_ffn_post)
    return h


def run_trunk(x, meta_tokens, g_mix_pre, g_mix_post, g_ffn_pre, g_ffn_post, w_in,
              q_norm_b, k_norm_b, sink_a, w_branch_a, w_branch_b, w_out, w_ffn_up, w_ffn_down):
    B, n = x.shape[0], x.shape[1]
    meta = jnp.broadcast_to(meta_tokens[None].astype(x.dtype), (B, N_META, D_MODEL))
    h = jnp.concatenate([meta, x], axis=1)
    tabs = axial_rope_tables(n)
    slopes = alibi_slopes(HEADS_A)
    for l in range(DEPTH):
        h = encoder_layer(h, tabs, slopes, g_mix_pre[l], g_mix_post[l], g_ffn_pre[l], g_ffn_post[l],
                          w_in[l], q_norm_b[l], k_norm_b[l], sink_a[l], w_branch_a[l],
                          w_branch_b[l], w_out[l], w_ffn_up[l], w_ffn_down[l])
    return h[:, N_META:]


def setup_inputs(seed: int = 0) -> dict:
    key = jax.random.key(seed)
    ks = jax.random.split(key, 16)
    f32 = jnp.float32

    def w(k, shape, fan_in):
        return jax.random.normal(k, shape, f32) * fan_in ** -0.5

    def gain(k, shape):
        return 1.0 + 0.02 * jax.random.normal(k, shape, f32)

    return {
        "x_prompt": jax.random.normal(ks[0], (BATCH, SEQ, D_MODEL), f32),
        "x_sample": jax.random.normal(ks[1], (DEC_BATCH, DEC_SEQ, D_MODEL), f32),
        "meta_tokens": jax.random.normal(ks[2], (N_META, D_MODEL), f32),
        "g_mix_pre": gain(ks[3], (DEPTH, D_MODEL)),
        "g_mix_post": gain(ks[4], (DEPTH, D_MODEL)),
        "g_ffn_pre": gain(ks[5], (DEPTH, D_MODEL)),
        "g_ffn_post": gain(ks[6], (DEPTH, D_MODEL)),
        "w_in": w(ks[7], (DEPTH, D_MODEL, IN_COLS), D_MODEL),
        "q_norm_b": gain(ks[8], (DEPTH, HEAD_DIM)),
        "k_norm_b": gain(ks[9], (DEPTH, HEAD_DIM)),
        "sink_a": 0.5 * jax.random.normal(ks[10], (DEPTH, HEADS_A), f32),
        "w_branch_a": w(ks[11], (DEPTH, Q_A, D_MODEL), Q_A),
        "w_branch_b": w(ks[12], (DEPTH, Q_B, D_MODEL), Q_B),
        "w_out": w(ks[13], (DEPTH, D_MODEL, D_MODEL), D_MODEL),
        "w_ffn_up": w(ks[14], (DEPTH, D_MODEL, 2 * D_FF), D_MODEL),
        "w_ffn_down": w(ks[15], (DEPTH, D_FF, D_MODEL), D_FF),
    }


def reference(x_prompt, x_sample, meta_tokens, g_mix_pre, g_mix_post, g_ffn_pre, g_ffn_post,
              w_in, q_norm_b, k_norm_b, sink_a, w_branch_a, w_branch_b, w_out, w_ffn_up, w_ffn_down):
    y_prompt = run_trunk(x_prompt, meta_tokens, g_mix_pre, g_mix_post, g_ffn_pre, g_ffn_post, w_in,
                         q_norm_b, k_norm_b, sink_a, w_branch_a, w_branch_b, w_out, w_ffn_up, w_ffn_down)
    y_sample = run_trunk(x_sample, meta_tokens, g_mix_pre, g_mix_post, g_ffn_pre, g_ffn_post, w_in,
                         q_norm_b, k_norm_b, sink_a, w_branch_a, w_branch_b, w_out, w_ffn_up, w_ffn_down)
    return (y_prompt, y_sample)
```

```python
import functools
import math

import jax
import jax.numpy as jnp
import numpy as np
from jax import lax
from jax.experimental import pallas as pl
from jax.experimental.pallas import tpu as pltpu

D_MODEL = 1024
HEAD_DIM = 64
N_HEADS = 8
N_KV = 2
GROUP = N_HEADS // N_KV
Q_COLS = N_HEADS * HEAD_DIM
KV_COLS = N_KV * HEAD_DIM
N_META = 16
BLOCK = 128
GRID_W = 64
ROPE_BASE = 10000.0
ROPE_FREQS = HEAD_DIM // 4
D_FF = 2816
EPS = 1e-6
NEG_INF = -1e30
SCALE = HEAD_DIM ** -0.5

V_LANES = 128
TOKEN_TILE = 512
VMEM_LIMIT = 56 * 1024 * 1024

_C_QA = 0
_C_QB = _C_QA + Q_COLS
_C_VA = _C_QB + Q_COLS
_C_VB = _C_VA + N_KV * V_LANES
_C_GA = _C_VB + N_KV * V_LANES
_C_GB = _C_GA + D_MODEL
_C_K = _C_GB + D_MODEL
_C_END = _C_K + 2 * KV_COLS

_BF = jnp.bfloat16
_F32 = jnp.float32


def _dot(a, b):
    return jnp.dot(a, b, preferred_element_type=_F32)


def _params(sem, vmem=VMEM_LIMIT):
    return pltpu.CompilerParams(dimension_semantics=sem, vmem_limit_bytes=vmem)


def _const_spec(shape):
    nd = len(shape)
    return pl.BlockSpec(shape, lambda *_: (0,) * nd)


def _in_proj_body(h_ref, g_ref, w_ref, bd_ref, cos_ref, sin_ref, cost_ref, sint_ref, qg_ref, kg_ref,
                  qa_ref, qb_ref, kat_ref, kbt_ref, va_ref, vb_ref, ga_ref, gb_ref):
    h = h_ref[...]
    ms = jnp.mean(h * h, axis=-1, keepdims=True)
    xn = (h * lax.rsqrt(ms + EPS) * g_ref[...]).astype(_BF)

    qa = _dot(xn, w_ref[:, _C_QA:_C_QA + Q_COLS])
    qa_ref[...] = (qa * SCALE).astype(_BF)

    qb = _dot(xn, w_ref[:, _C_QB:_C_QB + Q_COLS])
    ssq = _dot((qb * qb).astype(_BF), bd_ref[...])
    qn = qb * lax.rsqrt(ssq * (1.0 / HEAD_DIM) + EPS) * qg_ref[...]
    lane = lax.broadcasted_iota(jnp.int32, qn.shape, 1)
    first = (lane & ROPE_FREQS) == 0
    partner = jnp.where(first, pltpu.roll(qn, Q_COLS - ROPE_FREQS, 1), pltpu.roll(qn, ROPE_FREQS, 1))
    cos = jnp.concatenate([cos_ref[...]] * (Q_COLS // V_LANES), axis=1)
    sin = jnp.concatenate([sin_ref[...]] * (Q_COLS // V_LANES), axis=1)
    qb_ref[...] = ((qn * cos + partner * sin) * SCALE).astype(_BF)

    ones_col = (lax.broadcasted_iota(jnp.int32, (1, N_KV * V_LANES), 1) % V_LANES == HEAD_DIM).astype(_F32)
    va = (_dot(xn, w_ref[:, _C_VA:_C_VA + N_KV * V_LANES]) + ones_col).astype(_BF)
    vb = (_dot(xn, w_ref[:, _C_VB:_C_VB + N_KV * V_LANES]) + ones_col).astype(_BF)
    for j in range(N_KV):
        va_ref[j] = va[:, j * V_LANES:(j + 1) * V_LANES]
        vb_ref[j] = vb[:, j * V_LANES:(j + 1) * V_LANES]

    ga_ref[...] = jax.nn.sigmoid(_dot(xn, w_ref[:, _C_GA:_C_GA + D_MODEL])).astype(_BF)
    gb_ref[...] = jax.nn.sigmoid(_dot(xn, w_ref[:, _C_GB:_C_GB + D_MODEL])).astype(_BF)

    kt = _dot(xn, w_ref[:, _C_K:_C_END]).T
    cost = cost_ref[...]
    sint = sint_ref[...]
    f = ROPE_FREQS
    for j in range(N_KV):
        kat_ref[j, 0] = kt[j * HEAD_DIM:(j + 1) * HEAD_DIM].astype(_BF)
        x = kt[KV_COLS + j * HEAD_DIM:KV_COLS + (j + 1) * HEAD_DIM]
        kms = jnp.mean(x * x, axis=0, keepdims=True)
        x = x * lax.rsqrt(kms + EPS) * kg_ref[...]
        partner_t = jnp.concatenate([x[f:2 * f], x[0:f], x[3 * f:4 * f], x[2 * f:3 * f]], axis=0)
        kbt_ref[j, 0] = (x * cost + partner_t * sint).astype(_BF)


def _in_proj(h, g, w, bd, cos, sin, cost, sint, qg, kg):
    ntok = h.shape[0]
    nch = ntok // TOKEN_TILE
    t = TOKEN_TILE
    row = lambda i: (i, 0)
    out_shape = (
        jax.ShapeDtypeStruct((ntok, Q_COLS), _BF),
        jax.ShapeDtypeStruct((ntok, Q_COLS), _BF),
        jax.ShapeDtypeStruct((N_KV, nch, HEAD_DIM, t), _BF),
        jax.ShapeDtypeStruct((N_KV, nch, HEAD_DIM, t), _BF),
        jax.ShapeDtypeStruct((N_KV, ntok, V_LANES), _BF),
        jax.ShapeDtypeStruct((N_KV, ntok, V_LANES), _BF),
        jax.ShapeDtypeStruct((ntok, D_MODEL), _BF),
        jax.ShapeDtypeStruct((ntok, D_MODEL), _BF),
    )
    kt_spec = pl.BlockSpec((N_KV, 1, HEAD_DIM, t), lambda i: (0, i, 0, 0))
    v_spec = pl.BlockSpec((N_KV, t, V_LANES), lambda i: (0, i, 0))
    return pl.pallas_call(
        _in_proj_body,
        grid=(nch,),
        in_specs=[
            pl.BlockSpec((t, D_MODEL), row),
            _const_spec((1, D_MODEL)),
            _const_spec((D_MODEL, _C_END)),
            _const_spec((Q_COLS, Q_COLS)),
            pl.BlockSpec((t, V_LANES), row),
            pl.BlockSpec((t, V_LANES), row),
            pl.BlockSpec((HEAD_DIM, t), lambda i: (0, i)),
            pl.BlockSpec((HEAD_DIM, t), lambda i: (0, i)),
            _const_spec((1, Q_COLS)),
            _const_spec((HEAD_DIM, 1)),
        ],
        out_specs=(
            pl.BlockSpec((t, Q_COLS), row), pl.BlockSpec((t, Q_COLS), row),
            kt_spec, kt_spec, v_spec, v_spec,
            pl.BlockSpec((t, D_MODEL), row), pl.BlockSpec((t, D_MODEL), row),
        ),
        out_shape=out_shape,
        compiler_params=_params(("parallel",)),
        name="in_proj",
    )(h, g, w, bd, cos, sin, cost, sint, qg, kg)


def _stack_heads(q, j):
    base = j * GROUP * HEAD_DIM
    return jnp.concatenate([q[:, base + g * HEAD_DIM: base + (g + 1) * HEAD_DIM] for g in range(GROUP)], axis=0)


def _sink_column(sink_ref, j, rows):
    return jnp.concatenate([jnp.full((rows, 1), sink_ref[j * GROUP + g], _F32) for g in range(GROUP)], axis=0)


def _window_body(geom, sink_ref, q_ref, kmain_ref, kprev_ref, knext_ref, vmain_ref, vprev_ref, vnext_ref,
                 kmeta_ref, vmeta_ref, bias_ref, _tail_ref, o_ref):
    t1, tpb1, tpb2 = geom
    t = pl.program_id(0)
    in_prompt = t < t1
    is_first = jnp.where(in_prompt, t % tpb1 == 0, (t - t1) % tpb2 == 0)
    is_last = jnp.where(in_prompt, t % tpb1 == tpb1 - 1, (t - t1) % tpb2 == tpb2 - 1)
    nblk = TOKEN_TILE // BLOCK
    col = lax.broadcasted_iota(jnp.int32, (1, 3 * BLOCK), 1)
    q_all = q_ref[...]
    for j in range(N_KV):
        kwin = jnp.concatenate([kprev_ref[j, 0], kmain_ref[j, 0], knext_ref[j, 0]], axis=1)
        vwin = jnp.concatenate([vprev_ref[j], vmain_ref[j], vnext_ref[j]], axis=0)
        kmeta = kmeta_ref[0, j]
        vmeta = vmeta_ref[0, j]
        bias = bias_ref[j]
        sink = _sink_column(sink_ref, j, BLOCK)
        for r in range(nblk):
            qs = _stack_heads(q_all[r * BLOCK:(r + 1) * BLOCK], j)
            s = _dot(qs, kwin[:, r * BLOCK:(r + 3) * BLOCK]) + bias
            if r == 0:
                s = jnp.where(col < jnp.where(is_first, BLOCK, 0), NEG_INF, s)
            if r == nblk - 1:
                s = jnp.where(col >= jnp.where(is_last, 2 * BLOCK, 3 * BLOCK), NEG_INF, s)
            sm = _dot(qs, kmeta)
            m = jnp.maximum(jnp.maximum(s.max(axis=1, keepdims=True), sm.max(axis=1, keepdims=True)), sink)
            p = jnp.exp(s - m).astype(_BF)
            pm = jnp.exp(sm - m).astype(_BF)
            acc = _dot(p, vwin[r * BLOCK:(r + 3) * BLOCK]) + _dot(pm, vmeta)
            l = acc[:, HEAD_DIM:HEAD_DIM + 1] + jnp.exp(sink - m)
            o = (acc[:, :HEAD_DIM] / l).astype(_BF)
            for g in range(GROUP):
                c0 = (j * GROUP + g) * HEAD_DIM
                o_ref[r * BLOCK:(r + 1) * BLOCK, c0:c0 + HEAD_DIM] = o[g * BLOCK:(g + 1) * BLOCK]


def _window_attention(sink, qa, kat, va, kmeta, vmeta, bias, tail, layout):
    n_real, b1, n1, b2, n2 = layout
    t = TOKEN_TILE
    nch = kat.shape[1]
    t1 = b1 * n1 // t
    tpb1, tpb2 = n1 // t, n2 // t
    sub = t // BLOCK

    def bid(i):
        return jnp.where(i < t1, i // tpb1, b1 + (i - t1) // tpb2)

    return pl.pallas_call(
        functools.partial(_window_body, (t1, tpb1, tpb2)),
        grid=(n_real // t,),
        in_specs=[
            pl.BlockSpec(memory_space=pltpu.SMEM),
            pl.BlockSpec((t, Q_COLS), lambda i: (i, 0)),
            pl.BlockSpec((N_KV, 1, HEAD_DIM, t), lambda i: (0, i, 0, 0)),
            pl.BlockSpec((N_KV, 1, HEAD_DIM, BLOCK), lambda i: (0, jnp.maximum(i - 1, 0), 0, sub - 1)),
            pl.BlockSpec((N_KV, 1, HEAD_DIM, BLOCK), lambda i: (0, jnp.minimum(i + 1, nch - 1), 0, 0)),
            pl.BlockSpec((N_KV, t, V_LANES), lambda i: (0, i, 0)),
            pl.BlockSpec((N_KV, BLOCK, V_LANES), lambda i: (0, jnp.maximum(i * sub - 1, 0), 0)),
            pl.BlockSpec((N_KV, BLOCK, V_LANES), lambda i: (0, jnp.minimum(i + 1, nch - 1) * sub, 0)),
            pl.BlockSpec((1, N_KV, HEAD_DIM, N_META), lambda i: (bid(i), 0, 0, 0)),
            pl.BlockSpec((1, N_KV, N_META, V_LANES), lambda i: (bid(i), 0, 0, 0)),
            _const_spec((N_KV, GROUP * BLOCK, 3 * BLOCK)),
            pl.BlockSpec(memory_space=pl.ANY),
        ],
        out_specs=pl.BlockSpec((t, Q_COLS), lambda i: (i, 0)),
        out_shape=jax.ShapeDtypeStruct(qa.shape, _BF),
        input_output_aliases={11: 0},
        compiler_params=_params(("parallel",)),
        name="window_attn",
    )(sink, qa, kat, kat, kat, va, va, va, kmeta, vmeta, bias, tail)


def _window_meta_body(nb, sink_ref, q_ref, kfirst_ref, vfirst_ref, kmeta_ref, vmeta_ref, o_ref):
    b = pl.program_id(0)

    @pl.when(b < nb)
    def _():
        q_all = q_ref[...]
        for j in range(N_KV):
            qs = _stack_heads(q_all, j)
            sink = _sink_column(sink_ref, j, N_META)
            sm = _dot(qs, kmeta_ref[0, j])
            sf = _dot(qs, kfirst_ref[j, 0])
            m = jnp.maximum(jnp.maximum(sm.max(axis=1, keepdims=True), sf.max(axis=1, keepdims=True)), sink)
            pm = jnp.exp(sm - m).astype(_BF)
            pf = jnp.exp(sf - m).astype(_BF)
            acc = _dot(pm, vmeta_ref[0, j]) + _dot(pf, vfirst_ref[j])
            l = acc[:, HEAD_DIM:HEAD_DIM + 1] + jnp.exp(sink - m)
            o = (acc[:, :HEAD_DIM] / l).astype(_BF)
            for g in range(GROUP):
                c0 = (j * GROUP + g) * HEAD_DIM
                o_ref[:, c0:c0 + HEAD_DIM] = o[g * N_META:(g + 1) * N_META]

    @pl.when(b >= nb)
    def _():
        o_ref[...] = jnp.zeros(o_ref.shape, o_ref.dtype)


def _window_meta_attention(sink, qa, kat, va, kmeta, vmeta, layout):
    n_real, b1, n1, b2, n2 = layout
    ntok = qa.shape[0]
    nb = b1 + b2
    t = TOKEN_TILE
    sub = t // BLOCK

    def start_chunk(b):
        bc = jnp.minimum(b, nb - 1)
        return jnp.where(bc < b1, bc * (n1 // t), b1 * (n1 // t) + (bc - b1) * (n2 // t))

    return pl.pallas_call(
        functools.partial(_window_meta_body, nb),
        grid=((ntok - n_real) // N_META,),
        in_specs=[
            pl.BlockSpec(memory_space=pltpu.SMEM),
            pl.BlockSpec((N_META, Q_COLS), lambda b: (n_real // N_META + b, 0)),
            pl.BlockSpec((N_KV, 1, HEAD_DIM, BLOCK), lambda b: (0, start_chunk(b), 0, 0)),
            pl.BlockSpec((N_KV, BLOCK, V_LANES), lambda b: (0, start_chunk(b) * sub, 0)),
            pl.BlockSpec((1, N_KV, HEAD_DIM, N_META), lambda b: (jnp.minimum(b, nb - 1), 0, 0, 0)),
            pl.BlockSpec((1, N_KV, N_META, V_LANES), lambda b: (jnp.minimum(b, nb - 1), 0, 0, 0)),
        ],
        out_specs=pl.BlockSpec((N_META, Q_COLS), lambda b: (n_real // N_META + b, 0)),
        out_shape=jax.ShapeDtypeStruct(qa.shape, _BF),
        compiler_params=_params(("arbitrary",)),
        name="window_meta_attn",
    )(sink, qa, kat, va, kmeta, vmeta)


def _global_body(nvalid, nchunks, tq, q_ref, kt_ref, v_ref, kmeta_ref, vmeta_ref, _prev_ref, o_ref,
                 qs_ref, m_ref, acc_ref):
    b = pl.program_id(0)

    @pl.when(b < nvalid)
    def _():
        q_all = q_ref[...]
        for g in range(GROUP):
            qs_ref[g * tq:(g + 1) * tq, :] = q_all[:, g * HEAD_DIM:(g + 1) * HEAD_DIM]
        qs = qs_ref[...]
        sm = _dot(qs, kmeta_ref[0, 0])
        m0 = sm.max(axis=1, keepdims=True)
        m_ref[...] = m0
        acc_ref[...] = _dot(jnp.exp(sm - m0).astype(_BF), vmeta_ref[0, 0])

        def step(c, carry):
            s = _dot(qs_ref[...], kt_ref[0, c])
            m_prev = m_ref[...]
            m_new = jnp.maximum(m_prev, s.max(axis=1, keepdims=True))
            p = jnp.exp(s - m_new).astype(_BF)
            r0 = pl.multiple_of(c * TOKEN_TILE, TOKEN_TILE)
            acc_ref[...] = jnp.exp(m_prev - m_new) * acc_ref[...] + _dot(p, v_ref[0, pl.ds(r0, TOKEN_TILE), :])
            m_ref[...] = m_new
            return carry

        lax.fori_loop(0, nchunks, step, 0)
        acc = acc_ref[...]
        o = (acc[:, :HEAD_DIM] / acc[:, HEAD_DIM:HEAD_DIM + 1]).astype(_BF)
        for g in range(GROUP):
            o_ref[:, g * HEAD_DIM:(g + 1) * HEAD_DIM] = o[g * tq:(g + 1) * tq]

    @pl.when(b >= nvalid)
    def _():
        o_ref[...] = jnp.zeros(o_ref.shape, o_ref.dtype)


def _global_attention(qb, kbt, vb, kmeta, vmeta, prev, *, tq, q_row0, q_rows_per_batch, grid_batches,
                      valid_batches, kv_batch0, kv_row0, n):
    nchunks = n // TOKEN_TILE
    qt = q_rows_per_batch // tq
    assert q_row0 % tq == 0 and kv_row0 % n == 0 and q_rows_per_batch % tq == 0
    kvb = lambda b: jnp.minimum(b, valid_batches - 1)
    qmap = lambda b, j, i: (q_row0 // tq + b * qt + i, j)
    in_specs = [
        pl.BlockSpec((tq, GROUP * HEAD_DIM), qmap),
        pl.BlockSpec((1, nchunks, HEAD_DIM, TOKEN_TILE), lambda b, j, i: (j, kv_row0 // n + kvb(b), 0, 0)),
        pl.BlockSpec((1, n, V_LANES), lambda b, j, i: (j, kv_row0 // n + kvb(b), 0)),
        pl.BlockSpec((1, 1, HEAD_DIM, N_META), lambda b, j, i: (kv_batch0 + kvb(b), j, 0, 0)),
        pl.BlockSpec((1, 1, N_META, V_LANES), lambda b, j, i: (kv_batch0 + kvb(b), j, 0, 0)),
    ]
    args = [qb, kbt, vb, kmeta, vmeta]
    aliases = {}
    if prev is not None:
        in_specs.append(pl.BlockSpec(memory_space=pl.ANY))
        args.append(prev)
        aliases = {5: 0}
    body = functools.partial(_global_body, valid_batches, nchunks, tq)
    if prev is None:
        body = functools.partial(_global_body_noprev, body)
    return pl.pallas_call(
        body,
        grid=(grid_batches, N_KV, qt),
        in_specs=in_specs,
        out_specs=pl.BlockSpec((tq, GROUP * HEAD_DIM), qmap),
        out_shape=jax.ShapeDtypeStruct(qb.shape, _BF),
        scratch_shapes=[
            pltpu.VMEM((GROUP * tq, HEAD_DIM), _BF),
            pltpu.VMEM((GROUP * tq, 1), _F32),
            pltpu.VMEM((GROUP * tq, V_LANES), _F32),
        ],
        input_output_aliases=aliases,
        compiler_params=_params(("parallel", "parallel", "arbitrary")),
        name=f"global_attn_tq{tq}_n{n}",
    )(*args)


def _global_body_noprev(body, q_ref, kt_ref, v_ref, kmeta_ref, vmeta_ref, o_ref, qs_ref, m_ref, acc_ref):
    body(q_ref, kt_ref, v_ref, kmeta_ref, vmeta_ref, None, o_ref, qs_ref, m_ref, acc_ref)


def _mix_body(h_ref, oa_ref, ob_ref, ga_ref, gb_ref, wa_ref, wb_ref, wo_ref, g_ref, o_ref):
    mix = (ga_ref[...].astype(_F32) * _dot(oa_ref[...], wa_ref[...])
           + gb_ref[...].astype(_F32) * _dot(ob_ref[...], wb_ref[...]))
    u = _dot(mix.astype(_BF), wo_ref[...])
    ms = jnp.mean(u * u, axis=-1, keepdims=True)
    o_ref[...] = h_ref[...] + u * lax.rsqrt(ms + EPS) * g_ref[...]


def _mix(h, oa, ob, ga, gb, wa, wb, wo, g):
    ntok = h.shape[0]
    t = TOKEN_TILE
    row = lambda i: (i, 0)
    return pl.pallas_call(
        _mix_body,
        grid=(ntok // t,),
        in_specs=[
            pl.BlockSpec((t, D_MODEL), row),
            pl.BlockSpec((t, Q_COLS), row), pl.BlockSpec((t, Q_COLS), row),
            pl.BlockSpec((t, D_MODEL), row), pl.BlockSpec((t, D_MODEL), row),
            _const_spec((Q_COLS, D_MODEL)), _const_spec((Q_COLS, D_MODEL)),
            _const_spec((D_MODEL, D_MODEL)), _const_spec((1, D_MODEL)),
        ],
        out_specs=pl.BlockSpec((t, D_MODEL), row),
        out_shape=jax.ShapeDtypeStruct(h.shape, _F32),
        input_output_aliases={0: 0},
        compiler_params=_params(("parallel",)),
        name="branch_mix",
    )(h, oa, ob, ga, gb, wa, wb, wo, g)


_FF_CHUNK = 256


def _ffn_body(h_ref, gpre_ref, wup_ref, wdown_ref, gpost_ref, o_ref, act_ref):
    h = h_ref[...]
    ms = jnp.mean(h * h, axis=-1, keepdims=True)
    xn = (h * lax.rsqrt(ms + EPS) * gpre_ref[...]).astype(_BF)
    for c in range(0, D_FF, _FF_CHUNK):
        a = _dot(xn, wup_ref[:, c:c + _FF_CHUNK])
        b = _dot(xn, wup_ref[:, D_FF + c:D_FF + c + _FF_CHUNK])
        act_ref[:, c:c + _FF_CHUNK] = (a * jax.nn.sigmoid(a) * b).astype(_BF)
    u = _dot(act_ref[...], wdown_ref[...])
    ms = jnp.mean(u * u, axis=-1, keepdims=True)
    o_ref[...] = h + u * lax.rsqrt(ms + EPS) * gpost_ref[...]


def _ffn(h, gpre, wup, wdown, gpost):
    ntok = h.shape[0]
    t = TOKEN_TILE
    row = lambda i: (i, 0)
    return pl.pallas_call(
        _ffn_body,
        grid=(ntok // t,),
        in_specs=[
            pl.BlockSpec((t, D_MODEL), row),
            _const_spec((1, D_MODEL)),
            _const_spec((D_MODEL, 2 * D_FF)),
            _const_spec((D_FF, D_MODEL)),
            _const_spec((1, D_MODEL)),
        ],
        out_specs=pl.BlockSpec((t, D_MODEL), row),
        out_shape=jax.ShapeDtypeStruct(h.shape, _F32),
        scratch_shapes=[pltpu.VMEM((t, D_FF), _BF)],
        input_output_aliases={0: 0},
        compiler_params=_params(("parallel",)),
        name="swiglu_ffn",
    )(h, gpre, wup, wdown, gpost)


def _rope_tables(layout, ntok):
    n_real, b1, n1, b2, n2 = layout
    idx = np.zeros((ntok,), np.int64)
    idx[:b1 * n1] = np.arange(b1 * n1) % n1
    idx[b1 * n1:n_real] = np.arange(b2 * n2) % n2
    rows = (idx // GRID_W).astype(np.float32)
    cols = (idx % GRID_W).astype(np.float32)
    rows[n_real:] = 0.0
    cols[n_real:] = 0.0
    freqs = ROPE_BASE ** (-jnp.arange(ROPE_FREQS, dtype=_F32) / ROPE_FREQS)
    ang_r = jnp.asarray(rows)[:, None] * freqs[None, :]
    ang_c = jnp.asarray(cols)[:, None] * freqs[None, :]
    cr, sr, cc, sc = jnp.cos(ang_r), jnp.sin(ang_r), jnp.cos(ang_c), jnp.sin(ang_c)
    cos64 = jnp.concatenate([cr, cr, cc, cc], axis=1)
    sin64 = jnp.concatenate([-sr, sr, -sc, sc], axis=1)
    return (jnp.concatenate([cos64, cos64], axis=1), jnp.concatenate([sin64, sin64], axis=1),
            cos64.T, sin64.T)


def _window_bias():
    slopes = 2.0 ** (-8.0 * np.arange(1, N_HEADS + 1, dtype=np.float64) / N_HEADS)
    rel = np.arange(BLOCK)[:, None] - (np.arange(3 * BLOCK) - BLOCK)[None, :]
    dist = np.abs(rel)
    band = dist <= BLOCK
    out = np.where(band[None], -slopes[:, None, None] * dist[None].astype(np.float64), NEG_INF)
    return jnp.asarray(out.reshape(N_KV, GROUP * BLOCK, 3 * BLOCK), _F32)


def _rearranged_w_in(w_in):
    o = np.cumsum([0, Q_COLS, KV_COLS, KV_COLS, Q_COLS, KV_COLS, KV_COLS, D_MODEL, D_MODEL])
    qa, ka, va, qb, kb, vb, ga, gb = [w_in[..., o[i]:o[i + 1]] for i in range(8)]

    def pad_v(v):
        z = jnp.zeros(v.shape[:-1] + (V_LANES - HEAD_DIM,), v.dtype)
        return jnp.concatenate([v[..., :HEAD_DIM], z, v[..., HEAD_DIM:], z], axis=-1)

    return jnp.concatenate([qa, qb, pad_v(va), pad_v(vb), ga, gb, ka, kb], axis=-1).astype(_BF)


def _meta_kv(kt, v, n_real, nb):
    tail = kt[:, n_real // TOKEN_TILE:]
    tail = jnp.moveaxis(tail, 2, 1).reshape(N_KV, HEAD_DIM, -1)[:, :, :nb * N_META]
    kmeta = jnp.transpose(tail.reshape(N_KV, HEAD_DIM, nb, N_META), (2, 0, 1, 3))
    vmeta = jnp.transpose(v[:, n_real:n_real + nb * N_META].reshape(N_KV, nb, N_META, V_LANES), (1, 0, 2, 3))
    return kmeta, vmeta


def kernel(x_prompt, x_sample, meta_tokens, g_mix_pre, g_mix_post, g_ffn_pre, g_ffn_post, w_in, q_norm_b,
           k_norm_b, sink_a, w_branch_a, w_branch_b, w_out, w_ffn_up, w_ffn_down):
    b1, n1, _ = x_prompt.shape
    b2, n2, _ = x_sample.shape
    depth = w_in.shape[0]
    t = TOKEN_TILE
    assert n1 % t == 0 and n2 % t == 0 and (b1 * n1) % n2 == 0 and n1 % GRID_W == 0 and n2 % GRID_W == 0
    nb = b1 + b2
    n_real = b1 * n1 + b2 * n2
    tail = -(-(nb * N_META) // t) * t
    ntok = n_real + tail
    layout = (n_real, b1, n1, b2, n2)

    h = jnp.concatenate([
        x_prompt.reshape(b1 * n1, D_MODEL), x_sample.reshape(b2 * n2, D_MODEL),
        jnp.tile(meta_tokens.astype(x_prompt.dtype), (nb, 1)),
        jnp.zeros((tail - nb * N_META, D_MODEL), x_prompt.dtype)], axis=0)

    cos, sin, cost, sint = _rope_tables(layout, ntok)
    bias = _window_bias()
    w_main = _rearranged_w_in(w_in)
    bd = jnp.asarray(np.kron(np.eye(N_HEADS), np.ones((HEAD_DIM, HEAD_DIM))), _BF)
    wa, wb, wo = w_branch_a.astype(_BF), w_branch_b.astype(_BF), w_out.astype(_BF)
    wup, wdown = w_ffn_up.astype(_BF), w_ffn_down.astype(_BF)
    row = lambda g: g.reshape(1, -1).astype(_F32)

    tq_real = 256
    for l in range(depth):
        qa, qb, kat, kbt, va, vb, ga, gb = _in_proj(
            h, row(g_mix_pre[l]), w_main[l], bd, cos, sin, cost, sint,
            jnp.tile(q_norm_b[l].astype(_F32), N_HEADS).reshape(1, Q_COLS),
            k_norm_b[l].astype(_F32).reshape(HEAD_DIM, 1))
        kameta, vameta = _meta_kv(kat, va, n_real, nb)
        kbmeta, vbmeta = _meta_kv(kbt, vb, n_real, nb)
        sink = sink_a[l].astype(_F32)

        oa = _window_meta_attention(sink, qa, kat, va, kameta, vameta, layout)
        oa = _window_attention(sink, qa, kat, va, kameta, vameta, bias, oa, layout)

        glob = functools.partial(_global_attention, qb, kbt, vb, kbmeta, vbmeta)
        ob = glob(None, tq=N_META, q_row0=n_real, q_rows_per_batch=N_META, grid_batches=b1,
                  valid_batches=b1, kv_batch0=0, kv_row0=0, n=n1)
        ob = glob(ob, tq=N_META, q_row0=n_real + b1 * N_META, q_rows_per_batch=N_META,
                  grid_batches=tail // N_META - b1, valid_batches=b2, kv_batch0=b1, kv_row0=b1 * n1, n=n2)
        ob = glob(ob, tq=tq_real, q_row0=0, q_rows_per_batch=n1, grid_batches=b1, valid_batches=b1,
                  kv_batch0=0, kv_row0=0, n=n1)
        ob = glob(ob, tq=tq_real, q_row0=b1 * n1, q_rows_per_batch=n2, grid_batches=b2, valid_batches=b2,
                  kv_batch0=b1, kv_row0=b1 * n1, n=n2)

        h = _mix(h, oa, ob, ga, gb, wa[l], wb[l], wo[l], row(g_mix_post[l]))
        h = _ffn(h, row(g_ffn_pre[l]), wup[l], wdown[l], row(g_ffn_post[l]))

    y_prompt = h[:b1 * n1].reshape(b1, n1, D_MODEL)
    y_sample = h[b1 * n1:n_real].reshape(b2, n2, D_MODEL)
    return (y_prompt, y_sample)
```

```python
import functools
import math

import jax
import jax.numpy as jnp
import numpy as np
from jax import lax
from jax.experimental import pallas as pl
from jax.experimental.pallas import tpu as pltpu

D_MODEL = 1024
HEAD_DIM = 64
N_HEADS = 8
N_KV = 2
GROUP = N_HEADS // N_KV
Q_COLS = N_HEADS * HEAD_DIM
KV_COLS = N_KV * HEAD_DIM
N_META = 16
BLOCK = 128
GRID_W = 64
ROPE_BASE = 10000.0
ROPE_FREQS = HEAD_DIM // 4
D_FF = 2816
EPS = 1e-6
NEG_INF = -1e30
SCALE = HEAD_DIM ** -0.5
LOG2E = math.log2(math.e)

V_LANES = 128
TOKEN_TILE = 512
VMEM_LIMIT = 56 * 1024 * 1024

_C_QA = 0
_C_QB = _C_QA + Q_COLS
_C_VA = _C_QB + Q_COLS
_C_VB = _C_VA + N_KV * V_LANES
_C_GA = _C_VB + N_KV * V_LANES
_C_GB = _C_GA + D_MODEL
_C_K = _C_GB + D_MODEL
_C_END = _C_K + 2 * KV_COLS

_BF = jnp.bfloat16
_F32 = jnp.float32


def _dot(a, b):
    return jnp.dot(a, b, preferred_element_type=_F32)


def _params(sem, vmem=VMEM_LIMIT):
    return pltpu.CompilerParams(dimension_semantics=sem, vmem_limit_bytes=vmem)


def _const_spec(shape):
    nd = len(shape)
    return pl.BlockSpec(shape, lambda *_: (0,) * nd)


def _in_proj_body(h_ref, g_ref, w_ref, bd_ref, cos_ref, sin_ref, cost_ref, sint_ref, qg_ref, kg_ref,
                  qa_ref, qb_ref, kat_ref, kbt_ref, va_ref, vb_ref, ga_ref, gb_ref):
    h = h_ref[...]
    ms = jnp.mean(h * h, axis=-1, keepdims=True)
    xn = (h * lax.rsqrt(ms + EPS) * g_ref[...]).astype(_BF)

    qa = _dot(xn, w_ref[:, _C_QA:_C_QA + Q_COLS])
    qa_ref[...] = (qa * SCALE).astype(_BF)

    qb = _dot(xn, w_ref[:, _C_QB:_C_QB + Q_COLS])
    ssq = _dot((qb * qb).astype(_BF), bd_ref[...])
    qn = qb * lax.rsqrt(ssq * (1.0 / HEAD_DIM) + EPS) * qg_ref[...]
    lane = lax.broadcasted_iota(jnp.int32, qn.shape, 1)
    first = (lane & ROPE_FREQS) == 0
    partner = jnp.where(first, pltpu.roll(qn, Q_COLS - ROPE_FREQS, 1), pltpu.roll(qn, ROPE_FREQS, 1))
    cos = jnp.concatenate([cos_ref[...]] * (Q_COLS // V_LANES), axis=1)
    sin = jnp.concatenate([sin_ref[...]] * (Q_COLS // V_LANES), axis=1)
    qb_ref[...] = ((qn * cos + partner * sin) * (SCALE * LOG2E)).astype(_BF)

    ones_col = (lax.broadcasted_iota(jnp.int32, (1, N_KV * V_LANES), 1) % V_LANES == HEAD_DIM).astype(_F32)
    va = (_dot(xn, w_ref[:, _C_VA:_C_VA + N_KV * V_LANES]) + ones_col).astype(_BF)
    vb = (_dot(xn, w_ref[:, _C_VB:_C_VB + N_KV * V_LANES]) + ones_col).astype(_BF)
    for j in range(N_KV):
        va_ref[j] = va[:, j * V_LANES:(j + 1) * V_LANES]
        vb_ref[j] = vb[:, j * V_LANES:(j + 1) * V_LANES]

    ga_ref[...] = jax.nn.sigmoid(_dot(xn, w_ref[:, _C_GA:_C_GA + D_MODEL])).astype(_BF)
    gb_ref[...] = jax.nn.sigmoid(_dot(xn, w_ref[:, _C_GB:_C_GB + D_MODEL])).astype(_BF)

    kt = _dot(xn, w_ref[:, _C_K:_C_END]).T
    cost = cost_ref[...]
    sint = sint_ref[...]
    f = ROPE_FREQS
    for j in range(N_KV):
        kat_ref[j, 0] = kt[j * HEAD_DIM:(j + 1) * HEAD_DIM].astype(_BF)
        x = kt[KV_COLS + j * HEAD_DIM:KV_COLS + (j + 1) * HEAD_DIM]
        kms = jnp.mean(x * x, axis=0, keepdims=True)
        x = x * lax.rsqrt(kms + EPS) * kg_ref[...]
        partner_t = jnp.concatenate([x[f:2 * f], x[0:f], x[3 * f:4 * f], x[2 * f:3 * f]], axis=0)
        kbt_ref[j, 0] = (x * cost + partner_t * sint).astype(_BF)


def _in_proj(h, g, w, bd, cos, sin, cost, sint, qg, kg):
    ntok = h.shape[0]
    nch = ntok // TOKEN_TILE
    t = TOKEN_TILE
    row = lambda i: (i, 0)
    out_shape = (
        jax.ShapeDtypeStruct((ntok, Q_COLS), _BF),
        jax.ShapeDtypeStruct((ntok, Q_COLS), _BF),
        jax.ShapeDtypeStruct((N_KV, nch, HEAD_DIM, t), _BF),
        jax.ShapeDtypeStruct((N_KV, nch, HEAD_DIM, t), _BF),
        jax.ShapeDtypeStruct((N_KV, ntok, V_LANES), _BF),
        jax.ShapeDtypeStruct((N_KV, ntok, V_LANES), _BF),
        jax.ShapeDtypeStruct((ntok, D_MODEL), _BF),
        jax.ShapeDtypeStruct((ntok, D_MODEL), _BF),
    )
    kt_spec = pl.BlockSpec((N_KV, 1, HEAD_DIM, t), lambda i: (0, i, 0, 0))
    v_spec = pl.BlockSpec((N_KV, t, V_LANES), lambda i: (0, i, 0))
    return pl.pallas_call(
        _in_proj_body,
        grid=(nch,),
        in_specs=[
            pl.BlockSpec((t, D_MODEL), row),
            _const_spec((1, D_MODEL)),
            _const_spec((D_MODEL, _C_END)),
            _const_spec((Q_COLS, Q_COLS)),
            pl.BlockSpec((t, V_LANES), row),
            pl.BlockSpec((t, V_LANES), row),
            pl.BlockSpec((HEAD_DIM, t), lambda i: (0, i)),
            pl.BlockSpec((HEAD_DIM, t), lambda i: (0, i)),
            _const_spec((1, Q_COLS)),
            _const_spec((HEAD_DIM, 1)),
        ],
        out_specs=(
            pl.BlockSpec((t, Q_COLS), row), pl.BlockSpec((t, Q_COLS), row),
            kt_spec, kt_spec, v_spec, v_spec,
            pl.BlockSpec((t, D_MODEL), row), pl.BlockSpec((t, D_MODEL), row),
        ),
        out_shape=out_shape,
        compiler_params=_params(("parallel",)),
        name="in_proj",
    )(h, g, w, bd, cos, sin, cost, sint, qg, kg)


def _stack_heads(q, j):
    base = j * GROUP * HEAD_DIM
    return jnp.concatenate([q[:, base + g * HEAD_DIM: base + (g + 1) * HEAD_DIM] for g in range(GROUP)], axis=0)


def _sink_column(sink_ref, j, rows):
    return jnp.concatenate([jnp.full((rows, 1), sink_ref[j * GROUP + g], _F32) for g in range(GROUP)], axis=0)


def _window_body(geom, sink_ref, q_ref, kmain_ref, kprev_ref, knext_ref, vmain_ref, vprev_ref, vnext_ref,
                 kmeta_ref, vmeta_ref, bias_ref, _tail_ref, o_ref):
    t1, tpb1, tpb2 = geom
    t = pl.program_id(0)
    in_prompt = t < t1
    is_first = jnp.where(in_prompt, t % tpb1 == 0, (t - t1) % tpb2 == 0)
    is_last = jnp.where(in_prompt, t % tpb1 == tpb1 - 1, (t - t1) % tpb2 == tpb2 - 1)
    nblk = TOKEN_TILE // BLOCK
    col = lax.broadcasted_iota(jnp.int32, (1, 3 * BLOCK), 1)
    q_all = q_ref[...]
    for j in range(N_KV):
        kwin = jnp.concatenate([kprev_ref[j, 0], kmain_ref[j, 0], knext_ref[j, 0]], axis=1)
        vwin = jnp.concatenate([vprev_ref[j], vmain_ref[j], vnext_ref[j]], axis=0)
        kmeta = kmeta_ref[0, j]
        vmeta = vmeta_ref[0, j]
        bias = bias_ref[j]
        sink = _sink_column(sink_ref, j, BLOCK)
        for r in range(nblk):
            qs = _stack_heads(q_all[r * BLOCK:(r + 1) * BLOCK], j)
            s = _dot(qs, kwin[:, r * BLOCK:(r + 3) * BLOCK]) + bias
            if r == 0:
                s = jnp.where(col < jnp.where(is_first, BLOCK, 0), NEG_INF, s)
            if r == nblk - 1:
                s = jnp.where(col >= jnp.where(is_last, 2 * BLOCK, 3 * BLOCK), NEG_INF, s)
            sm = _dot(qs, kmeta)
            m = jnp.maximum(jnp.maximum(s.max(axis=1, keepdims=True), sm.max(axis=1, keepdims=True)), sink)
            p = jnp.exp(s - m).astype(_BF)
            pm = jnp.exp(sm - m).astype(_BF)
            acc = _dot(p, vwin[r * BLOCK:(r + 3) * BLOCK]) + _dot(pm, vmeta)
            l = acc[:, HEAD_DIM:HEAD_DIM + 1] + jnp.exp(sink - m)
            o = (acc[:, :HEAD_DIM] / l).astype(_BF)
            for g in range(GROUP):
                c0 = (j * GROUP + g) * HEAD_DIM
                o_ref[r * BLOCK:(r + 1) * BLOCK, c0:c0 + HEAD_DIM] = o[g * BLOCK:(g + 1) * BLOCK]


def _window_attention(sink, qa, kat, va, kmeta, vmeta, bias, tail, layout):
    n_real, b1, n1, b2, n2 = layout
    t = TOKEN_TILE
    nch = kat.shape[1]
    t1 = b1 * n1 // t
    tpb1, tpb2 = n1 // t, n2 // t
    sub = t // BLOCK

    def bid(i):
        return jnp.where(i < t1, i // tpb1, b1 + (i - t1) // tpb2)

    return pl.pallas_call(
        functools.partial(_window_body, (t1, tpb1, tpb2)),
        grid=(n_real // t,),
        in_specs=[
            pl.BlockSpec(memory_space=pltpu.SMEM),
            pl.BlockSpec((t, Q_COLS), lambda i: (i, 0)),
            pl.BlockSpec((N_KV, 1, HEAD_DIM, t), lambda i: (0, i, 0, 0)),
            pl.BlockSpec((N_KV, 1, HEAD_DIM, BLOCK), lambda i: (0, jnp.maximum(i - 1, 0), 0, sub - 1)),
            pl.BlockSpec((N_KV, 1, HEAD_DIM, BLOCK), lambda i: (0, jnp.minimum(i + 1, nch - 1), 0, 0)),
            pl.BlockSpec((N_KV, t, V_LANES), lambda i: (0, i, 0)),
            pl.BlockSpec((N_KV, BLOCK, V_LANES), lambda i: (0, jnp.maximum(i * sub - 1, 0), 0)),
            pl.BlockSpec((N_KV, BLOCK, V_LANES), lambda i: (0, jnp.minimum(i + 1, nch - 1) * sub, 0)),
            pl.BlockSpec((1, N_KV, HEAD_DIM, N_META), lambda i: (bid(i), 0, 0, 0)),
            pl.BlockSpec((1, N_KV, N_META, V_LANES), lambda i: (bid(i), 0, 0, 0)),
            _const_spec((N_KV, GROUP * BLOCK, 3 * BLOCK)),
            pl.BlockSpec(memory_space=pl.ANY),
        ],
        out_specs=pl.BlockSpec((t, Q_COLS), lambda i: (i, 0)),
        out_shape=jax.ShapeDtypeStruct(qa.shape, _BF),
        input_output_aliases={11: 0},
        compiler_params=_params(("parallel",)),
        name="window_attn",
    )(sink, qa, kat, kat, kat, va, va, va, kmeta, vmeta, bias, tail)


def _window_meta_body(nb, sink_ref, q_ref, kfirst_ref, vfirst_ref, kmeta_ref, vmeta_ref, o_ref):
    b = pl.program_id(0)

    @pl.when(b < nb)
    def _():
        q_all = q_ref[...]
        for j in range(N_KV):
            qs = _stack_heads(q_all, j)
            sink = _sink_column(sink_ref, j, N_META)
            sm = _dot(qs, kmeta_ref[0, j])
            sf = _dot(qs, kfirst_ref[j, 0])
            m = jnp.maximum(jnp.maximum(sm.max(axis=1, keepdims=True), sf.max(axis=1, keepdims=True)), sink)
            pm = jnp.exp(sm - m).astype(_BF)
            pf = jnp.exp(sf - m).astype(_BF)
            acc = _dot(pm, vmeta_ref[0, j]) + _dot(pf, vfirst_ref[j])
            l = acc[:, HEAD_DIM:HEAD_DIM + 1] + jnp.exp(sink - m)
            o = (acc[:, :HEAD_DIM] / l).astype(_BF)
            for g in range(GROUP):
                c0 = (j * GROUP + g) * HEAD_DIM
                o_ref[:, c0:c0 + HEAD_DIM] = o[g * N_META:(g + 1) * N_META]

    @pl.when(b >= nb)
    def _():
        o_ref[...] = jnp.zeros(o_ref.shape, o_ref.dtype)


def _window_meta_attention(sink, qa, kat, va, kmeta, vmeta, layout):
    n_real, b1, n1, b2, n2 = layout
    ntok = qa.shape[0]
    nb = b1 + b2
    t = TOKEN_TILE
    sub = t // BLOCK

    def start_chunk(b):
        bc = jnp.minimum(b, nb - 1)
        return jnp.where(bc < b1, bc * (n1 // t), b1 * (n1 // t) + (bc - b1) * (n2 // t))

    return pl.pallas_call(
        functools.partial(_window_meta_body, nb),
        grid=((ntok - n_real) // N_META,),
        in_specs=[
            pl.BlockSpec(memory_space=pltpu.SMEM),
            pl.BlockSpec((N_META, Q_COLS), lambda b: (n_real // N_META + b, 0)),
            pl.BlockSpec((N_KV, 1, HEAD_DIM, BLOCK), lambda b: (0, start_chunk(b), 0, 0)),
            pl.BlockSpec((N_KV, BLOCK, V_LANES), lambda b: (0, start_chunk(b) * sub, 0)),
            pl.BlockSpec((1, N_KV, HEAD_DIM, N_META), lambda b: (jnp.minimum(b, nb - 1), 0, 0, 0)),
            pl.BlockSpec((1, N_KV, N_META, V_LANES), lambda b: (jnp.minimum(b, nb - 1), 0, 0, 0)),
        ],
        out_specs=pl.BlockSpec((N_META, Q_COLS), lambda b: (n_real // N_META + b, 0)),
        out_shape=jax.ShapeDtypeStruct(qa.shape, _BF),
        compiler_params=_params(("arbitrary",)),
        name="window_meta_attn",
    )(sink, qa, kat, va, kmeta, vmeta)


def _global_body(nvalid, nchunks, tq, q_ref, kt_ref, v_ref, kmeta_ref, vmeta_ref, _prev_ref, o_ref,
                 qs_ref, s0_ref, s1_ref, m_ref, acc_ref):
    b = pl.program_id(0)
    lane_tiles = TOKEN_TILE // V_LANES

    def scores(c, s_ref):
        s_ref[...] = _dot(qs_ref[...], kt_ref[0, c])

    def softmax_pv(c, s_ref):
        s = s_ref[...]
        m_prev = m_ref[...]
        m_new = jnp.maximum(m_prev, jnp.broadcast_to(s.max(axis=1, keepdims=True), m_prev.shape))
        p = jnp.exp2(s - jnp.concatenate([m_new] * lane_tiles, axis=1)).astype(_BF)
        r0 = pl.multiple_of(c * TOKEN_TILE, TOKEN_TILE)
        acc_ref[...] = jnp.exp2(m_prev - m_new) * acc_ref[...] + _dot(p, v_ref[0, pl.ds(r0, TOKEN_TILE), :])
        m_ref[...] = m_new

    @pl.when(b < nvalid)
    def _():
        q_all = q_ref[...]
        for g in range(GROUP):
            qs_ref[g * tq:(g + 1) * tq, :] = q_all[:, g * HEAD_DIM:(g + 1) * HEAD_DIM]
        qs = qs_ref[...]
        sm = _dot(qs, kmeta_ref[0, 0])
        m0 = sm.max(axis=1, keepdims=True)
        m_ref[...] = jnp.broadcast_to(m0, m_ref.shape)
        acc_ref[...] = _dot(jnp.exp2(sm - m0).astype(_BF), vmeta_ref[0, 0])
        scores(0, s0_ref)

        def pair(i, carry):
            c = 2 * i
            scores(c + 1, s1_ref)
            softmax_pv(c, s0_ref)
            scores(c + 2, s0_ref)
            softmax_pv(c + 1, s1_ref)
            return carry

        lax.fori_loop(0, (nchunks - 1) // 2, pair, 0)
        if nchunks % 2 == 0:
            scores(nchunks - 1, s1_ref)
            softmax_pv(nchunks - 2, s0_ref)
            softmax_pv(nchunks - 1, s1_ref)
        else:
            softmax_pv(nchunks - 1, s0_ref)
        acc = acc_ref[...]
        o = (acc[:, :HEAD_DIM] / acc[:, HEAD_DIM:HEAD_DIM + 1]).astype(_BF)
        for g in range(GROUP):
            o_ref[:, g * HEAD_DIM:(g + 1) * HEAD_DIM] = o[g * tq:(g + 1) * tq]

    @pl.when(b >= nvalid)
    def _():
        o_ref[...] = jnp.zeros(o_ref.shape, o_ref.dtype)


def _global_attention(qb, kbt, vb, kmeta, vmeta, prev, *, tq, q_row0, q_rows_per_batch, grid_batches,
                      valid_batches, kv_batch0, kv_row0, n):
    nchunks = n // TOKEN_TILE
    qt = q_rows_per_batch // tq
    assert q_row0 % tq == 0 and kv_row0 % n == 0 and q_rows_per_batch % tq == 0
    kvb = lambda b: jnp.minimum(b, valid_batches - 1)
    qmap = lambda b, j, i: (q_row0 // tq + b * qt + i, j)
    in_specs = [
        pl.BlockSpec((tq, GROUP * HEAD_DIM), qmap),
        pl.BlockSpec((1, nchunks, HEAD_DIM, TOKEN_TILE), lambda b, j, i: (j, kv_row0 // n + kvb(b), 0, 0)),
        pl.BlockSpec((1, n, V_LANES), lambda b, j, i: (j, kv_row0 // n + kvb(b), 0)),
        pl.BlockSpec((1, 1, HEAD_DIM, N_META), lambda b, j, i: (kv_batch0 + kvb(b), j, 0, 0)),
        pl.BlockSpec((1, 1, N_META, V_LANES), lambda b, j, i: (kv_batch0 + kvb(b), j, 0, 0)),
    ]
    args = [qb, kbt, vb, kmeta, vmeta]
    aliases = {}
    if prev is not None:
        in_specs.append(pl.BlockSpec(memory_space=pl.ANY))
        args.append(prev)
        aliases = {5: 0}
    body = functools.partial(_global_body, valid_batches, nchunks, tq)
    if prev is None:
        body = functools.partial(_global_body_noprev, body)
    return pl.pallas_call(
        body,
        grid=(grid_batches, N_KV, qt),
        in_specs=in_specs,
        out_specs=pl.BlockSpec((tq, GROUP * HEAD_DIM), qmap),
        out_shape=jax.ShapeDtypeStruct(qb.shape, _BF),
        scratch_shapes=[
            pltpu.VMEM((GROUP * tq, HEAD_DIM), _BF),
            pltpu.VMEM((GROUP * tq, TOKEN_TILE), _F32),
            pltpu.VMEM((GROUP * tq, TOKEN_TILE), _F32),
            pltpu.VMEM((GROUP * tq, V_LANES), _F32),
            pltpu.VMEM((GROUP * tq, V_LANES), _F32),
        ],
        input_output_aliases=aliases,
        compiler_params=_params(("parallel", "parallel", "arbitrary")),
        name=f"global_attn_tq{tq}_n{n}",
    )(*args)


def _global_body_noprev(body, q_ref, kt_ref, v_ref, kmeta_ref, vmeta_ref, o_ref, *scratch):
    body(q_ref, kt_ref, v_ref, kmeta_ref, vmeta_ref, None, o_ref, *scratch)


def _mix_body(h_ref, oa_ref, ob_ref, ga_ref, gb_ref, wa_ref, wb_ref, wo_ref, g_ref, o_ref):
    mix = (ga_ref[...].astype(_F32) * _dot(oa_ref[...], wa_ref[...])
           + gb_ref[...].astype(_F32) * _dot(ob_ref[...], wb_ref[...]))
    u = _dot(mix.astype(_BF), wo_ref[...])
    ms = jnp.mean(u * u, axis=-1, keepdims=True)
    o_ref[...] = h_ref[...] + u * lax.rsqrt(ms + EPS) * g_ref[...]


def _mix(h, oa, ob, ga, gb, wa, wb, wo, g):
    ntok = h.shape[0]
    t = TOKEN_TILE
    row = lambda i: (i, 0)
    return pl.pallas_call(
        _mix_body,
        grid=(ntok // t,),
        in_specs=[
            pl.BlockSpec((t, D_MODEL), row),
            pl.BlockSpec((t, Q_COLS), row), pl.BlockSpec((t, Q_COLS), row),
            pl.BlockSpec((t, D_MODEL), row), pl.BlockSpec((t, D_MODEL), row),
            _const_spec((Q_COLS, D_MODEL)), _const_spec((Q_COLS, D_MODEL)),
            _const_spec((D_MODEL, D_MODEL)), _const_spec((1, D_MODEL)),
        ],
        out_specs=pl.BlockSpec((t, D_MODEL), row),
        out_shape=jax.ShapeDtypeStruct(h.shape, _F32),
        input_output_aliases={0: 0},
        compiler_params=_params(("parallel",)),
        name="branch_mix",
    )(h, oa, ob, ga, gb, wa, wb, wo, g)


_FF_CHUNK = 256


def _ffn_body(h_ref, gpre_ref, wup_ref, wdown_ref, gpost_ref, o_ref, act_ref):
    h = h_ref[...]
    ms = jnp.mean(h * h, axis=-1, keepdims=True)
    xn = (h * lax.rsqrt(ms + EPS) * gpre_ref[...]).astype(_BF)
    for c in range(0, D_FF, _FF_CHUNK):
        a = _dot(xn, wup_ref[:, c:c + _FF_CHUNK])
        b = _dot(xn, wup_ref[:, D_FF + c:D_FF + c + _FF_CHUNK])
        act_ref[:, c:c + _FF_CHUNK] = (a * jax.nn.sigmoid(a) * b).astype(_BF)
    u = _dot(act_ref[...], wdown_ref[...])
    ms = jnp.mean(u * u, axis=-1, keepdims=True)
    o_ref[...] = h + u * lax.rsqrt(ms + EPS) * gpost_ref[...]


def _ffn(h, gpre, wup, wdown, gpost):
    ntok = h.shape[0]
    t = TOKEN_TILE
    row = lambda i: (i, 0)
    return pl.pallas_call(
        _ffn_body,
        grid=(ntok // t,),
        in_specs=[
            pl.BlockSpec((t, D_MODEL), row),
            _const_spec((1, D_MODEL)),
            _const_spec((D_MODEL, 2 * D_FF)),
            _const_spec((D_FF, D_MODEL)),
            _const_spec((1, D_MODEL)),
        ],
        out_specs=pl.BlockSpec((t, D_MODEL), row),
        out_shape=jax.ShapeDtypeStruct(h.shape, _F32),
        scratch_shapes=[pltpu.VMEM((t, D_FF), _BF)],
        input_output_aliases={0: 0},
        compiler_params=_params(("parallel",)),
        name="swiglu_ffn",
    )(h, gpre, wup, wdown, gpost)


def _rope_tables(layout, ntok):
    n_real, b1, n1, b2, n2 = layout
    idx = np.zeros((ntok,), np.int64)
    idx[:b1 * n1] = np.arange(b1 * n1) % n1
    idx[b1 * n1:n_real] = np.arange(b2 * n2) % n2
    rows = (idx // GRID_W).astype(np.float32)
    cols = (idx % GRID_W).astype(np.float32)
    rows[n_real:] = 0.0
    cols[n_real:] = 0.0
    freqs = ROPE_BASE ** (-jnp.arange(ROPE_FREQS, dtype=_F32) / ROPE_FREQS)
    ang_r = jnp.asarray(rows)[:, None] * freqs[None, :]
    ang_c = jnp.asarray(cols)[:, None] * freqs[None, :]
    cr, sr, cc, sc = jnp.cos(ang_r), jnp.sin(ang_r), jnp.cos(ang_c), jnp.sin(ang_c)
    cos64 = jnp.concatenate([cr, cr, cc, cc], axis=1)
    sin64 = jnp.concatenate([-sr, sr, -sc, sc], axis=1)
    return (jnp.concatenate([cos64, cos64], axis=1), jnp.concatenate([sin64, sin64], axis=1),
            cos64.T, sin64.T)


def _window_bias():
    slopes = 2.0 ** (-8.0 * np.arange(1, N_HEADS + 1, dtype=np.float64) / N_HEADS)
    rel = np.arange(BLOCK)[:, None] - (np.arange(3 * BLOCK) - BLOCK)[None, :]
    dist = np.abs(rel)
    band = dist <= BLOCK
    out = np.where(band[None], -slopes[:, None, None] * dist[None].astype(np.float64), NEG_INF)
    return jnp.asarray(out.reshape(N_KV, GROUP * BLOCK, 3 * BLOCK), _F32)


def _rearranged_w_in(w_in):
    o = np.cumsum([0, Q_COLS, KV_COLS, KV_COLS, Q_COLS, KV_COLS, KV_COLS, D_MODEL, D_MODEL])
    qa, ka, va, qb, kb, vb, ga, gb = [w_in[..., o[i]:o[i + 1]] for i in range(8)]

    def pad_v(v):
        z = jnp.zeros(v.shape[:-1] + (V_LANES - HEAD_DIM,), v.dtype)
        return jnp.concatenate([v[..., :HEAD_DIM], z, v[..., HEAD_DIM:], z], axis=-1)

    return jnp.concatenate([qa, qb, pad_v(va), pad_v(vb), ga, gb, ka, kb], axis=-1).astype(_BF)


def _meta_kv(kt, v, n_real, nb):
    tail = kt[:, n_real // TOKEN_TILE:]
    tail = jnp.moveaxis(tail, 2, 1).reshape(N_KV, HEAD_DIM, -1)[:, :, :nb * N_META]
    kmeta = jnp.transpose(tail.reshape(N_KV, HEAD_DIM, nb, N_META), (2, 0, 1, 3))
    vmeta = jnp.transpose(v[:, n_real:n_real + nb * N_META].reshape(N_KV, nb, N_META, V_LANES), (1, 0, 2, 3))
    return kmeta, vmeta


def kernel(x_prompt, x_sample, meta_tokens, g_mix_pre, g_mix_post, g_ffn_pre, g_ffn_post, w_in, q_norm_b,
           k_norm_b, sink_a, w_branch_a, w_branch_b, w_out, w_ffn_up, w_ffn_down):
    b1, n1, _ = x_prompt.shape
    b2, n2, _ = x_sample.shape
    depth = w_in.shape[0]
    t = TOKEN_TILE
    assert n1 % t == 0 and n2 % t == 0 and (b1 * n1) % n2 == 0 and n1 % GRID_W == 0 and n2 % GRID_W == 0
    nb = b1 + b2
    n_real = b1 * n1 + b2 * n2
    tail = -(-(nb * N_META) // t) * t
    ntok = n_real + tail
    layout = (n_real, b1, n1, b2, n2)

    h = jnp.concatenate([
        x_prompt.reshape(b1 * n1, D_MODEL), x_sample.reshape(b2 * n2, D_MODEL),
        jnp.tile(meta_tokens.astype(x_prompt.dtype), (nb, 1)),
        jnp.zeros((tail - nb * N_META, D_MODEL), x_prompt.dtype)], axis=0)

    cos, sin, cost, sint = _rope_tables(layout, ntok)
    bias = _window_bias()
    w_main = _rearranged_w_in(w_in)
    bd = jnp.asarray(np.kron(np.eye(N_HEADS), np.ones((HEAD_DIM, HEAD_DIM))), _BF)
    wa, wb, wo = w_branch_a.astype(_BF), w_branch_b.astype(_BF), w_out.astype(_BF)
    wup, wdown = w_ffn_up.astype(_BF), w_ffn_down.astype(_BF)
    row = lambda g: g.reshape(1, -1).astype(_F32)

    tq_real = 256
    for l in range(depth):
        qa, qb, kat, kbt, va, vb, ga, gb = _in_proj(
            h, row(g_mix_pre[l]), w_main[l], bd, cos, sin, cost, sint,
            jnp.tile(q_norm_b[l].astype(_F32), N_HEADS).reshape(1, Q_COLS),
            k_norm_b[l].astype(_F32).reshape(HEAD_DIM, 1))
        kameta, vameta = _meta_kv(kat, va, n_real, nb)
        kbmeta, vbmeta = _meta_kv(kbt, vb, n_real, nb)
        sink = sink_a[l].astype(_F32)

        oa = _window_meta_attention(sink, qa, kat, va, kameta, vameta, layout)
        oa = _window_attention(sink, qa, kat, va, kameta, vameta, bias, oa, layout)

        glob = functools.partial(_global_attention, qb, kbt, vb, kbmeta, vbmeta)
        ob = glob(None, tq=N_META, q_row0=n_real, q_rows_per_batch=N_META, grid_batches=b1,
                  valid_batches=b1, kv_batch0=0, kv_row0=0, n=n1)
        ob = glob(ob, tq=N_META, q_row0=n_real + b1 * N_META, q_rows_per_batch=N_META,
                  grid_batches=tail // N_META - b1, valid_batches=b2, kv_batch0=b1, kv_row0=b1 * n1, n=n2)
        ob = glob(ob, tq=tq_real, q_row0=0, q_rows_per_batch=n1, grid_batches=b1, valid_batches=b1,
                  kv_batch0=0, kv_row0=0, n=n1)
        ob = glob(ob, tq=tq_real, q_row0=b1 * n1, q_rows_per_batch=n2, grid_batches=b2, valid_batches=b2,
                  kv_batch0=b1, kv_row0=b1 * n1, n=n2)

        h = _mix(h, oa, ob, ga, gb, wa[l], wb[l], wo[l], row(g_mix_post[l]))
        h = _ffn(h, row(g_ffn_pre[l]), wup[l], wdown[l], row(g_ffn_post[l]))

    y_prompt = h[:b1 * n1].reshape(b1, n1, D_MODEL)
    y_sample = h[b1 * n1:n_real].reshape(b2, n2, D_MODEL)
    return (y_prompt, y_sample)
```

```python
import functools
import math

import jax
import jax.numpy as jnp
import numpy as np
from jax import lax
from jax.experimental import pallas as pl
from jax.experimental.pallas import tpu as pltpu

D_MODEL = 1024
HEAD_DIM = 64
N_HEADS = 8
N_KV = 2
GROUP = N_HEADS // N_KV
Q_COLS = N_HEADS * HEAD_DIM
KV_COLS = N_KV * HEAD_DIM
N_META = 16
BLOCK = 128
GRID_W = 64
ROPE_BASE = 10000.0
ROPE_FREQS = HEAD_DIM // 4
D_FF = 2816
EPS = 1e-6
NEG_INF = -1e30
SCALE = HEAD_DIM ** -0.5
LOG2E = math.log2(math.e)

V_LANES = 128
TOKEN_TILE = 512
VMEM_LIMIT = 56 * 1024 * 1024

_C_QA = 0
_C_QB = _C_QA + Q_COLS
_C_VA = _C_QB + Q_COLS
_C_VB = _C_VA + N_KV * V_LANES
_C_GA = _C_VB + N_KV * V_LANES
_C_GB = _C_GA + D_MODEL
_C_K = _C_GB + D_MODEL
_C_END = _C_K + 2 * KV_COLS

_BF = jnp.bfloat16
_F32 = jnp.float32


def _dot(a, b):
    return jnp.dot(a, b, preferred_element_type=_F32)


def _params(sem, vmem=VMEM_LIMIT):
    return pltpu.CompilerParams(dimension_semantics=sem, vmem_limit_bytes=vmem)


def _const_spec(shape):
    nd = len(shape)
    return pl.BlockSpec(shape, lambda *_: (0,) * nd)


def _in_proj_body(h_ref, g_ref, w_ref, bd_ref, cos_ref, sin_ref, cost_ref, sint_ref, qg_ref, kg_ref,
                  qa_ref, qb_ref, kat_ref, kbt_ref, va_ref, vb_ref, ga_ref, gb_ref):
    h = h_ref[...]
    ms = jnp.mean(h * h, axis=-1, keepdims=True)
    xn = (h * lax.rsqrt(ms + EPS) * g_ref[...]).astype(_BF)

    qa = _dot(xn, w_ref[:, _C_QA:_C_QA + Q_COLS])
    qa_ref[...] = (qa * (SCALE * LOG2E)).astype(_BF)

    qb = _dot(xn, w_ref[:, _C_QB:_C_QB + Q_COLS])
    ssq = _dot((qb * qb).astype(_BF), bd_ref[...])
    qn = qb * lax.rsqrt(ssq * (1.0 / HEAD_DIM) + EPS) * qg_ref[...]
    lane = lax.broadcasted_iota(jnp.int32, qn.shape, 1)
    first = (lane & ROPE_FREQS) == 0
    partner = jnp.where(first, pltpu.roll(qn, Q_COLS - ROPE_FREQS, 1), pltpu.roll(qn, ROPE_FREQS, 1))
    cos = jnp.concatenate([cos_ref[...]] * (Q_COLS // V_LANES), axis=1)
    sin = jnp.concatenate([sin_ref[...]] * (Q_COLS // V_LANES), axis=1)
    qb_ref[...] = ((qn * cos + partner * sin) * (SCALE * LOG2E)).astype(_BF)

    ones_col = (lax.broadcasted_iota(jnp.int32, (1, N_KV * V_LANES), 1) % V_LANES == HEAD_DIM).astype(_F32)
    va = (_dot(xn, w_ref[:, _C_VA:_C_VA + N_KV * V_LANES]) + ones_col).astype(_BF)
    vb = (_dot(xn, w_ref[:, _C_VB:_C_VB + N_KV * V_LANES]) + ones_col).astype(_BF)
    for j in range(N_KV):
        va_ref[j] = va[:, j * V_LANES:(j + 1) * V_LANES]
        vb_ref[j] = vb[:, j * V_LANES:(j + 1) * V_LANES]

    ga_ref[...] = jax.nn.sigmoid(_dot(xn, w_ref[:, _C_GA:_C_GA + D_MODEL])).astype(_BF)
    gb_ref[...] = jax.nn.sigmoid(_dot(xn, w_ref[:, _C_GB:_C_GB + D_MODEL])).astype(_BF)

    kt = _dot(xn, w_ref[:, _C_K:_C_END]).T
    cost = cost_ref[...]
    sint = sint_ref[...]
    f = ROPE_FREQS
    for j in range(N_KV):
        kat_ref[j, 0] = kt[j * HEAD_DIM:(j + 1) * HEAD_DIM].astype(_BF)
        x = kt[KV_COLS + j * HEAD_DIM:KV_COLS + (j + 1) * HEAD_DIM]
        kms = jnp.mean(x * x, axis=0, keepdims=True)
        x = x * lax.rsqrt(kms + EPS) * kg_ref[...]
        partner_t = jnp.concatenate([x[f:2 * f], x[0:f], x[3 * f:4 * f], x[2 * f:3 * f]], axis=0)
        kbt_ref[j, 0] = (x * cost + partner_t * sint).astype(_BF)


def _in_proj(h, g, w, bd, cos, sin, cost, sint, qg, kg):
    ntok = h.shape[0]
    nch = ntok // TOKEN_TILE
    t = TOKEN_TILE
    row = lambda i: (i, 0)
    out_shape = (
        jax.ShapeDtypeStruct((ntok, Q_COLS), _BF),
        jax.ShapeDtypeStruct((ntok, Q_COLS), _BF),
        jax.ShapeDtypeStruct((N_KV, nch, HEAD_DIM, t), _BF),
        jax.ShapeDtypeStruct((N_KV, nch, HEAD_DIM, t), _BF),
        jax.ShapeDtypeStruct((N_KV, ntok, V_LANES), _BF),
        jax.ShapeDtypeStruct((N_KV, ntok, V_LANES), _BF),
        jax.ShapeDtypeStruct((ntok, D_MODEL), _BF),
        jax.ShapeDtypeStruct((ntok, D_MODEL), _BF),
    )
    kt_spec = pl.BlockSpec((N_KV, 1, HEAD_DIM, t), lambda i: (0, i, 0, 0))
    v_spec = pl.BlockSpec((N_KV, t, V_LANES), lambda i: (0, i, 0))
    return pl.pallas_call(
        _in_proj_body,
        grid=(nch,),
        in_specs=[
            pl.BlockSpec((t, D_MODEL), row),
            _const_spec((1, D_MODEL)),
            _const_spec((D_MODEL, _C_END)),
            _const_spec((Q_COLS, Q_COLS)),
            pl.BlockSpec((t, V_LANES), row),
            pl.BlockSpec((t, V_LANES), row),
            pl.BlockSpec((HEAD_DIM, t), lambda i: (0, i)),
            pl.BlockSpec((HEAD_DIM, t), lambda i: (0, i)),
            _const_spec((1, Q_COLS)),
            _const_spec((HEAD_DIM, 1)),
        ],
        out_specs=(
            pl.BlockSpec((t, Q_COLS), row), pl.BlockSpec((t, Q_COLS), row),
            kt_spec, kt_spec, v_spec, v_spec,
            pl.BlockSpec((t, D_MODEL), row), pl.BlockSpec((t, D_MODEL), row),
        ),
        out_shape=out_shape,
        compiler_params=_params(("parallel",)),
        name="in_proj",
    )(h, g, w, bd, cos, sin, cost, sint, qg, kg)


def _stack_heads(q, j):
    base = j * GROUP * HEAD_DIM
    return jnp.concatenate([q[:, base + g * HEAD_DIM: base + (g + 1) * HEAD_DIM] for g in range(GROUP)], axis=0)


def _sink_rows(sink_ref, j, rows, lanes):
    return jnp.concatenate([jnp.full((rows, lanes), sink_ref[j * GROUP + g], _F32) for g in range(GROUP)], axis=0)


def _window_body(geom, sink_ref, q_ref, kmain_ref, kprev_ref, knext_ref, vmain_ref, vprev_ref, vnext_ref,
                 kmeta_ref, vmeta_ref, bias_ref, _tail_ref, o_ref):
    t1, tpb1, tpb2 = geom
    t = pl.program_id(0)
    in_prompt = t < t1
    is_first = jnp.where(in_prompt, t % tpb1 == 0, (t - t1) % tpb2 == 0)
    is_last = jnp.where(in_prompt, t % tpb1 == tpb1 - 1, (t - t1) % tpb2 == tpb2 - 1)
    nblk = TOKEN_TILE // BLOCK
    lane_tiles = 4 * BLOCK // V_LANES
    q_all = q_ref[...]
    for j in range(N_KV):
        kcat = jnp.concatenate([kprev_ref[j, 0], kmain_ref[j, 0], knext_ref[j, 0]], axis=1)
        vcat = jnp.concatenate([vprev_ref[j], vmain_ref[j], vnext_ref[j]], axis=0)
        kmeta = kmeta_ref[0, j]
        vmeta = vmeta_ref[0, j]
        sink = _sink_rows(sink_ref, j, BLOCK, V_LANES)
        for r in range(nblk):
            variant = 0
            if r == 0:
                variant = jnp.where(is_first, 1, 0)
            if r == nblk - 1:
                variant = jnp.where(is_last, 2, variant)
            qs = _stack_heads(q_all[r * BLOCK:(r + 1) * BLOCK], j)
            kwin = jnp.concatenate([kcat[:, r * BLOCK:(r + 3) * BLOCK], kmeta], axis=1)
            vwin = jnp.concatenate([vcat[r * BLOCK:(r + 3) * BLOCK], vmeta], axis=0)
            s = _dot(qs, kwin) + bias_ref[variant, j]
            m = jnp.maximum(jnp.broadcast_to(s.max(axis=1, keepdims=True), sink.shape), sink)
            p = jnp.exp2(s - jnp.concatenate([m] * lane_tiles, axis=1)).astype(_BF)
            acc = _dot(p, vwin)
            l = jnp.broadcast_to(acc[:, HEAD_DIM:HEAD_DIM + 1], sink.shape) + jnp.exp2(sink - m)
            o = (acc / l).astype(_BF)
            for g in range(GROUP):
                c0 = (j * GROUP + g) * HEAD_DIM
                o_ref[r * BLOCK:(r + 1) * BLOCK, c0:c0 + HEAD_DIM] = o[g * BLOCK:(g + 1) * BLOCK, :HEAD_DIM]


def _window_attention(sink, qa, kat, va, kmeta, vmeta, bias, tail, layout):
    n_real, b1, n1, b2, n2 = layout
    t = TOKEN_TILE
    nch = kat.shape[1]
    t1 = b1 * n1 // t
    tpb1, tpb2 = n1 // t, n2 // t
    sub = t // BLOCK

    def bid(i):
        return jnp.where(i < t1, i // tpb1, b1 + (i - t1) // tpb2)

    return pl.pallas_call(
        functools.partial(_window_body, (t1, tpb1, tpb2)),
        grid=(n_real // t,),
        in_specs=[
            pl.BlockSpec(memory_space=pltpu.SMEM),
            pl.BlockSpec((t, Q_COLS), lambda i: (i, 0)),
            pl.BlockSpec((N_KV, 1, HEAD_DIM, t), lambda i: (0, i, 0, 0)),
            pl.BlockSpec((N_KV, 1, HEAD_DIM, BLOCK), lambda i: (0, jnp.maximum(i - 1, 0), 0, sub - 1)),
            pl.BlockSpec((N_KV, 1, HEAD_DIM, BLOCK), lambda i: (0, jnp.minimum(i + 1, nch - 1), 0, 0)),
            pl.BlockSpec((N_KV, t, V_LANES), lambda i: (0, i, 0)),
            pl.BlockSpec((N_KV, BLOCK, V_LANES), lambda i: (0, jnp.maximum(i * sub - 1, 0), 0)),
            pl.BlockSpec((N_KV, BLOCK, V_LANES), lambda i: (0, jnp.minimum(i + 1, nch - 1) * sub, 0)),
            pl.BlockSpec((1, N_KV, HEAD_DIM, BLOCK), lambda i: (bid(i), 0, 0, 0)),
            pl.BlockSpec((1, N_KV, BLOCK, V_LANES), lambda i: (bid(i), 0, 0, 0)),
            _const_spec((3, N_KV, GROUP * BLOCK, 4 * BLOCK)),
            pl.BlockSpec(memory_space=pl.ANY),
        ],
        out_specs=pl.BlockSpec((t, Q_COLS), lambda i: (i, 0)),
        out_shape=jax.ShapeDtypeStruct(qa.shape, _BF),
        input_output_aliases={11: 0},
        compiler_params=_params(("parallel",)),
        name="window_attn",
    )(sink, qa, kat, kat, kat, va, va, va, kmeta, vmeta, bias, tail)


def _window_meta_body(nb, sink_ref, q_ref, kfirst_ref, vfirst_ref, kmeta_ref, vmeta_ref, o_ref):
    b = pl.program_id(0)

    @pl.when(b < nb)
    def _():
        q_all = q_ref[...]
        for j in range(N_KV):
            qs = _stack_heads(q_all, j)
            sink = _sink_rows(sink_ref, j, N_META, 1)
            sm = _dot(qs, kmeta_ref[0, j])
            sf = _dot(qs, kfirst_ref[j, 0])
            m = jnp.maximum(jnp.maximum(sm.max(axis=1, keepdims=True), sf.max(axis=1, keepdims=True)), sink)
            pm = jnp.exp2(sm - m).astype(_BF)
            pf = jnp.exp2(sf - m).astype(_BF)
            acc = _dot(pm, vmeta_ref[0, j]) + _dot(pf, vfirst_ref[j])
            l = acc[:, HEAD_DIM:HEAD_DIM + 1] + jnp.exp2(sink - m)
            o = (acc[:, :HEAD_DIM] / l).astype(_BF)
            for g in range(GROUP):
                c0 = (j * GROUP + g) * HEAD_DIM
                o_ref[:, c0:c0 + HEAD_DIM] = o[g * N_META:(g + 1) * N_META]

    @pl.when(b >= nb)
    def _():
        o_ref[...] = jnp.zeros(o_ref.shape, o_ref.dtype)


def _window_meta_attention(sink, qa, kat, va, kmeta, vmeta, layout):
    n_real, b1, n1, b2, n2 = layout
    ntok = qa.shape[0]
    nb = b1 + b2
    t = TOKEN_TILE
    sub = t // BLOCK

    def start_chunk(b):
        bc = jnp.minimum(b, nb - 1)
        return jnp.where(bc < b1, bc * (n1 // t), b1 * (n1 // t) + (bc - b1) * (n2 // t))

    return pl.pallas_call(
        functools.partial(_window_meta_body, nb),
        grid=((ntok - n_real) // N_META,),
        in_specs=[
            pl.BlockSpec(memory_space=pltpu.SMEM),
            pl.BlockSpec((N_META, Q_COLS), lambda b: (n_real // N_META + b, 0)),
            pl.BlockSpec((N_KV, 1, HEAD_DIM, BLOCK), lambda b: (0, start_chunk(b), 0, 0)),
            pl.BlockSpec((N_KV, BLOCK, V_LANES), lambda b: (0, start_chunk(b) * sub, 0)),
            pl.BlockSpec((1, N_KV, HEAD_DIM, N_META), lambda b: (jnp.minimum(b, nb - 1), 0, 0, 0)),
            pl.BlockSpec((1, N_KV, N_META, V_LANES), lambda b: (jnp.minimum(b, nb - 1), 0, 0, 0)),
        ],
        out_specs=pl.BlockSpec((N_META, Q_COLS), lambda b: (n_real // N_META + b, 0)),
        out_shape=jax.ShapeDtypeStruct(qa.shape, _BF),
        compiler_params=_params(("arbitrary",)),
        name="window_meta_attn",
    )(sink, qa, kat, va, kmeta, vmeta)


def _global_body(nvalid, nchunks, tq, q_ref, kt_ref, v_ref, kmeta_ref, vmeta_ref, _prev_ref, o_ref,
                 qs_ref, s0_ref, s1_ref, m_ref, acc_ref):
    b = pl.program_id(0)
    lane_tiles = TOKEN_TILE // V_LANES

    def scores(c, s_ref):
        s_ref[...] = _dot(qs_ref[...], kt_ref[0, c])

    def softmax_pv(c, s_ref):
        s = s_ref[...]
        m_prev = m_ref[...]
        m_new = jnp.maximum(m_prev, jnp.broadcast_to(s.max(axis=1, keepdims=True), m_prev.shape))
        p = jnp.exp2(s - jnp.concatenate([m_new] * lane_tiles, axis=1)).astype(_BF)
        r0 = pl.multiple_of(c * TOKEN_TILE, TOKEN_TILE)
        acc_ref[...] = jnp.exp2(m_prev - m_new) * acc_ref[...] + _dot(p, v_ref[0, pl.ds(r0, TOKEN_TILE), :])
        m_ref[...] = m_new

    @pl.when(b < nvalid)
    def _():
        q_all = q_ref[...]
        for g in range(GROUP):
            qs_ref[g * tq:(g + 1) * tq, :] = q_all[:, g * HEAD_DIM:(g + 1) * HEAD_DIM]
        qs = qs_ref[...]
        sm = _dot(qs, kmeta_ref[0, 0])
        m0 = sm.max(axis=1, keepdims=True)
        m_ref[...] = jnp.broadcast_to(m0, m_ref.shape)
        acc_ref[...] = _dot(jnp.exp2(sm - m0).astype(_BF), vmeta_ref[0, 0])
        scores(0, s0_ref)

        def pair(i, carry):
            c = 2 * i
            scores(c + 1, s1_ref)
            softmax_pv(c, s0_ref)
            scores(c + 2, s0_ref)
            softmax_pv(c + 1, s1_ref)
            return carry

        lax.fori_loop(0, (nchunks - 1) // 2, pair, 0)
        if nchunks % 2 == 0:
            scores(nchunks - 1, s1_ref)
            softmax_pv(nchunks - 2, s0_ref)
            softmax_pv(nchunks - 1, s1_ref)
        else:
            softmax_pv(nchunks - 1, s0_ref)
        acc = acc_ref[...]
        o = (acc[:, :HEAD_DIM] / acc[:, HEAD_DIM:HEAD_DIM + 1]).astype(_BF)
        for g in range(GROUP):
            o_ref[:, g * HEAD_DIM:(g + 1) * HEAD_DIM] = o[g * tq:(g + 1) * tq]

    @pl.when(b >= nvalid)
    def _():
        o_ref[...] = jnp.zeros(o_ref.shape, o_ref.dtype)


def _global_attention(qb, kbt, vb, kmeta, vmeta, prev, *, tq, q_row0, q_rows_per_batch, grid_batches,
                      valid_batches, kv_batch0, kv_row0, n):
    nchunks = n // TOKEN_TILE
    qt = q_rows_per_batch // tq
    assert q_row0 % tq == 0 and kv_row0 % n == 0 and q_rows_per_batch % tq == 0
    kvb = lambda b: jnp.minimum(b, valid_batches - 1)
    qmap = lambda b, j, i: (q_row0 // tq + b * qt + i, j)
    in_specs = [
        pl.BlockSpec((tq, GROUP * HEAD_DIM), qmap),
        pl.BlockSpec((1, nchunks, HEAD_DIM, TOKEN_TILE), lambda b, j, i: (j, kv_row0 // n + kvb(b), 0, 0)),
        pl.BlockSpec((1, n, V_LANES), lambda b, j, i: (j, kv_row0 // n + kvb(b), 0)),
        pl.BlockSpec((1, 1, HEAD_DIM, N_META), lambda b, j, i: (kv_batch0 + kvb(b), j, 0, 0)),
        pl.BlockSpec((1, 1, N_META, V_LANES), lambda b, j, i: (kv_batch0 + kvb(b), j, 0, 0)),
    ]
    args = [qb, kbt, vb, kmeta, vmeta]
    aliases = {}
    if prev is not None:
        in_specs.append(pl.BlockSpec(memory_space=pl.ANY))
        args.append(prev)
        aliases = {5: 0}
    body = functools.partial(_global_body, valid_batches, nchunks, tq)
    if prev is None:
        body = functools.partial(_global_body_noprev, body)
    return pl.pallas_call(
        body,
        grid=(grid_batches, N_KV, qt),
        in_specs=in_specs,
        out_specs=pl.BlockSpec((tq, GROUP * HEAD_DIM), qmap),
        out_shape=jax.ShapeDtypeStruct(qb.shape, _BF),
        scratch_shapes=[
            pltpu.VMEM((GROUP * tq, HEAD_DIM), _BF),
            pltpu.VMEM((GROUP * tq, TOKEN_TILE), _F32),
            pltpu.VMEM((GROUP * tq, TOKEN_TILE), _F32),
            pltpu.VMEM((GROUP * tq, V_LANES), _F32),
            pltpu.VMEM((GROUP * tq, V_LANES), _F32),
        ],
        input_output_aliases=aliases,
        compiler_params=_params(("parallel", "parallel", "arbitrary")),
        name=f"global_attn_tq{tq}_n{n}",
    )(*args)


def _global_body_noprev(body, q_ref, kt_ref, v_ref, kmeta_ref, vmeta_ref, o_ref, *scratch):
    body(q_ref, kt_ref, v_ref, kmeta_ref, vmeta_ref, None, o_ref, *scratch)


def _mix_body(h_ref, oa_ref, ob_ref, ga_ref, gb_ref, wa_ref, wb_ref, wo_ref, g_ref, o_ref):
    mix = (ga_ref[...].astype(_F32) * _dot(oa_ref[...], wa_ref[...])
           + gb_ref[...].astype(_F32) * _dot(ob_ref[...], wb_ref[...]))
    u = _dot(mix.astype(_BF), wo_ref[...])
    ms = jnp.mean(u * u, axis=-1, keepdims=True)
    o_ref[...] = h_ref[...] + u * lax.rsqrt(ms + EPS) * g_ref[...]


def _mix(h, oa, ob, ga, gb, wa, wb, wo, g):
    ntok = h.shape[0]
    t = TOKEN_TILE
    row = lambda i: (i, 0)
    return pl.pallas_call(
        _mix_body,
        grid=(ntok // t,),
        in_specs=[
            pl.BlockSpec((t, D_MODEL), row),
            pl.BlockSpec((t, Q_COLS), row), pl.BlockSpec((t, Q_COLS), row),
            pl.BlockSpec((t, D_MODEL), row), pl.BlockSpec((t, D_MODEL), row),
            _const_spec((Q_COLS, D_MODEL)), _const_spec((Q_COLS, D_MODEL)),
            _const_spec((D_MODEL, D_MODEL)), _const_spec((1, D_MODEL)),
        ],
        out_specs=pl.BlockSpec((t, D_MODEL), row),
        out_shape=jax.ShapeDtypeStruct(h.shape, _F32),
        input_output_aliases={0: 0},
        compiler_params=_params(("parallel",)),
        name="branch_mix",
    )(h, oa, ob, ga, gb, wa, wb, wo, g)


_FF_CHUNK = 256


def _ffn_body(h_ref, gpre_ref, wup_ref, wdown_ref, gpost_ref, o_ref, act_ref):
    h = h_ref[...]
    ms = jnp.mean(h * h, axis=-1, keepdims=True)
    xn = (h * lax.rsqrt(ms + EPS) * gpre_ref[...]).astype(_BF)
    for c in range(0, D_FF, _FF_CHUNK):
        a = _dot(xn, wup_ref[:, c:c + _FF_CHUNK])
        b = _dot(xn, wup_ref[:, D_FF + c:D_FF + c + _FF_CHUNK])
        act_ref[:, c:c + _FF_CHUNK] = (a * jax.nn.sigmoid(a) * b).astype(_BF)
    u = _dot(act_ref[...], wdown_ref[...])
    ms = jnp.mean(u * u, axis=-1, keepdims=True)
    o_ref[...] = h + u * lax.rsqrt(ms + EPS) * gpost_ref[...]


def _ffn(h, gpre, wup, wdown, gpost):
    ntok = h.shape[0]
    t = TOKEN_TILE
    row = lambda i: (i, 0)
    return pl.pallas_call(
        _ffn_body,
        grid=(ntok // t,),
        in_specs=[
            pl.BlockSpec((t, D_MODEL), row),
            _const_spec((1, D_MODEL)),
            _const_spec((D_MODEL, 2 * D_FF)),
            _const_spec((D_FF, D_MODEL)),
            _const_spec((1, D_MODEL)),
        ],
        out_specs=pl.BlockSpec((t, D_MODEL), row),
        out_shape=jax.ShapeDtypeStruct(h.shape, _F32),
        scratch_shapes=[pltpu.VMEM((t, D_FF), _BF)],
        input_output_aliases={0: 0},
        compiler_params=_params(("parallel",)),
        name="swiglu_ffn",
    )(h, gpre, wup, wdown, gpost)


def _rope_tables(layout, ntok):
    n_real, b1, n1, b2, n2 = layout
    idx = np.zeros((ntok,), np.int64)
    idx[:b1 * n1] = np.arange(b1 * n1) % n1
    idx[b1 * n1:n_real] = np.arange(b2 * n2) % n2
    rows = (idx // GRID_W).astype(np.float32)
    cols = (idx % GRID_W).astype(np.float32)
    rows[n_real:] = 0.0
    cols[n_real:] = 0.0
    freqs = ROPE_BASE ** (-jnp.arange(ROPE_FREQS, dtype=_F32) / ROPE_FREQS)
    ang_r = jnp.asarray(rows)[:, None] * freqs[None, :]
    ang_c = jnp.asarray(cols)[:, None] * freqs[None, :]
    cr, sr, cc, sc = jnp.cos(ang_r), jnp.sin(ang_r), jnp.cos(ang_c), jnp.sin(ang_c)
    cos64 = jnp.concatenate([cr, cr, cc, cc], axis=1)
    sin64 = jnp.concatenate([-sr, sr, -sc, sc], axis=1)
    return (jnp.concatenate([cos64, cos64], axis=1), jnp.concatenate([sin64, sin64], axis=1),
            cos64.T, sin64.T)


def _window_bias():
    slopes = 2.0 ** (-8.0 * np.arange(1, N_HEADS + 1, dtype=np.float64) / N_HEADS)
    rel = np.arange(BLOCK)[:, None] - (np.arange(3 * BLOCK) - BLOCK)[None, :]
    dist = np.abs(rel)
    band = dist <= BLOCK
    local = np.where(band[None], -slopes[:, None, None] * dist[None].astype(np.float64) * LOG2E, NEG_INF)
    extra = np.full((N_HEADS, BLOCK, BLOCK), NEG_INF)
    extra[:, :, :N_META] = 0.0
    base = np.concatenate([local, extra], axis=2)
    no_prev = base.copy()
    no_prev[:, :, :BLOCK] = NEG_INF
    no_next = base.copy()
    no_next[:, :, 2 * BLOCK:3 * BLOCK] = NEG_INF
    out = np.stack([base, no_prev, no_next]).reshape(3, N_KV, GROUP * BLOCK, 4 * BLOCK)
    return jnp.asarray(out, _F32)


def _rearranged_w_in(w_in):
    o = np.cumsum([0, Q_COLS, KV_COLS, KV_COLS, Q_COLS, KV_COLS, KV_COLS, D_MODEL, D_MODEL])
    qa, ka, va, qb, kb, vb, ga, gb = [w_in[..., o[i]:o[i + 1]] for i in range(8)]

    def pad_v(v):
        z = jnp.zeros(v.shape[:-1] + (V_LANES - HEAD_DIM,), v.dtype)
        return jnp.concatenate([v[..., :HEAD_DIM], z, v[..., HEAD_DIM:], z], axis=-1)

    return jnp.concatenate([qa, qb, pad_v(va), pad_v(vb), ga, gb, ka, kb], axis=-1).astype(_BF)


def _meta_kv(kt, v, n_real, nb):
    tail = kt[:, n_real // TOKEN_TILE:]
    tail = jnp.moveaxis(tail, 2, 1).reshape(N_KV, HEAD_DIM, -1)[:, :, :nb * N_META]
    kmeta = jnp.transpose(tail.reshape(N_KV, HEAD_DIM, nb, N_META), (2, 0, 1, 3))
    vmeta = jnp.transpose(v[:, n_real:n_real + nb * N_META].reshape(N_KV, nb, N_META, V_LANES), (1, 0, 2, 3))
    return kmeta, vmeta


def kernel(x_prompt, x_sample, meta_tokens, g_mix_pre, g_mix_post, g_ffn_pre, g_ffn_post, w_in, q_norm_b,
           k_norm_b, sink_a, w_branch_a, w_branch_b, w_out, w_ffn_up, w_ffn_down):
    b1, n1, _ = x_prompt.shape
    b2, n2, _ = x_sample.shape
    depth = w_in.shape[0]
    t = TOKEN_TILE
    assert n1 % t == 0 and n2 % t == 0 and (b1 * n1) % n2 == 0 and n1 % GRID_W == 0 and n2 % GRID_W == 0
    nb = b1 + b2
    n_real = b1 * n1 + b2 * n2
    tail = -(-(nb * N_META) // t) * t
    ntok = n_real + tail
    layout = (n_real, b1, n1, b2, n2)

    h = jnp.concatenate([
        x_prompt.reshape(b1 * n1, D_MODEL), x_sample.reshape(b2 * n2, D_MODEL),
        jnp.tile(meta_tokens.astype(x_prompt.dtype), (nb, 1)),
        jnp.zeros((tail - nb * N_META, D_MODEL), x_prompt.dtype)], axis=0)

    cos, sin, cost, sint = _rope_tables(layout, ntok)
    bias = _window_bias()
    w_main = _rearranged_w_in(w_in)
    bd = jnp.asarray(np.kron(np.eye(N_HEADS), np.ones((HEAD_DIM, HEAD_DIM))), _BF)
    wa, wb, wo = w_branch_a.astype(_BF), w_branch_b.astype(_BF), w_out.astype(_BF)
    wup, wdown = w_ffn_up.astype(_BF), w_ffn_down.astype(_BF)
    row = lambda g: g.reshape(1, -1).astype(_F32)

    tq_real = 256
    for l in range(depth):
        qa, qb, kat, kbt, va, vb, ga, gb = _in_proj(
            h, row(g_mix_pre[l]), w_main[l], bd, cos, sin, cost, sint,
            jnp.tile(q_norm_b[l].astype(_F32), N_HEADS).reshape(1, Q_COLS),
            k_norm_b[l].astype(_F32).reshape(HEAD_DIM, 1))
        kameta, vameta = _meta_kv(kat, va, n_real, nb)
        kbmeta, vbmeta = _meta_kv(kbt, vb, n_real, nb)
        sink = sink_a[l].astype(_F32) * LOG2E
        kameta_pad = jnp.pad(kameta, ((0, 0), (0, 0), (0, 0), (0, BLOCK - N_META)))
        vameta_pad = jnp.pad(vameta, ((0, 0), (0, 0), (0, BLOCK - N_META), (0, 0)))

        oa = _window_meta_attention(sink, qa, kat, va, kameta, vameta, layout)
        oa = _window_attention(sink, qa, kat, va, kameta_pad, vameta_pad, bias, oa, layout)

        glob = functools.partial(_global_attention, qb, kbt, vb, kbmeta, vbmeta)
        ob = glob(None, tq=N_META, q_row0=n_real, q_rows_per_batch=N_META, grid_batches=b1,
                  valid_batches=b1, kv_batch0=0, kv_row0=0, n=n1)
        ob = glob(ob, tq=N_META, q_row0=n_real + b1 * N_META, q_rows_per_batch=N_META,
                  grid_batches=tail // N_META - b1, valid_batches=b2, kv_batch0=b1, kv_row0=b1 * n1, n=n2)
        ob = glob(ob, tq=tq_real, q_row0=0, q_rows_per_batch=n1, grid_batches=b1, valid_batches=b1,
                  kv_batch0=0, kv_row0=0, n=n1)
        ob = glob(ob, tq=tq_real, q_row0=b1 * n1, q_rows_per_batch=n2, grid_batches=b2, valid_batches=b2,
                  kv_batch0=b1, kv_row0=b1 * n1, n=n2)

        h = _mix(h, oa, ob, ga, gb, wa[l], wb[l], wo[l], row(g_mix_post[l]))
        h = _ffn(h, row(g_ffn_pre[l]), wup[l], wdown[l], row(g_ffn_post[l]))

    y_prompt = h[:b1 * n1].reshape(b1, n1, D_MODEL)
    y_sample = h[b1 * n1:n_real].reshape(b2, n2, D_MODEL)
    return (y_prompt, y_sample)
```

```python
import functools
import math

import jax
import jax.numpy as jnp
import numpy as np
from jax import lax
from jax.experimental import pallas as pl
from jax.experimental.pallas import tpu as pltpu

D_MODEL = 1024
HEAD_DIM = 64
N_HEADS = 8
N_KV = 2
GROUP = N_HEADS // N_KV
Q_COLS = N_HEADS * HEAD_DIM
KV_COLS = N_KV * HEAD_DIM
N_META = 16
BLOCK = 128
GRID_W = 64
ROPE_BASE = 10000.0
ROPE_FREQS = HEAD_DIM // 4
D_FF = 2816
EPS = 1e-6
NEG_INF = -1e30
SCALE = HEAD_DIM ** -0.5
LOG2E = math.log2(math.e)

V_LANES = 128
TOKEN_TILE = 512
VMEM_LIMIT = 56 * 1024 * 1024

_C_QA = 0
_C_VA = _C_QA + Q_COLS
_C_GA = _C_VA + N_KV * V_LANES
_C_GB = _C_GA + D_MODEL
_C_END = _C_GB + D_MODEL
_R_KA = 0
_R_KB = _R_KA + KV_COLS
_R_VB = _R_KB + KV_COLS
_R_QB = _R_VB + N_KV * V_LANES
_R_END = _R_QB + Q_COLS

_BF = jnp.bfloat16
_F32 = jnp.float32


def _dot(a, b):
    return jnp.dot(a, b, preferred_element_type=_F32)


def _dot_nt(a, b):
    return lax.dot_general(a, b, (((1,), (1,)), ((), ())), preferred_element_type=_F32)


def _dot_tn(a, b):
    return lax.dot_general(a, b, (((0,), (0,)), ((), ())), preferred_element_type=_F32)


def _params(sem, vmem=VMEM_LIMIT):
    return pltpu.CompilerParams(dimension_semantics=sem, vmem_limit_bytes=vmem)


def _const_spec(shape):
    nd = len(shape)
    return pl.BlockSpec(shape, lambda *_: (0,) * nd)


def _in_proj_body(h_ref, g_ref, w_ref, wt_ref, cost_ref, sint_ref, qg_ref, kg_ref,
                  qa_ref, qbt_ref, kat_ref, kbt_ref, va_ref, vbt_ref, ga_ref, gb_ref):
    h = h_ref[...]
    ms = jnp.mean(h * h, axis=-1, keepdims=True)
    xn = (h * lax.rsqrt(ms + EPS) * g_ref[...]).astype(_BF)

    qa = _dot(xn, w_ref[:, _C_QA:_C_QA + Q_COLS])
    qa_ref[...] = (qa * (SCALE * LOG2E)).astype(_BF)

    ones_col = (lax.broadcasted_iota(jnp.int32, (1, N_KV * V_LANES), 1) % V_LANES == HEAD_DIM).astype(_F32)
    va = (_dot(xn, w_ref[:, _C_VA:_C_VA + N_KV * V_LANES]) + ones_col).astype(_BF)
    for j in range(N_KV):
        va_ref[j] = va[:, j * V_LANES:(j + 1) * V_LANES]

    ga_ref[...] = jax.nn.sigmoid(_dot(xn, w_ref[:, _C_GA:_C_GA + D_MODEL])).astype(_BF)
    gb_ref[...] = jax.nn.sigmoid(_dot(xn, w_ref[:, _C_GB:_C_GB + D_MODEL])).astype(_BF)

    tt = _dot_nt(wt_ref[...], xn)
    cost = cost_ref[...]
    sint = sint_ref[...]
    f = ROPE_FREQS

    def norm_rope(x, gain):
        x = x * lax.rsqrt(jnp.mean(x * x, axis=0, keepdims=True) + EPS) * gain
        partner = jnp.concatenate([x[f:2 * f], x[0:f], x[3 * f:4 * f], x[2 * f:3 * f]], axis=0)
        return x * cost + partner * sint

    ones_row = (lax.broadcasted_iota(jnp.int32, (V_LANES, 1), 0) == HEAD_DIM).astype(_F32)
    for j in range(N_KV):
        kat_ref[j, 0] = tt[_R_KA + j * HEAD_DIM:_R_KA + (j + 1) * HEAD_DIM].astype(_BF)
        kbt_ref[j, 0] = norm_rope(tt[_R_KB + j * HEAD_DIM:_R_KB + (j + 1) * HEAD_DIM], kg_ref[...]).astype(_BF)
        vbt_ref[j, 0] = (tt[_R_VB + j * V_LANES:_R_VB + (j + 1) * V_LANES] + ones_row).astype(_BF)
    for hd in range(N_HEADS):
        q = norm_rope(tt[_R_QB + hd * HEAD_DIM:_R_QB + (hd + 1) * HEAD_DIM], qg_ref[...])
        qbt_ref[hd, 0] = (q * (SCALE * LOG2E)).astype(_BF)


def _in_proj(h, g, w, wt, cost, sint, qg, kg):
    ntok = h.shape[0]
    nch = ntok // TOKEN_TILE
    t = TOKEN_TILE
    row = lambda i: (i, 0)
    chunk = lambda i: (0, i, 0, 0)
    out_shape = (
        jax.ShapeDtypeStruct((ntok, Q_COLS), _BF),
        jax.ShapeDtypeStruct((N_HEADS, nch, HEAD_DIM, t), _BF),
        jax.ShapeDtypeStruct((N_KV, nch, HEAD_DIM, t), _BF),
        jax.ShapeDtypeStruct((N_KV, nch, HEAD_DIM, t), _BF),
        jax.ShapeDtypeStruct((N_KV, ntok, V_LANES), _BF),
        jax.ShapeDtypeStruct((N_KV, nch, V_LANES, t), _BF),
        jax.ShapeDtypeStruct((ntok, D_MODEL), _BF),
        jax.ShapeDtypeStruct((ntok, D_MODEL), _BF),
    )
    kt_spec = pl.BlockSpec((N_KV, 1, HEAD_DIM, t), chunk)
    return pl.pallas_call(
        _in_proj_body,
        grid=(nch,),
        in_specs=[
            pl.BlockSpec((t, D_MODEL), row),
            _const_spec((1, D_MODEL)),
            _const_spec((D_MODEL, _C_END)),
            _const_spec((_R_END, D_MODEL)),
            pl.BlockSpec((HEAD_DIM, t), lambda i: (0, i)),
            pl.BlockSpec((HEAD_DIM, t), lambda i: (0, i)),
            _const_spec((HEAD_DIM, 1)),
            _const_spec((HEAD_DIM, 1)),
        ],
        out_specs=(
            pl.BlockSpec((t, Q_COLS), row),
            pl.BlockSpec((N_HEADS, 1, HEAD_DIM, t), chunk),
            kt_spec, kt_spec,
            pl.BlockSpec((N_KV, t, V_LANES), lambda i: (0, i, 0)),
            pl.BlockSpec((N_KV, 1, V_LANES, t), chunk),
            pl.BlockSpec((t, D_MODEL), row), pl.BlockSpec((t, D_MODEL), row),
        ),
        out_shape=out_shape,
        compiler_params=_params(("parallel",)),
        name="in_proj",
    )(h, g, w, wt, cost, sint, qg, kg)


def _stack_heads(q, j):
    base = j * GROUP * HEAD_DIM
    return jnp.concatenate([q[:, base + g * HEAD_DIM: base + (g + 1) * HEAD_DIM] for g in range(GROUP)], axis=0)


def _sink_rows(sink_ref, j, rows, lanes):
    return jnp.concatenate([jnp.full((rows, lanes), sink_ref[j * GROUP + g], _F32) for g in range(GROUP)], axis=0)


def _window_body(geom, sink_ref, q_ref, kmain_ref, kprev_ref, knext_ref, vmain_ref, vprev_ref, vnext_ref,
                 kmeta_ref, vmeta_ref, bias_ref, _tail_ref, o_ref):
    t1, tpb1, tpb2 = geom
    t = pl.program_id(0)
    in_prompt = t < t1
    is_first = jnp.where(in_prompt, t % tpb1 == 0, (t - t1) % tpb2 == 0)
    is_last = jnp.where(in_prompt, t % tpb1 == tpb1 - 1, (t - t1) % tpb2 == tpb2 - 1)
    nblk = TOKEN_TILE // BLOCK
    lane_tiles = 4 * BLOCK // V_LANES
    q_all = q_ref[...]
    for j in range(N_KV):
        kcat = jnp.concatenate([kprev_ref[j, 0], kmain_ref[j, 0], knext_ref[j, 0]], axis=1)
        vcat = jnp.concatenate([vprev_ref[j], vmain_ref[j], vnext_ref[j]], axis=0)
        kmeta = kmeta_ref[0, j]
        vmeta = vmeta_ref[0, j]
        sink = _sink_rows(sink_ref, j, BLOCK, V_LANES)
        for r in range(nblk):
            variant = 0
            if r == 0:
                variant = jnp.where(is_first, 1, 0)
            if r == nblk - 1:
                variant = jnp.where(is_last, 2, variant)
            qs = _stack_heads(q_all[r * BLOCK:(r + 1) * BLOCK], j)
            kwin = jnp.concatenate([kcat[:, r * BLOCK:(r + 3) * BLOCK], kmeta], axis=1)
            vwin = jnp.concatenate([vcat[r * BLOCK:(r + 3) * BLOCK], vmeta], axis=0)
            s = _dot(qs, kwin) + bias_ref[variant, j]
            m = jnp.maximum(jnp.broadcast_to(s.max(axis=1, keepdims=True), sink.shape), sink)
            p = jnp.exp2(s - jnp.concatenate([m] * lane_tiles, axis=1)).astype(_BF)
            acc = _dot(p, vwin)
            l = jnp.broadcast_to(acc[:, HEAD_DIM:HEAD_DIM + 1], sink.shape) + jnp.exp2(sink - m)
            o = (acc / l).astype(_BF)
            for g in range(GROUP):
                c0 = (j * GROUP + g) * HEAD_DIM
                o_ref[r * BLOCK:(r + 1) * BLOCK, c0:c0 + HEAD_DIM] = o[g * BLOCK:(g + 1) * BLOCK, :HEAD_DIM]


def _window_attention(sink, qa, kat, va, kmeta, vmeta, bias, tail, layout):
    n_real, b1, n1, b2, n2 = layout
    t = TOKEN_TILE
    nch = kat.shape[1]
    t1 = b1 * n1 // t
    tpb1, tpb2 = n1 // t, n2 // t
    sub = t // BLOCK

    def bid(i):
        return jnp.where(i < t1, i // tpb1, b1 + (i - t1) // tpb2)

    return pl.pallas_call(
        functools.partial(_window_body, (t1, tpb1, tpb2)),
        grid=(n_real // t,),
        in_specs=[
            pl.BlockSpec(memory_space=pltpu.SMEM),
            pl.BlockSpec((t, Q_COLS), lambda i: (i, 0)),
            pl.BlockSpec((N_KV, 1, HEAD_DIM, t), lambda i: (0, i, 0, 0)),
            pl.BlockSpec((N_KV, 1, HEAD_DIM, BLOCK), lambda i: (0, jnp.maximum(i - 1, 0), 0, sub - 1)),
            pl.BlockSpec((N_KV, 1, HEAD_DIM, BLOCK), lambda i: (0, jnp.minimum(i + 1, nch - 1), 0, 0)),
            pl.BlockSpec((N_KV, t, V_LANES), lambda i: (0, i, 0)),
            pl.BlockSpec((N_KV, BLOCK, V_LANES), lambda i: (0, jnp.maximum(i * sub - 1, 0), 0)),
            pl.BlockSpec((N_KV, BLOCK, V_LANES), lambda i: (0, jnp.minimum(i + 1, nch - 1) * sub, 0)),
            pl.BlockSpec((1, N_KV, HEAD_DIM, BLOCK), lambda i: (bid(i), 0, 0, 0)),
            pl.BlockSpec((1, N_KV, BLOCK, V_LANES), lambda i: (bid(i), 0, 0, 0)),
            _const_spec((3, N_KV, GROUP * BLOCK, 4 * BLOCK)),
            pl.BlockSpec(memory_space=pl.ANY),
        ],
        out_specs=pl.BlockSpec((t, Q_COLS), lambda i: (i, 0)),
        out_shape=jax.ShapeDtypeStruct(qa.shape, _BF),
        input_output_aliases={11: 0},
        compiler_params=_params(("parallel",)),
        name="window_attn",
    )(sink, qa, kat, kat, kat, va, va, va, kmeta, vmeta, bias, tail)


def _window_meta_body(nb, sink_ref, q_ref, kfirst_ref, vfirst_ref, kmeta_ref, vmeta_ref, o_ref):
    b = pl.program_id(0)

    @pl.when(b < nb)
    def _():
        q_all = q_ref[...]
        for j in range(N_KV):
            qs = _stack_heads(q_all, j)
            sink = _sink_rows(sink_ref, j, N_META, 1)
            sm = _dot(qs, kmeta_ref[0, j])
            sf = _dot(qs, kfirst_ref[j, 0])
            m = jnp.maximum(jnp.maximum(sm.max(axis=1, keepdims=True), sf.max(axis=1, keepdims=True)), sink)
            pm = jnp.exp2(sm - m).astype(_BF)
            pf = jnp.exp2(sf - m).astype(_BF)
            acc = _dot(pm, vmeta_ref[0, j]) + _dot(pf, vfirst_ref[j])
            l = acc[:, HEAD_DIM:HEAD_DIM + 1] + jnp.exp2(sink - m)
            o = (acc[:, :HEAD_DIM] / l).astype(_BF)
            for g in range(GROUP):
                c0 = (j * GROUP + g) * HEAD_DIM
                o_ref[:, c0:c0 + HEAD_DIM] = o[g * N_META:(g + 1) * N_META]

    @pl.when(b >= nb)
    def _():
        o_ref[...] = jnp.zeros(o_ref.shape, o_ref.dtype)


def _window_meta_attention(sink, qa, kat, va, kmeta, vmeta, layout):
    n_real, b1, n1, b2, n2 = layout
    ntok = qa.shape[0]
    nb = b1 + b2
    t = TOKEN_TILE
    sub = t // BLOCK

    def start_chunk(b):
        bc = jnp.minimum(b, nb - 1)
        return jnp.where(bc < b1, bc * (n1 // t), b1 * (n1 // t) + (bc - b1) * (n2 // t))

    return pl.pallas_call(
        functools.partial(_window_meta_body, nb),
        grid=((ntok - n_real) // N_META,),
        in_specs=[
            pl.BlockSpec(memory_space=pltpu.SMEM),
            pl.BlockSpec((N_META, Q_COLS), lambda b: (n_real // N_META + b, 0)),
            pl.BlockSpec((N_KV, 1, HEAD_DIM, BLOCK), lambda b: (0, start_chunk(b), 0, 0)),
            pl.BlockSpec((N_KV, BLOCK, V_LANES), lambda b: (0, start_chunk(b) * sub, 0)),
            pl.BlockSpec((1, N_KV, HEAD_DIM, N_META), lambda b: (jnp.minimum(b, nb - 1), 0, 0, 0)),
            pl.BlockSpec((1, N_KV, N_META, V_LANES), lambda b: (jnp.minimum(b, nb - 1), 0, 0, 0)),
        ],
        out_specs=pl.BlockSpec((N_META, Q_COLS), lambda b: (n_real // N_META + b, 0)),
        out_shape=jax.ShapeDtypeStruct(qa.shape, _BF),
        compiler_params=_params(("arbitrary",)),
        name="window_meta_attn",
    )(sink, qa, kat, va, kmeta, vmeta)


_KEY_UNROLL = 4
_QUERY_BLOCK = 256

def _global_body(nvalid, nchunks, tq, q_ref, kt_ref, vt_ref, kmeta_ref, vmetat_ref, _prev_ref, o_ref,
                 qt_ref, s0_ref, s1_ref, m_ref, acc_ref):
    b = pl.program_id(0)

    cols = qt_ref.shape[1]
    blocks = [slice(c0, min(c0 + _QUERY_BLOCK, cols)) for c0 in range(0, cols, _QUERY_BLOCK)]

    def scores(c, s_ref):
        for sl in blocks:
            s_ref[:, sl] = _dot_tn(kt_ref[0, c], qt_ref[:, sl])

    def softmax_pv(c, s_ref):
        for sl in blocks:
            s = s_ref[:, sl]
            m_prev = m_ref[:, sl]
            m_new = jnp.maximum(m_prev, jnp.max(s, axis=0, keepdims=True))
            p = jnp.exp2(s - m_new).astype(_BF)
            acc_ref[:, sl] = jnp.exp2(m_prev - m_new) * acc_ref[:, sl] + _dot(vt_ref[0, c], p)
            m_ref[:, sl] = m_new

    @pl.when(b < nvalid)
    def _():
        if len(q_ref.shape) == 4 and q_ref.shape[0] == GROUP:
            qt_ref[...] = jnp.concatenate([q_ref[g, 0] for g in range(GROUP)], axis=1)
        else:
            qt_ref[...] = q_ref[0, 0]
        sm = _dot(kmeta_ref[0, 0], qt_ref[...])
        m0 = jnp.max(sm, axis=0, keepdims=True)
        m_ref[...] = m0
        acc_ref[...] = _dot(vmetat_ref[0, 0], jnp.exp2(sm - m0).astype(_BF))
        bufs = (s0_ref, s1_ref)
        scores(0, s0_ref)

        def group(i, carry):
            c0 = _KEY_UNROLL * i
            for u in range(_KEY_UNROLL):
                scores(c0 + u + 1, bufs[(u + 1) % 2])
                softmax_pv(c0 + u, bufs[u % 2])
            return carry

        full = (nchunks - 1) // _KEY_UNROLL
        lax.fori_loop(0, full, group, 0)
        for c in range(full * _KEY_UNROLL, nchunks):
            if c + 1 < nchunks:
                scores(c + 1, bufs[(c + 1) % 2])
            softmax_pv(c, bufs[c % 2])
        acc = acc_ref[...]
        o = (acc / acc[HEAD_DIM:HEAD_DIM + 1]).T.astype(_BF)
        for g in range(GROUP):
            o_ref[:, g * HEAD_DIM:(g + 1) * HEAD_DIM] = o[g * tq:(g + 1) * tq, :HEAD_DIM]

    @pl.when(b >= nvalid)
    def _():
        o_ref[...] = jnp.zeros(o_ref.shape, o_ref.dtype)


def _global_attention(q, kbt, vbt, kmeta, vmetat, prev, out_rows, *, tq, q_row0, q_rows_per_batch, grid_batches,
                      valid_batches, kv_batch0, kv_row0, n):
    t = TOKEN_TILE
    nchunks = n // t
    qt = q_rows_per_batch // tq
    assert q_row0 % tq == 0 and kv_row0 % n == 0 and q_rows_per_batch % tq == 0
    kvb = lambda b: jnp.minimum(b, valid_batches - 1)
    if tq == N_META:
        cols = V_LANES
        q_spec = pl.BlockSpec((1, 1, HEAD_DIM, cols), lambda b, j, i: (kv_batch0 + kvb(b), j, 0, 0))
    else:
        cols = GROUP * tq
        assert t % tq == 0 and q_row0 % t == 0 and q_rows_per_batch % t == 0
        q_spec = pl.BlockSpec(
            (GROUP, 1, HEAD_DIM, tq),
            lambda b, j, i: (j, (q_row0 + b * q_rows_per_batch) // t + i // (t // tq), 0, i % (t // tq)))
    in_specs = [
        q_spec,
        pl.BlockSpec((1, nchunks, HEAD_DIM, t), lambda b, j, i: (j, kv_row0 // n + kvb(b), 0, 0)),
        pl.BlockSpec((1, nchunks, V_LANES, t), lambda b, j, i: (j, kv_row0 // n + kvb(b), 0, 0)),
        pl.BlockSpec((1, 1, N_META, HEAD_DIM), lambda b, j, i: (kv_batch0 + kvb(b), j, 0, 0)),
        pl.BlockSpec((1, 1, V_LANES, N_META), lambda b, j, i: (kv_batch0 + kvb(b), j, 0, 0)),
    ]
    args = [q, kbt, vbt, kmeta, vmetat]
    aliases = {}
    body = functools.partial(_global_body, valid_batches, nchunks, tq)
    if prev is None:
        body = functools.partial(_global_body_noprev, body)
    else:
        in_specs.append(pl.BlockSpec(memory_space=pl.ANY))
        args.append(prev)
        aliases = {5: 0}
    return pl.pallas_call(
        body,
        grid=(grid_batches, N_KV, qt),
        in_specs=in_specs,
        out_specs=pl.BlockSpec((tq, GROUP * HEAD_DIM), lambda b, j, i: (q_row0 // tq + b * qt + i, j)),
        out_shape=jax.ShapeDtypeStruct((out_rows, Q_COLS), _BF),
        scratch_shapes=[
            pltpu.VMEM((HEAD_DIM, cols), _BF),
            pltpu.VMEM((t, cols), _F32),
            pltpu.VMEM((t, cols), _F32),
            pltpu.VMEM((1, cols), _F32),
            pltpu.VMEM((V_LANES, cols), _F32),
        ],
        input_output_aliases=aliases,
        compiler_params=_params(("parallel", "parallel", "arbitrary")),
        name=f"global_attn_tq{tq}_n{n}",
    )(*args)


def _global_body_noprev(body, q_ref, kt_ref, vt_ref, kmeta_ref, vmetat_ref, o_ref, *scratch):
    body(q_ref, kt_ref, vt_ref, kmeta_ref, vmetat_ref, None, o_ref, *scratch)


def _mix_body(h_ref, oa_ref, ob_ref, ga_ref, gb_ref, wa_ref, wb_ref, wo_ref, g_ref, o_ref):
    mix = (ga_ref[...].astype(_F32) * _dot(oa_ref[...], wa_ref[...])
           + gb_ref[...].astype(_F32) * _dot(ob_ref[...], wb_ref[...]))
    u = _dot(mix.astype(_BF), wo_ref[...])
    ms = jnp.mean(u * u, axis=-1, keepdims=True)
    o_ref[...] = h_ref[...] + u * lax.rsqrt(ms + EPS) * g_ref[...]


def _mix(h, oa, ob, ga, gb, wa, wb, wo, g):
    ntok = h.shape[0]
    t = TOKEN_TILE
    row = lambda i: (i, 0)
    return pl.pallas_call(
        _mix_body,
        grid=(ntok // t,),
        in_specs=[
            pl.BlockSpec((t, D_MODEL), row),
            pl.BlockSpec((t, Q_COLS), row), pl.BlockSpec((t, Q_COLS), row),
            pl.BlockSpec((t, D_MODEL), row), pl.BlockSpec((t, D_MODEL), row),
            _const_spec((Q_COLS, D_MODEL)), _const_spec((Q_COLS, D_MODEL)),
            _const_spec((D_MODEL, D_MODEL)), _const_spec((1, D_MODEL)),
        ],
        out_specs=pl.BlockSpec((t, D_MODEL), row),
        out_shape=jax.ShapeDtypeStruct(h.shape, _F32),
        input_output_aliases={0: 0},
        compiler_params=_params(("parallel",)),
        name="branch_mix",
    )(h, oa, ob, ga, gb, wa, wb, wo, g)


_FF_CHUNK = 256


def _ffn_body(h_ref, gpre_ref, wup_ref, wdown_ref, gpost_ref, o_ref, act_ref):
    h = h_ref[...]
    ms = jnp.mean(h * h, axis=-1, keepdims=True)
    xn = (h * lax.rsqrt(ms + EPS) * gpre_ref[...]).astype(_BF)
    for c in range(0, D_FF, _FF_CHUNK):
        a = _dot(xn, wup_ref[:, c:c + _FF_CHUNK])
        b = _dot(xn, wup_ref[:, D_FF + c:D_FF + c + _FF_CHUNK])
        act_ref[:, c:c + _FF_CHUNK] = (a * jax.nn.sigmoid(a) * b).astype(_BF)
    u = _dot(act_ref[...], wdown_ref[...])
    ms = jnp.mean(u * u, axis=-1, keepdims=True)
    o_ref[...] = h + u * lax.rsqrt(ms + EPS) * gpost_ref[...]


def _ffn(h, gpre, wup, wdown, gpost):
    ntok = h.shape[0]
    t = TOKEN_TILE
    row = lambda i: (i, 0)
    return pl.pallas_call(
        _ffn_body,
        grid=(ntok // t,),
        in_specs=[
            pl.BlockSpec((t, D_MODEL), row),
            _const_spec((1, D_MODEL)),
            _const_spec((D_MODEL, 2 * D_FF)),
            _const_spec((D_FF, D_MODEL)),
            _const_spec((1, D_MODEL)),
        ],
        out_specs=pl.BlockSpec((t, D_MODEL), row),
        out_shape=jax.ShapeDtypeStruct(h.shape, _F32),
        scratch_shapes=[pltpu.VMEM((t, D_FF), _BF)],
        input_output_aliases={0: 0},
        compiler_params=_params(("parallel",)),
        name="swiglu_ffn",
    )(h, gpre, wup, wdown, gpost)


def _rope_tables(layout, ntok):
    n_real, b1, n1, b2, n2 = layout
    idx = np.zeros((ntok,), np.int64)
    idx[:b1 * n1] = np.arange(b1 * n1) % n1
    idx[b1 * n1:n_real] = np.arange(b2 * n2) % n2
    rows = (idx // GRID_W).astype(np.float32)
    cols = (idx % GRID_W).astype(np.float32)
    rows[n_real:] = 0.0
    cols[n_real:] = 0.0
    freqs = ROPE_BASE ** (-jnp.arange(ROPE_FREQS, dtype=_F32) / ROPE_FREQS)
    ang_r = jnp.asarray(rows)[:, None] * freqs[None, :]
    ang_c = jnp.asarray(cols)[:, None] * freqs[None, :]
    cr, sr, cc, sc = jnp.cos(ang_r), jnp.sin(ang_r), jnp.cos(ang_c), jnp.sin(ang_c)
    cos64 = jnp.concatenate([cr, cr, cc, cc], axis=1)
    sin64 = jnp.concatenate([-sr, sr, -sc, sc], axis=1)
    return cos64.T, sin64.T


def _window_bias():
    slopes = 2.0 ** (-8.0 * np.arange(1, N_HEADS + 1, dtype=np.float64) / N_HEADS)
    rel = np.arange(BLOCK)[:, None] - (np.arange(3 * BLOCK) - BLOCK)[None, :]
    dist = np.abs(rel)
    band = dist <= BLOCK
    local = np.where(band[None], -slopes[:, None, None] * dist[None].astype(np.float64) * LOG2E, NEG_INF)
    extra = np.full((N_HEADS, BLOCK, BLOCK), NEG_INF)
    extra[:, :, :N_META] = 0.0
    base = np.concatenate([local, extra], axis=2)
    no_prev = base.copy()
    no_prev[:, :, :BLOCK] = NEG_INF
    no_next = base.copy()
    no_next[:, :, 2 * BLOCK:3 * BLOCK] = NEG_INF
    out = np.stack([base, no_prev, no_next]).reshape(3, N_KV, GROUP * BLOCK, 4 * BLOCK)
    return jnp.asarray(out, _F32)


def _rearranged_w_in(w_in):
    o = np.cumsum([0, Q_COLS, KV_COLS, KV_COLS, Q_COLS, KV_COLS, KV_COLS, D_MODEL, D_MODEL])
    qa, ka, va, qb, kb, vb, ga, gb = [w_in[..., o[i]:o[i + 1]] for i in range(8)]

    def pad_v(v):
        z = jnp.zeros(v.shape[:-1] + (V_LANES - HEAD_DIM,), v.dtype)
        return jnp.concatenate([v[..., :HEAD_DIM], z, v[..., HEAD_DIM:], z], axis=-1)

    w_rows = jnp.concatenate([qa, pad_v(va), ga, gb], axis=-1).astype(_BF)
    w_t = jnp.swapaxes(jnp.concatenate([ka, kb, pad_v(vb), qb], axis=-1), -1, -2).astype(_BF)
    return w_rows, w_t


def _tail_features(xt, n_real, nb):
    tail = xt[:, n_real // TOKEN_TILE:]
    tail = jnp.moveaxis(tail, 2, 1).reshape(xt.shape[0], xt.shape[2], -1)[:, :, :nb * N_META]
    return tail.reshape(xt.shape[0], xt.shape[2], nb, N_META)


def _meta_kv(kt, v, n_real, nb):
    kmeta = jnp.transpose(_tail_features(kt, n_real, nb), (2, 0, 1, 3))
    vmeta = jnp.transpose(v[:, n_real:n_real + nb * N_META].reshape(N_KV, nb, N_META, V_LANES), (1, 0, 2, 3))
    return kmeta, vmeta


def kernel(x_prompt, x_sample, meta_tokens, g_mix_pre, g_mix_post, g_ffn_pre, g_ffn_post, w_in, q_norm_b,
           k_norm_b, sink_a, w_branch_a, w_branch_b, w_out, w_ffn_up, w_ffn_down):
    b1, n1, _ = x_prompt.shape
    b2, n2, _ = x_sample.shape
    depth = w_in.shape[0]
    t = TOKEN_TILE
    assert n1 % t == 0 and n2 % t == 0 and (b1 * n1) % n2 == 0 and n1 % GRID_W == 0 and n2 % GRID_W == 0
    nb = b1 + b2
    n_real = b1 * n1 + b2 * n2
    tail = -(-(nb * N_META) // t) * t
    ntok = n_real + tail
    layout = (n_real, b1, n1, b2, n2)

    h = jnp.concatenate([
        x_prompt.reshape(b1 * n1, D_MODEL), x_sample.reshape(b2 * n2, D_MODEL),
        jnp.tile(meta_tokens.astype(x_prompt.dtype), (nb, 1)),
        jnp.zeros((tail - nb * N_META, D_MODEL), x_prompt.dtype)], axis=0)

    cost, sint = _rope_tables(layout, ntok)
    bias = _window_bias()
    w_rows, w_t = _rearranged_w_in(w_in)
    wa, wb, wo = w_branch_a.astype(_BF), w_branch_b.astype(_BF), w_out.astype(_BF)
    wup, wdown = w_ffn_up.astype(_BF), w_ffn_down.astype(_BF)
    row = lambda g: g.reshape(1, -1).astype(_F32)
    col = lambda g: g.reshape(-1, 1).astype(_F32)

    tq_real = 256
    for l in range(depth):
        qa, qbt, kat, kbt, va, vbt, ga, gb = _in_proj(
            h, row(g_mix_pre[l]), w_rows[l], w_t[l], cost, sint, col(q_norm_b[l]), col(k_norm_b[l]))
        kameta, vameta = _meta_kv(kat, va, n_real, nb)
        sink = sink_a[l].astype(_F32) * LOG2E
        kameta_pad = jnp.pad(kameta, ((0, 0), (0, 0), (0, 0), (0, BLOCK - N_META)))
        vameta_pad = jnp.pad(vameta, ((0, 0), (0, 0), (0, BLOCK - N_META), (0, 0)))

        oa = _window_meta_attention(sink, qa, kat, va, kameta, vameta, layout)
        oa = _window_attention(sink, qa, kat, va, kameta_pad, vameta_pad, bias, oa, layout)

        kbmeta = jnp.transpose(_tail_features(kbt, n_real, nb), (2, 0, 3, 1))
        vbmetat = jnp.transpose(_tail_features(vbt, n_real, nb), (2, 0, 1, 3))
        qmeta = _tail_features(qbt, n_real, nb).reshape(N_KV, GROUP, HEAD_DIM, nb, N_META)
        qmeta = jnp.transpose(qmeta, (3, 0, 2, 1, 4)).reshape(nb, N_KV, HEAD_DIM, GROUP * N_META)
        qmeta = jnp.pad(qmeta, ((0, 0), (0, 0), (0, 0), (0, V_LANES - GROUP * N_META)))

        glob = functools.partial(_global_attention, kbt=kbt, vbt=vbt, kmeta=kbmeta, vmetat=vbmetat, out_rows=ntok)
        ob = glob(qmeta, prev=None, tq=N_META, q_row0=n_real, q_rows_per_batch=N_META, grid_batches=b1,
                  valid_batches=b1, kv_batch0=0, kv_row0=0, n=n1)
        ob = glob(qmeta, prev=ob, tq=N_META, q_row0=n_real + b1 * N_META, q_rows_per_batch=N_META,
                  grid_batches=tail // N_META - b1, valid_batches=b2, kv_batch0=b1, kv_row0=b1 * n1, n=n2)
        ob = glob(qbt, prev=ob, tq=tq_real, q_row0=0, q_rows_per_batch=n1, grid_batches=b1, valid_batches=b1,
                  kv_batch0=0, kv_row0=0, n=n1)
        ob = glob(qbt, prev=ob, tq=tq_real, q_row0=b1 * n1, q_rows_per_batch=n2, grid_batches=b2, valid_batches=b2,
                  kv_batch0=b1, kv_row0=b1 * n1, n=n2)

        h = _mix(h, oa, ob, ga, gb, wa[l], wb[l], wo[l], row(g_mix_post[l]))
        h = _ffn(h, row(g_ffn_pre[l]), wup[l], wdown[l], row(g_ffn_post[l]))

    y_prompt = h[:b1 * n1].reshape(b1, n1, D_MODEL)
    y_sample = h[b1 * n1:n_real].reshape(b2, n2, D_MODEL)
    return (y_prompt, y_sample)
```

```python
import functools
import math

import jax
import jax.numpy as jnp
import numpy as np
from jax import lax
from jax.experimental import pallas as pl
from jax.experimental.pallas import tpu as pltpu

D_MODEL = 1024
HEAD_DIM = 64
N_HEADS = 8
N_KV = 2
GROUP = N_HEADS // N_KV
Q_COLS = N_HEADS * HEAD_DIM
KV_COLS = N_KV * HEAD_DIM
N_META = 16
BLOCK = 128
GRID_W = 64
ROPE_BASE = 10000.0
ROPE_FREQS = HEAD_DIM // 4
D_FF = 2816
EPS = 1e-6
NEG_INF = -1e30
SCALE = HEAD_DIM ** -0.5
LOG2E = math.log2(math.e)

V_LANES = 128
TOKEN_TILE = 512
DENSE_TILE = 1024
VMEM_LIMIT = 56 * 1024 * 1024

_C_QA = 0
_C_VA = _C_QA + Q_COLS
_C_GA = _C_VA + N_KV * V_LANES
_C_GB = _C_GA + D_MODEL
_C_END = _C_GB + D_MODEL
_R_KA = 0
_R_KB = _R_KA + KV_COLS
_R_VB = _R_KB + KV_COLS
_R_QB = _R_VB + N_KV * V_LANES
_R_END = _R_QB + Q_COLS

_BF = jnp.bfloat16
_F32 = jnp.float32


def _dot(a, b):
    return jnp.dot(a, b, preferred_element_type=_F32)


def _dot_nt(a, b):
    return lax.dot_general(a, b, (((1,), (1,)), ((), ())), preferred_element_type=_F32)


def _dot_tn(a, b):
    return lax.dot_general(a, b, (((0,), (0,)), ((), ())), preferred_element_type=_F32)


def _params(sem, vmem=VMEM_LIMIT):
    return pltpu.CompilerParams(dimension_semantics=sem, vmem_limit_bytes=vmem)


def _const_spec(shape):
    nd = len(shape)
    return pl.BlockSpec(shape, lambda *_: (0,) * nd, pipeline_mode=pl.Buffered(1))


def _in_proj_body(h_ref, g_ref, w_ref, wt_ref, cost_ref, sint_ref, qg_ref, kg_ref,
                  qa_ref, qbt_ref, kat_ref, kbt_ref, va_ref, vbt_ref, ga_ref, gb_ref):
    h = h_ref[...]
    ms = jnp.mean(h * h, axis=-1, keepdims=True)
    xn = (h * lax.rsqrt(ms + EPS) * g_ref[...]).astype(_BF)

    qa = _dot(xn, w_ref[:, _C_QA:_C_QA + Q_COLS])
    qa_ref[...] = (qa * (SCALE * LOG2E)).astype(_BF)

    ones_col = (lax.broadcasted_iota(jnp.int32, (1, N_KV * V_LANES), 1) % V_LANES == HEAD_DIM).astype(_F32)
    va = (_dot(xn, w_ref[:, _C_VA:_C_VA + N_KV * V_LANES]) + ones_col).astype(_BF)
    for j in range(N_KV):
        va_ref[j] = va[:, j * V_LANES:(j + 1) * V_LANES]

    ga_ref[...] = jax.nn.sigmoid(_dot(xn, w_ref[:, _C_GA:_C_GA + D_MODEL])).astype(_BF)
    gb_ref[...] = jax.nn.sigmoid(_dot(xn, w_ref[:, _C_GB:_C_GB + D_MODEL])).astype(_BF)

    tt = _dot_nt(wt_ref[...], xn)
    cost = cost_ref[...]
    sint = sint_ref[...]
    f = ROPE_FREQS

    def norm_rope(x, gain):
        x = x * lax.rsqrt(jnp.mean(x * x, axis=0, keepdims=True) + EPS) * gain
        partner = jnp.concatenate([x[f:2 * f], x[0:f], x[3 * f:4 * f], x[2 * f:3 * f]], axis=0)
        return x * cost + partner * sint

    ones_row = (lax.broadcasted_iota(jnp.int32, (V_LANES, 1), 0) == HEAD_DIM).astype(_F32)
    for j in range(N_KV):
        kat_ref[j, 0] = tt[_R_KA + j * HEAD_DIM:_R_KA + (j + 1) * HEAD_DIM].astype(_BF)
        kbt_ref[j, 0] = norm_rope(tt[_R_KB + j * HEAD_DIM:_R_KB + (j + 1) * HEAD_DIM], kg_ref[...]).astype(_BF)
        vbt_ref[j, 0] = (tt[_R_VB + j * V_LANES:_R_VB + (j + 1) * V_LANES] + ones_row).astype(_BF)
    for hd in range(N_HEADS):
        q = norm_rope(tt[_R_QB + hd * HEAD_DIM:_R_QB + (hd + 1) * HEAD_DIM], qg_ref[...])
        qbt_ref[hd, 0] = (q * (SCALE * LOG2E)).astype(_BF)


def _in_proj(h, g, w, wt, cost, sint, qg, kg):
    ntok = h.shape[0]
    nch = ntok // TOKEN_TILE
    t = TOKEN_TILE
    row = lambda i: (i, 0)
    chunk = lambda i: (0, i, 0, 0)
    out_shape = (
        jax.ShapeDtypeStruct((ntok, Q_COLS), _BF),
        jax.ShapeDtypeStruct((N_HEADS, nch, HEAD_DIM, t), _BF),
        jax.ShapeDtypeStruct((N_KV, nch, HEAD_DIM, t), _BF),
        jax.ShapeDtypeStruct((N_KV, nch, HEAD_DIM, t), _BF),
        jax.ShapeDtypeStruct((N_KV, ntok, V_LANES), _BF),
        jax.ShapeDtypeStruct((N_KV, nch, V_LANES, t), _BF),
        jax.ShapeDtypeStruct((ntok, D_MODEL), _BF),
        jax.ShapeDtypeStruct((ntok, D_MODEL), _BF),
    )
    kt_spec = pl.BlockSpec((N_KV, 1, HEAD_DIM, t), chunk)
    return pl.pallas_call(
        _in_proj_body,
        grid=(nch,),
        in_specs=[
            pl.BlockSpec((t, D_MODEL), row),
            _const_spec((1, D_MODEL)),
            _const_spec((D_MODEL, _C_END)),
            _const_spec((_R_END, D_MODEL)),
            pl.BlockSpec((HEAD_DIM, t), lambda i: (0, i)),
            pl.BlockSpec((HEAD_DIM, t), lambda i: (0, i)),
            _const_spec((HEAD_DIM, 1)),
            _const_spec((HEAD_DIM, 1)),
        ],
        out_specs=(
            pl.BlockSpec((t, Q_COLS), row),
            pl.BlockSpec((N_HEADS, 1, HEAD_DIM, t), chunk),
            kt_spec, kt_spec,
            pl.BlockSpec((N_KV, t, V_LANES), lambda i: (0, i, 0)),
            pl.BlockSpec((N_KV, 1, V_LANES, t), chunk),
            pl.BlockSpec((t, D_MODEL), row), pl.BlockSpec((t, D_MODEL), row),
        ),
        out_shape=out_shape,
        compiler_params=_params(("parallel",)),
        name="in_proj",
    )(h, g, w, wt, cost, sint, qg, kg)


def _stack_heads(q, j):
    base = j * GROUP * HEAD_DIM
    return jnp.concatenate([q[:, base + g * HEAD_DIM: base + (g + 1) * HEAD_DIM] for g in range(GROUP)], axis=0)


def _sink_rows(sink_ref, j, rows, lanes):
    return jnp.concatenate([jnp.full((rows, lanes), sink_ref[j * GROUP + g], _F32) for g in range(GROUP)], axis=0)


def _window_body(geom, sink_ref, q_ref, kmain_ref, kprev_ref, knext_ref, vmain_ref, vprev_ref, vnext_ref,
                 kmeta_ref, vmeta_ref, bias_ref, _tail_ref, o_ref):
    t1, tpb1, tpb2 = geom
    t = pl.program_id(0)
    in_prompt = t < t1
    is_first = jnp.where(in_prompt, t % tpb1 == 0, (t - t1) % tpb2 == 0)
    is_last = jnp.where(in_prompt, t % tpb1 == tpb1 - 1, (t - t1) % tpb2 == tpb2 - 1)
    nblk = TOKEN_TILE // BLOCK
    lane_tiles = 4 * BLOCK // V_LANES
    q_all = q_ref[...]
    for j in range(N_KV):
        kcat = jnp.concatenate([kprev_ref[j, 0], kmain_ref[j, 0], knext_ref[j, 0]], axis=1)
        vcat = jnp.concatenate([vprev_ref[j], vmain_ref[j], vnext_ref[j]], axis=0)
        kmeta = kmeta_ref[0, j]
        vmeta = vmeta_ref[0, j]
        sink = _sink_rows(sink_ref, j, BLOCK, V_LANES)
        for r in range(nblk):
            variant = 0
            if r == 0:
                variant = jnp.where(is_first, 1, 0)
            if r == nblk - 1:
                variant = jnp.where(is_last, 2, variant)
            qs = _stack_heads(q_all[r * BLOCK:(r + 1) * BLOCK], j)
            kwin = jnp.concatenate([kcat[:, r * BLOCK:(r + 3) * BLOCK], kmeta], axis=1)
            vwin = jnp.concatenate([vcat[r * BLOCK:(r + 3) * BLOCK], vmeta], axis=0)
            s = _dot(qs, kwin) + bias_ref[variant, j]
            m = jnp.maximum(jnp.broadcast_to(s.max(axis=1, keepdims=True), sink.shape), sink)
            p = jnp.exp2(s - jnp.concatenate([m] * lane_tiles, axis=1)).astype(_BF)
            acc = _dot(p, vwin)
            l = jnp.broadcast_to(acc[:, HEAD_DIM:HEAD_DIM + 1], sink.shape) + jnp.exp2(sink - m)
            o = (acc / l).astype(_BF)
            for g in range(GROUP):
                c0 = (j * GROUP + g) * HEAD_DIM
                o_ref[r * BLOCK:(r + 1) * BLOCK, c0:c0 + HEAD_DIM] = o[g * BLOCK:(g + 1) * BLOCK, :HEAD_DIM]


def _window_attention(sink, qa, kat, va, kmeta, vmeta, bias, tail, layout):
    n_real, b1, n1, b2, n2 = layout
    t = TOKEN_TILE
    nch = kat.shape[1]
    t1 = b1 * n1 // t
    tpb1, tpb2 = n1 // t, n2 // t
    sub = t // BLOCK

    def bid(i):
        return jnp.where(i < t1, i // tpb1, b1 + (i - t1) // tpb2)

    return pl.pallas_call(
        functools.partial(_window_body, (t1, tpb1, tpb2)),
        grid=(n_real // t,),
        in_specs=[
            pl.BlockSpec(memory_space=pltpu.SMEM),
            pl.BlockSpec((t, Q_COLS), lambda i: (i, 0)),
            pl.BlockSpec((N_KV, 1, HEAD_DIM, t), lambda i: (0, i, 0, 0)),
            pl.BlockSpec((N_KV, 1, HEAD_DIM, BLOCK), lambda i: (0, jnp.maximum(i - 1, 0), 0, sub - 1)),
            pl.BlockSpec((N_KV, 1, HEAD_DIM, BLOCK), lambda i: (0, jnp.minimum(i + 1, nch - 1), 0, 0)),
            pl.BlockSpec((N_KV, t, V_LANES), lambda i: (0, i, 0)),
            pl.BlockSpec((N_KV, BLOCK, V_LANES), lambda i: (0, jnp.maximum(i * sub - 1, 0), 0)),
            pl.BlockSpec((N_KV, BLOCK, V_LANES), lambda i: (0, jnp.minimum(i + 1, nch - 1) * sub, 0)),
            pl.BlockSpec((1, N_KV, HEAD_DIM, BLOCK), lambda i: (bid(i), 0, 0, 0)),
            pl.BlockSpec((1, N_KV, BLOCK, V_LANES), lambda i: (bid(i), 0, 0, 0)),
            _const_spec((3, N_KV, GROUP * BLOCK, 4 * BLOCK)),
            pl.BlockSpec(memory_space=pl.ANY),
        ],
        out_specs=pl.BlockSpec((t, Q_COLS), lambda i: (i, 0)),
        out_shape=jax.ShapeDtypeStruct(qa.shape, _BF),
        input_output_aliases={11: 0},
        compiler_params=_params(("parallel",)),
        name="window_attn",
    )(sink, qa, kat, kat, kat, va, va, va, kmeta, vmeta, bias, tail)


def _window_meta_body(nb, sink_ref, q_ref, kfirst_ref, vfirst_ref, kmeta_ref, vmeta_ref, o_ref):
    b = pl.program_id(0)

    @pl.when(b < nb)
    def _():
        q_all = q_ref[...]
        for j in range(N_KV):
            qs = _stack_heads(q_all, j)
            sink = _sink_rows(sink_ref, j, N_META, 1)
            sm = _dot(qs, kmeta_ref[0, j])
            sf = _dot(qs, kfirst_ref[j, 0])
            m = jnp.maximum(jnp.maximum(sm.max(axis=1, keepdims=True), sf.max(axis=1, keepdims=True)), sink)
            pm = jnp.exp2(sm - m).astype(_BF)
            pf = jnp.exp2(sf - m).astype(_BF)
            acc = _dot(pm, vmeta_ref[0, j]) + _dot(pf, vfirst_ref[j])
            l = acc[:, HEAD_DIM:HEAD_DIM + 1] + jnp.exp2(sink - m)
            o = (acc[:, :HEAD_DIM] / l).astype(_BF)
            for g in range(GROUP):
                c0 = (j * GROUP + g) * HEAD_DIM
                o_ref[:, c0:c0 + HEAD_DIM] = o[g * N_META:(g + 1) * N_META]

    @pl.when(b >= nb)
    def _():
        o_ref[...] = jnp.zeros(o_ref.shape, o_ref.dtype)


def _window_meta_attention(sink, qa, kat, va, kmeta, vmeta, layout):
    n_real, b1, n1, b2, n2 = layout
    ntok = qa.shape[0]
    nb = b1 + b2
    t = TOKEN_TILE
    sub = t // BLOCK

    def start_chunk(b):
        bc = jnp.minimum(b, nb - 1)
        return jnp.where(bc < b1, bc * (n1 // t), b1 * (n1 // t) + (bc - b1) * (n2 // t))

    return pl.pallas_call(
        functools.partial(_window_meta_body, nb),
        grid=((ntok - n_real) // N_META,),
        in_specs=[
            pl.BlockSpec(memory_space=pltpu.SMEM),
            pl.BlockSpec((N_META, Q_COLS), lambda b: (n_real // N_META + b, 0)),
            pl.BlockSpec((N_KV, 1, HEAD_DIM, BLOCK), lambda b: (0, start_chunk(b), 0, 0)),
            pl.BlockSpec((N_KV, BLOCK, V_LANES), lambda b: (0, start_chunk(b) * sub, 0)),
            pl.BlockSpec((1, N_KV, HEAD_DIM, N_META), lambda b: (jnp.minimum(b, nb - 1), 0, 0, 0)),
            pl.BlockSpec((1, N_KV, N_META, V_LANES), lambda b: (jnp.minimum(b, nb - 1), 0, 0, 0)),
        ],
        out_specs=pl.BlockSpec((N_META, Q_COLS), lambda b: (n_real // N_META + b, 0)),
        out_shape=jax.ShapeDtypeStruct(qa.shape, _BF),
        compiler_params=_params(("arbitrary",)),
        name="window_meta_attn",
    )(sink, qa, kat, va, kmeta, vmeta)


_KEY_UNROLL = 4
_QUERY_BLOCK = 256

def _global_body(nvalid, nchunks, tq, q_ref, kt_ref, vt_ref, kmeta_ref, vmetat_ref, _prev_ref, o_ref,
                 qt_ref, s0_ref, s1_ref, m_ref, acc_ref):
    b = pl.program_id(0)

    cols = qt_ref.shape[1]
    blocks = [slice(c0, min(c0 + _QUERY_BLOCK, cols)) for c0 in range(0, cols, _QUERY_BLOCK)]

    def scores(c, s_ref):
        for sl in blocks:
            s_ref[:, sl] = _dot_tn(kt_ref[0, c], qt_ref[:, sl])

    def softmax_pv(c, s_ref):
        for sl in blocks:
            s = s_ref[:, sl]
            m_prev = m_ref[:, sl]
            m_new = jnp.maximum(m_prev, jnp.max(s, axis=0, keepdims=True))
            p = jnp.exp2(s - m_new).astype(_BF)
            acc_ref[:, sl] = jnp.exp2(m_prev - m_new) * acc_ref[:, sl] + _dot(vt_ref[0, c], p)
            m_ref[:, sl] = m_new

    @pl.when(b < nvalid)
    def _():
        if len(q_ref.shape) == 4 and q_ref.shape[0] == GROUP:
            qt_ref[...] = jnp.concatenate([q_ref[g, 0] for g in range(GROUP)], axis=1)
        else:
            qt_ref[...] = q_ref[0, 0]
        sm = _dot(kmeta_ref[0, 0], qt_ref[...])
        m0 = jnp.max(sm, axis=0, keepdims=True)
        m_ref[...] = m0
        acc_ref[...] = _dot(vmetat_ref[0, 0], jnp.exp2(sm - m0).astype(_BF))
        bufs = (s0_ref, s1_ref)
        scores(0, s0_ref)

        unroll = _KEY_UNROLL if nchunks > _KEY_UNROLL else 2

        def group(i, carry):
            c0 = unroll * i
            for u in range(unroll):
                scores(c0 + u + 1, bufs[(u + 1) % 2])
                softmax_pv(c0 + u, bufs[u % 2])
            return carry

        full = (nchunks - 1) // unroll
        lax.fori_loop(0, full, group, 0)
        for c in range(full * unroll, nchunks):
            if c + 1 < nchunks:
                scores(c + 1, bufs[(c + 1) % 2])
            softmax_pv(c, bufs[c % 2])
        acc = acc_ref[...]
        o = (acc / acc[HEAD_DIM:HEAD_DIM + 1]).T.astype(_BF)
        for g in range(GROUP):
            o_ref[:, g * HEAD_DIM:(g + 1) * HEAD_DIM] = o[g * tq:(g + 1) * tq, :HEAD_DIM]

    @pl.when(b >= nvalid)
    def _():
        o_ref[...] = jnp.zeros(o_ref.shape, o_ref.dtype)


def _global_attention(q, kbt, vbt, kmeta, vmetat, prev, out_rows, *, tq, q_row0, q_rows_per_batch, grid_batches,
                      valid_batches, kv_batch0, kv_row0, n):
    t = TOKEN_TILE
    nchunks = n // t
    qt = q_rows_per_batch // tq
    assert q_row0 % tq == 0 and kv_row0 % n == 0 and q_rows_per_batch % tq == 0
    kvb = lambda b: jnp.minimum(b, valid_batches - 1)
    if tq == N_META:
        cols = V_LANES
        q_spec = pl.BlockSpec((1, 1, HEAD_DIM, cols), lambda b, j, i: (kv_batch0 + kvb(b), j, 0, 0))
    else:
        cols = GROUP * tq
        assert t % tq == 0 and q_row0 % t == 0 and q_rows_per_batch % t == 0
        q_spec = pl.BlockSpec(
            (GROUP, 1, HEAD_DIM, tq),
            lambda b, j, i: (j, (q_row0 + b * q_rows_per_batch) // t + i // (t // tq), 0, i % (t // tq)))
    in_specs = [
        q_spec,
        pl.BlockSpec((1, nchunks, HEAD_DIM, t), lambda b, j, i: (j, kv_row0 // n + kvb(b), 0, 0)),
        pl.BlockSpec((1, nchunks, V_LANES, t), lambda b, j, i: (j, kv_row0 // n + kvb(b), 0, 0)),
        pl.BlockSpec((1, 1, N_META, HEAD_DIM), lambda b, j, i: (kv_batch0 + kvb(b), j, 0, 0)),
        pl.BlockSpec((1, 1, V_LANES, N_META), lambda b, j, i: (kv_batch0 + kvb(b), j, 0, 0)),
    ]
    args = [q, kbt, vbt, kmeta, vmetat]
    aliases = {}
    body = functools.partial(_global_body, valid_batches, nchunks, tq)
    if prev is None:
        body = functools.partial(_global_body_noprev, body)
    else:
        in_specs.append(pl.BlockSpec(memory_space=pl.ANY))
        args.append(prev)
        aliases = {5: 0}
    return pl.pallas_call(
        body,
        grid=(grid_batches, N_KV, qt),
        in_specs=in_specs,
        out_specs=pl.BlockSpec((tq, GROUP * HEAD_DIM), lambda b, j, i: (q_row0 // tq + b * qt + i, j)),
        out_shape=jax.ShapeDtypeStruct((out_rows, Q_COLS), _BF),
        scratch_shapes=[
            pltpu.VMEM((HEAD_DIM, cols), _BF),
            pltpu.VMEM((t, cols), _F32),
            pltpu.VMEM((t, cols), _F32),
            pltpu.VMEM((1, cols), _F32),
            pltpu.VMEM((V_LANES, cols), _F32),
        ],
        input_output_aliases=aliases,
        compiler_params=_params(("parallel", "parallel", "arbitrary")),
        name=f"global_attn_tq{tq}_n{n}",
    )(*args)


def _global_body_noprev(body, q_ref, kt_ref, vt_ref, kmeta_ref, vmetat_ref, o_ref, *scratch):
    body(q_ref, kt_ref, vt_ref, kmeta_ref, vmetat_ref, None, o_ref, *scratch)


def _mix_body(h_ref, oa_ref, ob_ref, ga_ref, gb_ref, wa_ref, wb_ref, wo_ref, g_ref, o_ref):
    mix = (ga_ref[...].astype(_F32) * _dot(oa_ref[...], wa_ref[...])
           + gb_ref[...].astype(_F32) * _dot(ob_ref[...], wb_ref[...]))
    u = _dot(mix.astype(_BF), wo_ref[...])
    ms = jnp.mean(u * u, axis=-1, keepdims=True)
    o_ref[...] = h_ref[...] + u * lax.rsqrt(ms + EPS) * g_ref[...]


def _mix(h, oa, ob, ga, gb, wa, wb, wo, g):
    ntok = h.shape[0]
    t = DENSE_TILE
    row = lambda i: (i, 0)
    return pl.pallas_call(
        _mix_body,
        grid=(ntok // t,),
        in_specs=[
            pl.BlockSpec((t, D_MODEL), row),
            pl.BlockSpec((t, Q_COLS), row), pl.BlockSpec((t, Q_COLS), row),
            pl.BlockSpec((t, D_MODEL), row), pl.BlockSpec((t, D_MODEL), row),
            _const_spec((Q_COLS, D_MODEL)), _const_spec((Q_COLS, D_MODEL)),
            _const_spec((D_MODEL, D_MODEL)), _const_spec((1, D_MODEL)),
        ],
        out_specs=pl.BlockSpec((t, D_MODEL), row),
        out_shape=jax.ShapeDtypeStruct(h.shape, _F32),
        input_output_aliases={0: 0},
        compiler_params=_params(("parallel",)),
        name="branch_mix",
    )(h, oa, ob, ga, gb, wa, wb, wo, g)


_FF_CHUNK = 256


def _ffn_body(h_ref, gpre_ref, wup_ref, wdown_ref, gpost_ref, o_ref, act_ref):
    h = h_ref[...]
    ms = jnp.mean(h * h, axis=-1, keepdims=True)
    xn = (h * lax.rsqrt(ms + EPS) * gpre_ref[...]).astype(_BF)
    for c in range(0, D_FF, _FF_CHUNK):
        a = _dot(xn, wup_ref[:, c:c + _FF_CHUNK])
        b = _dot(xn, wup_ref[:, D_FF + c:D_FF + c + _FF_CHUNK])
        act_ref[:, c:c + _FF_CHUNK] = (a * jax.nn.sigmoid(a) * b).astype(_BF)
    u = _dot(act_ref[...], wdown_ref[...])
    ms = jnp.mean(u * u, axis=-1, keepdims=True)
    o_ref[...] = h + u * lax.rsqrt(ms + EPS) * gpost_ref[...]


def _ffn(h, gpre, wup, wdown, gpost):
    ntok = h.shape[0]
    t = DENSE_TILE
    row = lambda i: (i, 0)
    return pl.pallas_call(
        _ffn_body,
        grid=(ntok // t,),
        in_specs=[
            pl.BlockSpec((t, D_MODEL), row),
            _const_spec((1, D_MODEL)),
            _const_spec((D_MODEL, 2 * D_FF)),
            _const_spec((D_FF, D_MODEL)),
            _const_spec((1, D_MODEL)),
        ],
        out_specs=pl.BlockSpec((t, D_MODEL), row),
        out_shape=jax.ShapeDtypeStruct(h.shape, _F32),
        scratch_shapes=[pltpu.VMEM((t, D_FF), _BF)],
        input_output_aliases={0: 0},
        compiler_params=_params(("parallel",)),
        name="swiglu_ffn",
    )(h, gpre, wup, wdown, gpost)


def _rope_tables(layout, ntok):
    n_real, b1, n1, b2, n2 = layout
    idx = np.zeros((ntok,), np.int64)
    idx[:b1 * n1] = np.arange(b1 * n1) % n1
    idx[b1 * n1:n_real] = np.arange(b2 * n2) % n2
    rows = (idx // GRID_W).astype(np.float32)
    cols = (idx % GRID_W).astype(np.float32)
    rows[n_real:] = 0.0
    cols[n_real:] = 0.0
    freqs = ROPE_BASE ** (-jnp.arange(ROPE_FREQS, dtype=_F32) / ROPE_FREQS)
    ang_r = jnp.asarray(rows)[:, None] * freqs[None, :]
    ang_c = jnp.asarray(cols)[:, None] * freqs[None, :]
    cr, sr, cc, sc = jnp.cos(ang_r), jnp.sin(ang_r), jnp.cos(ang_c), jnp.sin(ang_c)
    cos64 = jnp.concatenate([cr, cr, cc, cc], axis=1)
    sin64 = jnp.concatenate([-sr, sr, -sc, sc], axis=1)
    return cos64.T, sin64.T


def _window_bias():
    slopes = 2.0 ** (-8.0 * np.arange(1, N_HEADS + 1, dtype=np.float64) / N_HEADS)
    rel = np.arange(BLOCK)[:, None] - (np.arange(3 * BLOCK) - BLOCK)[None, :]
    dist = np.abs(rel)
    band = dist <= BLOCK
    local = np.where(band[None], -slopes[:, None, None] * dist[None].astype(np.float64) * LOG2E, NEG_INF)
    extra = np.full((N_HEADS, BLOCK, BLOCK), NEG_INF)
    extra[:, :, :N_META] = 0.0
    base = np.concatenate([local, extra], axis=2)
    no_prev = base.copy()
    no_prev[:, :, :BLOCK] = NEG_INF
    no_next = base.copy()
    no_next[:, :, 2 * BLOCK:3 * BLOCK] = NEG_INF
    out = np.stack([base, no_prev, no_next]).reshape(3, N_KV, GROUP * BLOCK, 4 * BLOCK)
    return jnp.asarray(out, _F32)


def _rearranged_w_in(w_in):
    o = np.cumsum([0, Q_COLS, KV_COLS, KV_COLS, Q_COLS, KV_COLS, KV_COLS, D_MODEL, D_MODEL])
    qa, ka, va, qb, kb, vb, ga, gb = [w_in[..., o[i]:o[i + 1]] for i in range(8)]

    def pad_v(v):
        z = jnp.zeros(v.shape[:-1] + (V_LANES - HEAD_DIM,), v.dtype)
        return jnp.concatenate([v[..., :HEAD_DIM], z, v[..., HEAD_DIM:], z], axis=-1)

    w_rows = jnp.concatenate([qa, pad_v(va), ga, gb], axis=-1).astype(_BF)
    w_t = jnp.swapaxes(jnp.concatenate([ka, kb, pad_v(vb), qb], axis=-1), -1, -2).astype(_BF)
    return w_rows, w_t


def _tail_features(xt, n_real, nb):
    tail = xt[:, n_real // TOKEN_TILE:]
    tail = jnp.moveaxis(tail, 2, 1).reshape(xt.shape[0], xt.shape[2], -1)[:, :, :nb * N_META]
    return tail.reshape(xt.shape[0], xt.shape[2], nb, N_META)


def _meta_kv(kt, v, n_real, nb):
    kmeta = jnp.transpose(_tail_features(kt, n_real, nb), (2, 0, 1, 3))
    vmeta = jnp.transpose(v[:, n_real:n_real + nb * N_META].reshape(N_KV, nb, N_META, V_LANES), (1, 0, 2, 3))
    return kmeta, vmeta


def kernel(x_prompt, x_sample, meta_tokens, g_mix_pre, g_mix_post, g_ffn_pre, g_ffn_post, w_in, q_norm_b,
           k_norm_b, sink_a, w_branch_a, w_branch_b, w_out, w_ffn_up, w_ffn_down):
    b1, n1, _ = x_prompt.shape
    b2, n2, _ = x_sample.shape
    depth = w_in.shape[0]
    t = TOKEN_TILE
    assert n1 % t == 0 and n2 % t == 0 and (b1 * n1) % n2 == 0 and n1 % GRID_W == 0 and n2 % GRID_W == 0
    nb = b1 + b2
    n_real = b1 * n1 + b2 * n2
    ntok = -(-(n_real + nb * N_META) // DENSE_TILE) * DENSE_TILE
    tail = ntok - n_real
    assert tail % t == 0
    layout = (n_real, b1, n1, b2, n2)

    h = jnp.concatenate([
        x_prompt.reshape(b1 * n1, D_MODEL), x_sample.reshape(b2 * n2, D_MODEL),
        jnp.tile(meta_tokens.astype(x_prompt.dtype), (nb, 1)),
        jnp.zeros((tail - nb * N_META, D_MODEL), x_prompt.dtype)], axis=0)

    cost, sint = _rope_tables(layout, ntok)
    bias = _window_bias()
    w_rows, w_t = _rearranged_w_in(w_in)
    wa, wb, wo = w_branch_a.astype(_BF), w_branch_b.astype(_BF), w_out.astype(_BF)
    wup, wdown = w_ffn_up.astype(_BF), w_ffn_down.astype(_BF)
    row = lambda g: g.reshape(1, -1).astype(_F32)
    col = lambda g: g.reshape(-1, 1).astype(_F32)

    tq_real = 256
    for l in range(depth):
        qa, qbt, kat, kbt, va, vbt, ga, gb = _in_proj(
            h, row(g_mix_pre[l]), w_rows[l], w_t[l], cost, sint, col(q_norm_b[l]), col(k_norm_b[l]))
        kameta, vameta = _meta_kv(kat, va, n_real, nb)
        sink = sink_a[l].astype(_F32) * LOG2E
        kameta_pad = jnp.pad(kameta, ((0, 0), (0, 0), (0, 0), (0, BLOCK - N_META)))
        vameta_pad = jnp.pad(vameta, ((0, 0), (0, 0), (0, BLOCK - N_META), (0, 0)))

        oa = _window_meta_attention(sink, qa, kat, va, kameta, vameta, layout)
        oa = _window_attention(sink, qa, kat, va, kameta_pad, vameta_pad, bias, oa, layout)

        kbmeta = jnp.transpose(_tail_features(kbt, n_real, nb), (2, 0, 3, 1))
        vbmetat = jnp.transpose(_tail_features(vbt, n_real, nb), (2, 0, 1, 3))
        qmeta = _tail_features(qbt, n_real, nb).reshape(N_KV, GROUP, HEAD_DIM, nb, N_META)
        qmeta = jnp.transpose(qmeta, (3, 0, 2, 1, 4)).reshape(nb, N_KV, HEAD_DIM, GROUP * N_META)
        qmeta = jnp.pad(qmeta, ((0, 0), (0, 0), (0, 0), (0, V_LANES - GROUP * N_META)))

        glob = functools.partial(_global_attention, kbt=kbt, vbt=vbt, kmeta=kbmeta, vmetat=vbmetat, out_rows=ntok)
        ob = glob(qmeta, prev=None, tq=N_META, q_row0=n_real, q_rows_per_batch=N_META, grid_batches=b1,
                  valid_batches=b1, kv_batch0=0, kv_row0=0, n=n1)
        ob = glob(qmeta, prev=ob, tq=N_META, q_row0=n_real + b1 * N_META, q_rows_per_batch=N_META,
                  grid_batches=tail // N_META - b1, valid_batches=b2, kv_batch0=b1, kv_row0=b1 * n1, n=n2)
        ob = glob(qbt, prev=ob, tq=2 * tq_real, q_row0=0, q_rows_per_batch=n1, grid_batches=b1, valid_batches=b1,
                  kv_batch0=0, kv_row0=0, n=n1)
        ob = glob(qbt, prev=ob, tq=2 * tq_real, q_row0=b1 * n1, q_rows_per_batch=n2, grid_batches=b2, valid_batches=b2,
                  kv_batch0=b1, kv_row0=b1 * n1, n=n2)

        h = _mix(h, oa, ob, ga, gb, wa[l], wb[l], wo[l], row(g_mix_post[l]))
        h = _ffn(h, row(g_ffn_pre[l]), wup[l], wdown[l], row(g_ffn_post[l]))

    y_prompt = h[:b1 * n1].reshape(b1, n1, D_MODEL)
    y_sample = h[b1 * n1:n_real].reshape(b2, n2, D_MODEL)
    return (y_prompt, y_sample)
```

```python
import functools
import math

import jax
import jax.numpy as jnp
import numpy as np
from jax import lax
from jax.experimental import pallas as pl
from jax.experimental.pallas import tpu as pltpu

D_MODEL = 1024
HEAD_DIM = 64
N_HEADS = 8
N_KV = 2
GROUP = N_HEADS // N_KV
Q_COLS = N_HEADS * HEAD_DIM
KV_COLS = N_KV * HEAD_DIM
N_META = 16
BLOCK = 128
GRID_W = 64
ROPE_BASE = 10000.0
ROPE_FREQS = HEAD_DIM // 4
D_FF = 2816
EPS = 1e-6
NEG_INF = -1e30
SCALE = HEAD_DIM ** -0.5
LOG2E = math.log2(math.e)

QK_ROWS = HEAD_DIM + 16
SAFE_OFFSET_MAX = 40.0
V_LANES = 128
TOKEN_TILE = 512
DENSE_TILE = 1024
VMEM_LIMIT = 56 * 1024 * 1024

_C_QA = 0
_C_VA = _C_QA + Q_COLS
_C_GA = _C_VA + N_KV * V_LANES
_C_GB = _C_GA + D_MODEL
_C_END = _C_GB + D_MODEL
_R_KA = 0
_R_KB = _R_KA + KV_COLS
_R_VB = _R_KB + KV_COLS
_R_QB = _R_VB + N_KV * V_LANES
_R_END = _R_QB + Q_COLS

_BF = jnp.bfloat16
_F32 = jnp.float32


def _dot(a, b):
    return jnp.dot(a, b, preferred_element_type=_F32)


def _dot_nt(a, b):
    return lax.dot_general(a, b, (((1,), (1,)), ((), ())), preferred_element_type=_F32)


def _dot_tn(a, b):
    return lax.dot_general(a, b, (((0,), (0,)), ((), ())), preferred_element_type=_F32)


def _params(sem, vmem=VMEM_LIMIT):
    return pltpu.CompilerParams(dimension_semantics=sem, vmem_limit_bytes=vmem)


def _const_spec(shape):
    nd = len(shape)
    return pl.BlockSpec(shape, lambda *_: (0,) * nd, pipeline_mode=pl.Buffered(1))


def _in_proj_body(h_ref, g_ref, w_ref, wt_ref, cost_ref, sint_ref, qg_ref, kg_ref, qx_ref, kx_ref,
                  qa_ref, qbt_ref, kat_ref, kbt_ref, va_ref, vbt_ref, ga_ref, gb_ref):
    h = h_ref[...]
    ms = jnp.mean(h * h, axis=-1, keepdims=True)
    xn = (h * lax.rsqrt(ms + EPS) * g_ref[...]).astype(_BF)

    qa = _dot(xn, w_ref[:, _C_QA:_C_QA + Q_COLS])
    qa_ref[...] = (qa * (SCALE * LOG2E)).astype(_BF)

    ones_col = (lax.broadcasted_iota(jnp.int32, (1, N_KV * V_LANES), 1) % V_LANES == HEAD_DIM).astype(_F32)
    va = (_dot(xn, w_ref[:, _C_VA:_C_VA + N_KV * V_LANES]) + ones_col).astype(_BF)
    for j in range(N_KV):
        va_ref[j] = va[:, j * V_LANES:(j + 1) * V_LANES]

    ga_ref[...] = jax.nn.sigmoid(_dot(xn, w_ref[:, _C_GA:_C_GA + D_MODEL])).astype(_BF)
    gb_ref[...] = jax.nn.sigmoid(_dot(xn, w_ref[:, _C_GB:_C_GB + D_MODEL])).astype(_BF)

    tt = _dot_nt(wt_ref[...], xn)
    cost = cost_ref[...]
    sint = sint_ref[...]
    f = ROPE_FREQS

    def norm_rope(x, gain):
        x = x * lax.rsqrt(jnp.mean(x * x, axis=0, keepdims=True) + EPS) * gain
        partner = jnp.concatenate([x[f:2 * f], x[0:f], x[3 * f:4 * f], x[2 * f:3 * f]], axis=0)
        return x * cost + partner * sint

    ones_row = (lax.broadcasted_iota(jnp.int32, (V_LANES, 1), 0) == HEAD_DIM).astype(_F32)
    tokens = tt.shape[1]
    q_extra = jnp.broadcast_to(qx_ref[...], (QK_ROWS - HEAD_DIM, tokens))
    k_extra = jnp.broadcast_to(kx_ref[...], (QK_ROWS - HEAD_DIM, tokens))
    for j in range(N_KV):
        kat_ref[j, 0] = tt[_R_KA + j * HEAD_DIM:_R_KA + (j + 1) * HEAD_DIM].astype(_BF)
        k = norm_rope(tt[_R_KB + j * HEAD_DIM:_R_KB + (j + 1) * HEAD_DIM], kg_ref[...])
        kbt_ref[j, 0] = jnp.concatenate([k, k_extra], axis=0).astype(_BF)
        vbt_ref[j, 0] = (tt[_R_VB + j * V_LANES:_R_VB + (j + 1) * V_LANES] + ones_row).astype(_BF)
    for hd in range(N_HEADS):
        q = norm_rope(tt[_R_QB + hd * HEAD_DIM:_R_QB + (hd + 1) * HEAD_DIM], qg_ref[...])
        qbt_ref[hd, 0] = jnp.concatenate([q * (SCALE * LOG2E), q_extra], axis=0).astype(_BF)


def _in_proj(h, g, w, wt, cost, sint, qg, kg, qx, kx):
    ntok = h.shape[0]
    nch = ntok // TOKEN_TILE
    t = TOKEN_TILE
    row = lambda i: (i, 0)
    chunk = lambda i: (0, i, 0, 0)
    out_shape = (
        jax.ShapeDtypeStruct((ntok, Q_COLS), _BF),
        jax.ShapeDtypeStruct((N_HEADS, nch, QK_ROWS, t), _BF),
        jax.ShapeDtypeStruct((N_KV, nch, HEAD_DIM, t), _BF),
        jax.ShapeDtypeStruct((N_KV, nch, QK_ROWS, t), _BF),
        jax.ShapeDtypeStruct((N_KV, ntok, V_LANES), _BF),
        jax.ShapeDtypeStruct((N_KV, nch, V_LANES, t), _BF),
        jax.ShapeDtypeStruct((ntok, D_MODEL), _BF),
        jax.ShapeDtypeStruct((ntok, D_MODEL), _BF),
    )
    kt_spec = pl.BlockSpec((N_KV, 1, HEAD_DIM, t), chunk)
    return pl.pallas_call(
        _in_proj_body,
        grid=(nch,),
        in_specs=[
            pl.BlockSpec((t, D_MODEL), row),
            _const_spec((1, D_MODEL)),
            _const_spec((D_MODEL, _C_END)),
            _const_spec((_R_END, D_MODEL)),
            pl.BlockSpec((HEAD_DIM, t), lambda i: (0, i)),
            pl.BlockSpec((HEAD_DIM, t), lambda i: (0, i)),
            _const_spec((HEAD_DIM, 1)),
            _const_spec((HEAD_DIM, 1)),
            _const_spec((QK_ROWS - HEAD_DIM, 1)),
            _const_spec((QK_ROWS - HEAD_DIM, 1)),
        ],
        out_specs=(
            pl.BlockSpec((t, Q_COLS), row),
            pl.BlockSpec((N_HEADS, 1, QK_ROWS, t), chunk),
            kt_spec, pl.BlockSpec((N_KV, 1, QK_ROWS, t), chunk),
            pl.BlockSpec((N_KV, t, V_LANES), lambda i: (0, i, 0)),
            pl.BlockSpec((N_KV, 1, V_LANES, t), chunk),
            pl.BlockSpec((t, D_MODEL), row), pl.BlockSpec((t, D_MODEL), row),
        ),
        out_shape=out_shape,
        compiler_params=_params(("parallel",)),
        name="in_proj",
    )(h, g, w, wt, cost, sint, qg, kg, qx, kx)


def _stack_heads(q, j):
    base = j * GROUP * HEAD_DIM
    return jnp.concatenate([q[:, base + g * HEAD_DIM: base + (g + 1) * HEAD_DIM] for g in range(GROUP)], axis=0)


def _sink_rows(sink_ref, j, rows, lanes):
    return jnp.concatenate([jnp.full((rows, lanes), sink_ref[j * GROUP + g], _F32) for g in range(GROUP)], axis=0)


def _window_body(geom, sink_ref, q_ref, kmain_ref, kprev_ref, knext_ref, vmain_ref, vprev_ref, vnext_ref,
                 kmeta_ref, vmeta_ref, bias_ref, _tail_ref, o_ref):
    t1, tpb1, tpb2 = geom
    t = pl.program_id(0)
    in_prompt = t < t1
    is_first = jnp.where(in_prompt, t % tpb1 == 0, (t - t1) % tpb2 == 0)
    is_last = jnp.where(in_prompt, t % tpb1 == tpb1 - 1, (t - t1) % tpb2 == tpb2 - 1)
    nblk = TOKEN_TILE // BLOCK
    lane_tiles = 4 * BLOCK // V_LANES
    q_all = q_ref[...]
    for j in range(N_KV):
        kcat = jnp.concatenate([kprev_ref[j, 0], kmain_ref[j, 0], knext_ref[j, 0]], axis=1)
        vcat = jnp.concatenate([vprev_ref[j], vmain_ref[j], vnext_ref[j]], axis=0)
        kmeta = kmeta_ref[0, j]
        vmeta = vmeta_ref[0, j]
        sink = _sink_rows(sink_ref, j, BLOCK, V_LANES)
        for r in range(nblk):
            variant = 0
            if r == 0:
                variant = jnp.where(is_first, 1, 0)
            if r == nblk - 1:
                variant = jnp.where(is_last, 2, variant)
            qs = _stack_heads(q_all[r * BLOCK:(r + 1) * BLOCK], j)
            kwin = jnp.concatenate([kcat[:, r * BLOCK:(r + 3) * BLOCK], kmeta], axis=1)
            vwin = jnp.concatenate([vcat[r * BLOCK:(r + 3) * BLOCK], vmeta], axis=0)
            s = _dot(qs, kwin) + bias_ref[variant, j]
            m = jnp.maximum(jnp.broadcast_to(s.max(axis=1, keepdims=True), sink.shape), sink)
            p = jnp.exp2(s - jnp.concatenate([m] * lane_tiles, axis=1)).astype(_BF)
            acc = _dot(p, vwin)
            l = jnp.broadcast_to(acc[:, HEAD_DIM:HEAD_DIM + 1], sink.shape) + jnp.exp2(sink - m)
            o = (acc / l).astype(_BF)
            for g in range(GROUP):
                c0 = (j * GROUP + g) * HEAD_DIM
                o_ref[r * BLOCK:(r + 1) * BLOCK, c0:c0 + HEAD_DIM] = o[g * BLOCK:(g + 1) * BLOCK, :HEAD_DIM]


def _window_attention(sink, qa, kat, va, kmeta, vmeta, bias, tail, layout):
    n_real, b1, n1, b2, n2 = layout
    t = TOKEN_TILE
    nch = kat.shape[1]
    t1 = b1 * n1 // t
    tpb1, tpb2 = n1 // t, n2 // t
    sub = t // BLOCK

    def bid(i):
        return jnp.where(i < t1, i // tpb1, b1 + (i - t1) // tpb2)

    return pl.pallas_call(
        functools.partial(_window_body, (t1, tpb1, tpb2)),
        grid=(n_real // t,),
        in_specs=[
            pl.BlockSpec(memory_space=pltpu.SMEM),
            pl.BlockSpec((t, Q_COLS), lambda i: (i, 0)),
            pl.BlockSpec((N_KV, 1, HEAD_DIM, t), lambda i: (0, i, 0, 0)),
            pl.BlockSpec((N_KV, 1, HEAD_DIM, BLOCK), lambda i: (0, jnp.maximum(i - 1, 0), 0, sub - 1)),
            pl.BlockSpec((N_KV, 1, HEAD_DIM, BLOCK), lambda i: (0, jnp.minimum(i + 1, nch - 1), 0, 0)),
            pl.BlockSpec((N_KV, t, V_LANES), lambda i: (0, i, 0)),
            pl.BlockSpec((N_KV, BLOCK, V_LANES), lambda i: (0, jnp.maximum(i * sub - 1, 0), 0)),
            pl.BlockSpec((N_KV, BLOCK, V_LANES), lambda i: (0, jnp.minimum(i + 1, nch - 1) * sub, 0)),
            pl.BlockSpec((1, N_KV, HEAD_DIM, BLOCK), lambda i: (bid(i), 0, 0, 0)),
            pl.BlockSpec((1, N_KV, BLOCK, V_LANES), lambda i: (bid(i), 0, 0, 0)),
            _const_spec((3, N_KV, GROUP * BLOCK, 4 * BLOCK)),
            pl.BlockSpec(memory_space=pl.ANY),
        ],
        out_specs=pl.BlockSpec((t, Q_COLS), lambda i: (i, 0)),
        out_shape=jax.ShapeDtypeStruct(qa.shape, _BF),
        input_output_aliases={11: 0},
        compiler_params=_params(("parallel",)),
        name="window_attn",
    )(sink, qa, kat, kat, kat, va, va, va, kmeta, vmeta, bias, tail)


def _window_meta_body(nb, sink_ref, q_ref, kfirst_ref, vfirst_ref, kmeta_ref, vmeta_ref, o_ref):
    b = pl.program_id(0)

    @pl.when(b < nb)
    def _():
        q_all = q_ref[...]
        for j in range(N_KV):
            qs = _stack_heads(q_all, j)
            sink = _sink_rows(sink_ref, j, N_META, 1)
            sm = _dot(qs, kmeta_ref[0, j])
            sf = _dot(qs, kfirst_ref[j, 0])
            m = jnp.maximum(jnp.maximum(sm.max(axis=1, keepdims=True), sf.max(axis=1, keepdims=True)), sink)
            pm = jnp.exp2(sm - m).astype(_BF)
            pf = jnp.exp2(sf - m).astype(_BF)
            acc = _dot(pm, vmeta_ref[0, j]) + _dot(pf, vfirst_ref[j])
            l = acc[:, HEAD_DIM:HEAD_DIM + 1] + jnp.exp2(sink - m)
            o = (acc[:, :HEAD_DIM] / l).astype(_BF)
            for g in range(GROUP):
                c0 = (j * GROUP + g) * HEAD_DIM
                o_ref[:, c0:c0 + HEAD_DIM] = o[g * N_META:(g + 1) * N_META]

    @pl.when(b >= nb)
    def _():
        o_ref[...] = jnp.zeros(o_ref.shape, o_ref.dtype)


def _window_meta_attention(sink, qa, kat, va, kmeta, vmeta, layout):
    n_real, b1, n1, b2, n2 = layout
    ntok = qa.shape[0]
    nb = b1 + b2
    t = TOKEN_TILE
    sub = t // BLOCK

    def start_chunk(b):
        bc = jnp.minimum(b, nb - 1)
        return jnp.where(bc < b1, bc * (n1 // t), b1 * (n1 // t) + (bc - b1) * (n2 // t))

    return pl.pallas_call(
        functools.partial(_window_meta_body, nb),
        grid=((ntok - n_real) // N_META,),
        in_specs=[
            pl.BlockSpec(memory_space=pltpu.SMEM),
            pl.BlockSpec((N_META, Q_COLS), lambda b: (n_real // N_META + b, 0)),
            pl.BlockSpec((N_KV, 1, HEAD_DIM, BLOCK), lambda b: (0, start_chunk(b), 0, 0)),
            pl.BlockSpec((N_KV, BLOCK, V_LANES), lambda b: (0, start_chunk(b) * sub, 0)),
            pl.BlockSpec((1, N_KV, HEAD_DIM, N_META), lambda b: (jnp.minimum(b, nb - 1), 0, 0, 0)),
            pl.BlockSpec((1, N_KV, N_META, V_LANES), lambda b: (jnp.minimum(b, nb - 1), 0, 0, 0)),
        ],
        out_specs=pl.BlockSpec((N_META, Q_COLS), lambda b: (n_real // N_META + b, 0)),
        out_shape=jax.ShapeDtypeStruct(qa.shape, _BF),
        compiler_params=_params(("arbitrary",)),
        name="window_meta_attn",
    )(sink, qa, kat, va, kmeta, vmeta)


_KEY_UNROLL = 4
_QUERY_BLOCK = 256
_SCORE_LOOKAHEAD = 3

def _global_body(nvalid, nchunks, tq, q_ref, kt_ref, vt_ref, kmeta_ref, vmetat_ref, _prev_ref, o_ref,
                 qt_ref, s0_ref, s1_ref, m_ref, acc_ref):
    b = pl.program_id(0)

    cols = qt_ref.shape[1]
    blocks = [slice(c0, min(c0 + _QUERY_BLOCK, cols)) for c0 in range(0, cols, _QUERY_BLOCK)]

    def scores(c, s_ref):
        for sl in blocks:
            s_ref[:, sl] = _dot_tn(kt_ref[0, c], qt_ref[:, sl])

    def softmax_pv(c, s_ref):
        for sl in blocks:
            s = s_ref[:, sl]
            m_prev = m_ref[:, sl]
            m_new = jnp.maximum(m_prev, jnp.max(s, axis=0, keepdims=True))
            p = jnp.exp2(s - m_new).astype(_BF)
            acc_ref[:, sl] = jnp.exp2(m_prev - m_new) * acc_ref[:, sl] + _dot(vt_ref[0, c], p)
            m_ref[:, sl] = m_new

    @pl.when(b < nvalid)
    def _():
        _stack_queries(q_ref, qt_ref)
        sm = _dot(kmeta_ref[0, 0], qt_ref[...])
        m0 = jnp.max(sm, axis=0, keepdims=True)
        m_ref[...] = m0
        acc_ref[...] = _dot(vmetat_ref[0, 0], jnp.exp2(sm - m0).astype(_BF))
        bufs = (s0_ref, s1_ref)
        scores(0, s0_ref)

        unroll = _KEY_UNROLL if nchunks > _KEY_UNROLL else 2

        def group(i, carry):
            c0 = unroll * i
            for u in range(unroll):
                scores(c0 + u + 1, bufs[(u + 1) % 2])
                softmax_pv(c0 + u, bufs[u % 2])
            return carry

        full = (nchunks - 1) // unroll
        lax.fori_loop(0, full, group, 0)
        for c in range(full * unroll, nchunks):
            if c + 1 < nchunks:
                scores(c + 1, bufs[(c + 1) % 2])
            softmax_pv(c, bufs[c % 2])
        _write_output(acc_ref, o_ref, tq)

    @pl.when(b >= nvalid)
    def _():
        o_ref[...] = jnp.zeros(o_ref.shape, o_ref.dtype)


def _global_fast_body(nvalid, nchunks, tq, q_ref, kt_ref, vt_ref, kmeta_ref, vmetat_ref, _prev_ref, o_ref,
                      qt_ref, acc_ref):
    b = pl.program_id(0)
    cols = qt_ref.shape[1]
    blocks = [slice(c0, min(c0 + _QUERY_BLOCK, cols)) for c0 in range(0, cols, _QUERY_BLOCK)]

    def chunks(cs):
        pending = []
        for c in cs:
            for sl in blocks:
                pending.append((c, sl, _dot_tn(kt_ref[0, c], qt_ref[:, sl])))
                if len(pending) > _SCORE_LOOKAHEAD:
                    values(*pending.pop(0))
        for item in pending:
            values(*item)

    def values(c, sl, s):
        acc_ref[:, sl] += _dot(vt_ref[0, c], jnp.exp2(s).astype(_BF))

    @pl.when(b < nvalid)
    def _():
        _stack_queries(q_ref, qt_ref)
        acc_ref[...] = _dot(vmetat_ref[0, 0], jnp.exp2(_dot(kmeta_ref[0, 0], qt_ref[...])).astype(_BF))

        def group(i, carry):
            chunks([_KEY_UNROLL * i + u for u in range(_KEY_UNROLL)])
            return carry

        full = nchunks // _KEY_UNROLL
        lax.fori_loop(0, full, group, 0)
        if full * _KEY_UNROLL < nchunks:
            chunks(range(full * _KEY_UNROLL, nchunks))
        _write_output(acc_ref, o_ref, tq)

    @pl.when(b >= nvalid)
    def _():
        o_ref[...] = jnp.zeros(o_ref.shape, o_ref.dtype)


def _stack_queries(q_ref, qt_ref):
    if q_ref.shape[0] == GROUP:
        qt_ref[...] = jnp.concatenate([q_ref[g, 0] for g in range(GROUP)], axis=1)
    else:
        qt_ref[...] = q_ref[0, 0]


def _write_output(acc_ref, o_ref, tq):
    acc = acc_ref[...]
    o = (acc / acc[HEAD_DIM:HEAD_DIM + 1]).T.astype(_BF)
    for g in range(GROUP):
        o_ref[:, g * HEAD_DIM:(g + 1) * HEAD_DIM] = o[g * tq:(g + 1) * tq, :HEAD_DIM]


def _global_attention(q, kbt, vbt, kmeta, vmetat, prev, out_rows, *, tq, q_row0, q_rows_per_batch, grid_batches,
                      valid_batches, kv_batch0, kv_row0, n, offset_is_bound=False):
    t = TOKEN_TILE
    nchunks = n // t
    qt = q_rows_per_batch // tq
    assert q_row0 % tq == 0 and kv_row0 % n == 0 and q_rows_per_batch % tq == 0
    kvb = lambda b: jnp.minimum(b, valid_batches - 1)
    if tq == N_META:
        cols = V_LANES
        q_spec = pl.BlockSpec((1, 1, QK_ROWS, cols), lambda b, j, i: (kv_batch0 + kvb(b), j, 0, 0))
    else:
        cols = GROUP * tq
        assert t % tq == 0 and q_row0 % t == 0 and q_rows_per_batch % t == 0
        q_spec = pl.BlockSpec(
            (GROUP, 1, QK_ROWS, tq),
            lambda b, j, i: (j, (q_row0 + b * q_rows_per_batch) // t + i // (t // tq), 0, i % (t // tq)))
    in_specs = [
        q_spec,
        pl.BlockSpec((1, nchunks, QK_ROWS, t), lambda b, j, i: (j, kv_row0 // n + kvb(b), 0, 0)),
        pl.BlockSpec((1, nchunks, V_LANES, t), lambda b, j, i: (j, kv_row0 // n + kvb(b), 0, 0)),
        pl.BlockSpec((1, 1, N_META, QK_ROWS), lambda b, j, i: (kv_batch0 + kvb(b), j, 0, 0)),
        pl.BlockSpec((1, 1, V_LANES, N_META), lambda b, j, i: (kv_batch0 + kvb(b), j, 0, 0)),
    ]
    args = [q, kbt, vbt, kmeta, vmetat]
    aliases = {}
    body = functools.partial(_global_fast_body if offset_is_bound else _global_body, valid_batches, nchunks, tq)
    score_bufs = [] if offset_is_bound else [pltpu.VMEM((t, cols), _F32), pltpu.VMEM((t, cols), _F32),
                                             pltpu.VMEM((1, cols), _F32)]
    if prev is None:
        body = functools.partial(_global_body_noprev, body)
    else:
        in_specs.append(pl.BlockSpec(memory_space=pl.ANY))
        args.append(prev)
        aliases = {5: 0}
    return pl.pallas_call(
        body,
        grid=(grid_batches, N_KV, qt),
        in_specs=in_specs,
        out_specs=pl.BlockSpec((tq, GROUP * HEAD_DIM), lambda b, j, i: (q_row0 // tq + b * qt + i, j)),
        out_shape=jax.ShapeDtypeStruct((out_rows, Q_COLS), _BF),
        scratch_shapes=[pltpu.VMEM((QK_ROWS, cols), _BF)] + score_bufs + [pltpu.VMEM((V_LANES, cols), _F32)],
        input_output_aliases=aliases,
        compiler_params=_params(("parallel", "parallel", "arbitrary")),
        name=f"global_attn{'_fast' if offset_is_bound else ''}_tq{tq}_n{n}",
    )(*args)


def _global_body_noprev(body, q_ref, kt_ref, vt_ref, kmeta_ref, vmetat_ref, o_ref, *scratch):
    body(q_ref, kt_ref, vt_ref, kmeta_ref, vmetat_ref, None, o_ref, *scratch)


def _mix_body(h_ref, oa_ref, ob_ref, ga_ref, gb_ref, wa_ref, wb_ref, wo_ref, g_ref, o_ref):
    mix = (ga_ref[...].astype(_F32) * _dot(oa_ref[...], wa_ref[...])
           + gb_ref[...].astype(_F32) * _dot(ob_ref[...], wb_ref[...]))
    u = _dot(mix.astype(_BF), wo_ref[...])
    ms = jnp.mean(u * u, axis=-1, keepdims=True)
    o_ref[...] = h_ref[...] + u * lax.rsqrt(ms + EPS) * g_ref[...]


def _mix(h, oa, ob, ga, gb, wa, wb, wo, g):
    ntok = h.shape[0]
    t = DENSE_TILE
    row = lambda i: (i, 0)
    return pl.pallas_call(
        _mix_body,
        grid=(ntok // t,),
        in_specs=[
            pl.BlockSpec((t, D_MODEL), row),
            pl.BlockSpec((t, Q_COLS), row), pl.BlockSpec((t, Q_COLS), row),
            pl.BlockSpec((t, D_MODEL), row), pl.BlockSpec((t, D_MODEL), row),
            _const_spec((Q_COLS, D_MODEL)), _const_spec((Q_COLS, D_MODEL)),
            _const_spec((D_MODEL, D_MODEL)), _const_spec((1, D_MODEL)),
        ],
        out_specs=pl.BlockSpec((t, D_MODEL), row),
        out_shape=jax.ShapeDtypeStruct(h.shape, _F32),
        input_output_aliases={0: 0},
        compiler_params=_params(("parallel",)),
        name="branch_mix",
    )(h, oa, ob, ga, gb, wa, wb, wo, g)


_FF_CHUNK = 256


def _ffn_body(h_ref, gpre_ref, wup_ref, wdown_ref, gpost_ref, o_ref, act_ref):
    h = h_ref[...]
    ms = jnp.mean(h * h, axis=-1, keepdims=True)
    xn = (h * lax.rsqrt(ms + EPS) * gpre_ref[...]).astype(_BF)
    for c in range(0, D_FF, _FF_CHUNK):
        a = _dot(xn, wup_ref[:, c:c + _FF_CHUNK])
        b = _dot(xn, wup_ref[:, D_FF + c:D_FF + c + _FF_CHUNK])
        act_ref[:, c:c + _FF_CHUNK] = (a * jax.nn.sigmoid(a) * b).astype(_BF)
    u = _dot(act_ref[...], wdown_ref[...])
    ms = jnp.mean(u * u, axis=-1, keepdims=True)
    o_ref[...] = h + u * lax.rsqrt(ms + EPS) * gpost_ref[...]


def _ffn(h, gpre, wup, wdown, gpost):
    ntok = h.shape[0]
    t = DENSE_TILE
    row = lambda i: (i, 0)
    return pl.pallas_call(
        _ffn_body,
        grid=(ntok // t,),
        in_specs=[
            pl.BlockSpec((t, D_MODEL), row),
            _const_spec((1, D_MODEL)),
            _const_spec((D_MODEL, 2 * D_FF)),
            _const_spec((D_FF, D_MODEL)),
            _const_spec((1, D_MODEL)),
        ],
        out_specs=pl.BlockSpec((t, D_MODEL), row),
        out_shape=jax.ShapeDtypeStruct(h.shape, _F32),
        scratch_shapes=[pltpu.VMEM((t, D_FF), _BF)],
        input_output_aliases={0: 0},
        compiler_params=_params(("parallel",)),
        name="swiglu_ffn",
    )(h, gpre, wup, wdown, gpost)


def _rope_tables(layout, ntok):
    n_real, b1, n1, b2, n2 = layout
    idx = np.zeros((ntok,), np.int64)
    idx[:b1 * n1] = np.arange(b1 * n1) % n1
    idx[b1 * n1:n_real] = np.arange(b2 * n2) % n2
    rows = (idx // GRID_W).astype(np.float32)
    cols = (idx % GRID_W).astype(np.float32)
    rows[n_real:] = 0.0
    cols[n_real:] = 0.0
    freqs = ROPE_BASE ** (-jnp.arange(ROPE_FREQS, dtype=_F32) / ROPE_FREQS)
    ang_r = jnp.asarray(rows)[:, None] * freqs[None, :]
    ang_c = jnp.asarray(cols)[:, None] * freqs[None, :]
    cr, sr, cc, sc = jnp.cos(ang_r), jnp.sin(ang_r), jnp.cos(ang_c), jnp.sin(ang_c)
    cos64 = jnp.concatenate([cr, cr, cc, cc], axis=1)
    sin64 = jnp.concatenate([-sr, sr, -sc, sc], axis=1)
    return cos64.T, sin64.T


def _window_bias():
    slopes = 2.0 ** (-8.0 * np.arange(1, N_HEADS + 1, dtype=np.float64) / N_HEADS)
    rel = np.arange(BLOCK)[:, None] - (np.arange(3 * BLOCK) - BLOCK)[None, :]
    dist = np.abs(rel)
    band = dist <= BLOCK
    local = np.where(band[None], -slopes[:, None, None] * dist[None].astype(np.float64) * LOG2E, NEG_INF)
    extra = np.full((N_HEADS, BLOCK, BLOCK), NEG_INF)
    extra[:, :, :N_META] = 0.0
    base = np.concatenate([local, extra], axis=2)
    no_prev = base.copy()
    no_prev[:, :, :BLOCK] = NEG_INF
    no_next = base.copy()
    no_next[:, :, 2 * BLOCK:3 * BLOCK] = NEG_INF
    out = np.stack([base, no_prev, no_next]).reshape(3, N_KV, GROUP * BLOCK, 4 * BLOCK)
    return jnp.asarray(out, _F32)


def _rearranged_w_in(w_in):
    o = np.cumsum([0, Q_COLS, KV_COLS, KV_COLS, Q_COLS, KV_COLS, KV_COLS, D_MODEL, D_MODEL])
    qa, ka, va, qb, kb, vb, ga, gb = [w_in[..., o[i]:o[i + 1]] for i in range(8)]

    def pad_v(v):
        z = jnp.zeros(v.shape[:-1] + (V_LANES - HEAD_DIM,), v.dtype)
        return jnp.concatenate([v[..., :HEAD_DIM], z, v[..., HEAD_DIM:], z], axis=-1)

    w_rows = jnp.concatenate([qa, pad_v(va), ga, gb], axis=-1).astype(_BF)
    w_t = jnp.swapaxes(jnp.concatenate([ka, kb, pad_v(vb), qb], axis=-1), -1, -2).astype(_BF)
    return w_rows, w_t


def _tail_features(xt, n_real, nb):
    tail = xt[:, n_real // TOKEN_TILE:]
    tail = jnp.moveaxis(tail, 2, 1).reshape(xt.shape[0], xt.shape[2], -1)[:, :, :nb * N_META]
    return tail.reshape(xt.shape[0], xt.shape[2], nb, N_META)


def _meta_kv(kt, v, n_real, nb):
    kmeta = jnp.transpose(_tail_features(kt, n_real, nb), (2, 0, 1, 3))
    vmeta = jnp.transpose(v[:, n_real:n_real + nb * N_META].reshape(N_KV, nb, N_META, V_LANES), (1, 0, 2, 3))
    return kmeta, vmeta


def kernel(x_prompt, x_sample, meta_tokens, g_mix_pre, g_mix_post, g_ffn_pre, g_ffn_post, w_in, q_norm_b,
           k_norm_b, sink_a, w_branch_a, w_branch_b, w_out, w_ffn_up, w_ffn_down):
    b1, n1, _ = x_prompt.shape
    b2, n2, _ = x_sample.shape
    depth = w_in.shape[0]
    t = TOKEN_TILE
    assert n1 % t == 0 and n2 % t == 0 and (b1 * n1) % n2 == 0 and n1 % GRID_W == 0 and n2 % GRID_W == 0
    nb = b1 + b2
    n_real = b1 * n1 + b2 * n2
    ntok = -(-(n_real + nb * N_META) // DENSE_TILE) * DENSE_TILE
    tail = ntok - n_real
    assert tail % t == 0
    layout = (n_real, b1, n1, b2, n2)

    h = jnp.concatenate([
        x_prompt.reshape(b1 * n1, D_MODEL), x_sample.reshape(b2 * n2, D_MODEL),
        jnp.tile(meta_tokens.astype(x_prompt.dtype), (nb, 1)),
        jnp.zeros((tail - nb * N_META, D_MODEL), x_prompt.dtype)], axis=0)

    cost, sint = _rope_tables(layout, ntok)
    bias = _window_bias()
    w_rows, w_t = _rearranged_w_in(w_in)
    wa, wb, wo = w_branch_a.astype(_BF), w_branch_b.astype(_BF), w_out.astype(_BF)
    wup, wdown = w_ffn_up.astype(_BF), w_ffn_down.astype(_BF)
    row = lambda g: g.reshape(1, -1).astype(_F32)
    col = lambda g: g.reshape(-1, 1).astype(_F32)

    tq_real = 512
    for l in range(depth):
        bound = (HEAD_DIM * SCALE * LOG2E * 1.02) * jnp.max(jnp.abs(q_norm_b[l])) * jnp.max(jnp.abs(k_norm_b[l]))
        unit = (jnp.arange(QK_ROWS - HEAD_DIM) == 0).astype(_F32).reshape(-1, 1)
        qa, qbt, kat, kbt, va, vbt, ga, gb = _in_proj(
            h, row(g_mix_pre[l]), w_rows[l], w_t[l], cost, sint, col(q_norm_b[l]), col(k_norm_b[l]),
            -bound.astype(_F32) * unit, unit)
        kameta, vameta = _meta_kv(kat, va, n_real, nb)
        sink = sink_a[l].astype(_F32) * LOG2E
        kameta_pad = jnp.pad(kameta, ((0, 0), (0, 0), (0, 0), (0, BLOCK - N_META)))
        vameta_pad = jnp.pad(vameta, ((0, 0), (0, 0), (0, BLOCK - N_META), (0, 0)))

        oa = _window_meta_attention(sink, qa, kat, va, kameta, vameta, layout)
        oa = _window_attention(sink, qa, kat, va, kameta_pad, vameta_pad, bias, oa, layout)

        kbmeta = jnp.transpose(_tail_features(kbt, n_real, nb), (2, 0, 3, 1))
        vbmetat = jnp.transpose(_tail_features(vbt, n_real, nb), (2, 0, 1, 3))
        qmeta = _tail_features(qbt, n_real, nb).reshape(N_KV, GROUP, QK_ROWS, nb, N_META)
        qmeta = jnp.transpose(qmeta, (3, 0, 2, 1, 4)).reshape(nb, N_KV, QK_ROWS, GROUP * N_META)
        qmeta = jnp.pad(qmeta, ((0, 0), (0, 0), (0, 0), (0, V_LANES - GROUP * N_META)))

        glob = functools.partial(_global_attention, kbt=kbt, vbt=vbt, kmeta=kbmeta, vmetat=vbmetat, out_rows=ntok)
        ob = glob(qmeta, prev=None, tq=N_META, q_row0=n_real, q_rows_per_batch=N_META, grid_batches=b1,
                  valid_batches=b1, kv_batch0=0, kv_row0=0, n=n1)
        ob = glob(qmeta, prev=ob, tq=N_META, q_row0=n_real + b1 * N_META, q_rows_per_batch=N_META,
                  grid_batches=tail // N_META - b1, valid_batches=b2, kv_batch0=b1, kv_row0=b1 * n1, n=n2)

        def real_queries(prev, offset_is_bound):
            out = glob(qbt, prev=prev, tq=tq_real, q_row0=0, q_rows_per_batch=n1, grid_batches=b1, valid_batches=b1,
                       kv_batch0=0, kv_row0=0, n=n1, offset_is_bound=offset_is_bound)
            return glob(qbt, prev=out, tq=tq_real, q_row0=b1 * n1, q_rows_per_batch=n2, grid_batches=b2,
                        valid_batches=b2, kv_batch0=b1, kv_row0=b1 * n1, n=n2, offset_is_bound=offset_is_bound)

        ob = lax.cond(bound <= SAFE_OFFSET_MAX, functools.partial(real_queries, offset_is_bound=True),
                      functools.partial(real_queries, offset_is_bound=False), ob)

        h = _mix(h, oa, ob, ga, gb, wa[l], wb[l], wo[l], row(g_mix_post[l]))
        h = _ffn(h, row(g_ffn_pre[l]), wup[l], wdown[l], row(g_ffn_post[l]))

    y_prompt = h[:b1 * n1].reshape(b1, n1, D_MODEL)
    y_sample = h[b1 * n1:n_real].reshape(b2, n2, D_MODEL)
    return (y_prompt, y_sample)
```

```python
import functools
import math

import jax
import jax.numpy as jnp
import numpy as np
from jax import lax
from jax.experimental import pallas as pl
from jax.experimental.pallas import tpu as pltpu

D_MODEL = 1024
HEAD_DIM = 64
N_HEADS = 8
N_KV = 2
GROUP = N_HEADS // N_KV
Q_COLS = N_HEADS * HEAD_DIM
KV_COLS = N_KV * HEAD_DIM
N_META = 16
BLOCK = 128
GRID_W = 64
ROPE_BASE = 10000.0
ROPE_FREQS = HEAD_DIM // 4
D_FF = 2816
EPS = 1e-6
NEG_INF = -1e30
SCALE = HEAD_DIM ** -0.5
LOG2E = math.log2(math.e)

QK_ROWS = HEAD_DIM + 16
SAFE_OFFSET_MAX = 40.0
V_LANES = 128
TOKEN_TILE = 512
DENSE_TILE = 1024
VMEM_LIMIT = 56 * 1024 * 1024

_C_GA = 0
_C_GB = _C_GA + D_MODEL
_C_END = _C_GB + D_MODEL
_R_KA = 0
_R_KB = _R_KA + KV_COLS
_R_VB = _R_KB + KV_COLS
_R_QB = _R_VB + N_KV * V_LANES
_R_VA = _R_QB + Q_COLS
_R_QA = _R_VA + N_KV * V_LANES
_R_END = _R_QA + Q_COLS

_BF = jnp.bfloat16
_F32 = jnp.float32


def _dot(a, b):
    return jnp.dot(a, b, preferred_element_type=_F32)


def _dot_nt(a, b):
    return lax.dot_general(a, b, (((1,), (1,)), ((), ())), preferred_element_type=_F32)


def _dot_tn(a, b):
    return lax.dot_general(a, b, (((0,), (0,)), ((), ())), preferred_element_type=_F32)


def _params(sem, vmem=VMEM_LIMIT):
    return pltpu.CompilerParams(dimension_semantics=sem, vmem_limit_bytes=vmem)


def _const_spec(shape):
    nd = len(shape)
    return pl.BlockSpec(shape, lambda *_: (0,) * nd, pipeline_mode=pl.Buffered(1))


def _in_proj_body(h_ref, g_ref, w_ref, wt_ref, cost_ref, sint_ref, qg_ref, kg_ref, qx_ref, kx_ref,
                  qat_ref, qbt_ref, kat_ref, kbt_ref, vat_ref, vbt_ref, ga_ref, gb_ref):
    h = h_ref[...]
    ms = jnp.mean(h * h, axis=-1, keepdims=True)
    xn = (h * lax.rsqrt(ms + EPS) * g_ref[...]).astype(_BF)

    ga_ref[...] = jax.nn.sigmoid(_dot(xn, w_ref[:, _C_GA:_C_GA + D_MODEL])).astype(_BF)
    gb_ref[...] = jax.nn.sigmoid(_dot(xn, w_ref[:, _C_GB:_C_GB + D_MODEL])).astype(_BF)

    tt = _dot_nt(wt_ref[...], xn)
    cost = cost_ref[...]
    sint = sint_ref[...]
    f = ROPE_FREQS

    def norm_rope(x, gain):
        x = x * lax.rsqrt(jnp.mean(x * x, axis=0, keepdims=True) + EPS) * gain
        partner = jnp.concatenate([x[f:2 * f], x[0:f], x[3 * f:4 * f], x[2 * f:3 * f]], axis=0)
        return x * cost + partner * sint

    def put(ref, head, x):
        x = x.astype(_BF)
        for cc in range(x.shape[1] // TOKEN_TILE):
            ref[head, cc] = x[:, cc * TOKEN_TILE:(cc + 1) * TOKEN_TILE]

    ones_row = (lax.broadcasted_iota(jnp.int32, (V_LANES, 1), 0) == HEAD_DIM).astype(_F32)
    tokens = tt.shape[1]
    q_extra = jnp.broadcast_to(qx_ref[...], (QK_ROWS - HEAD_DIM, tokens))
    k_extra = jnp.broadcast_to(kx_ref[...], (QK_ROWS - HEAD_DIM, tokens))
    for j in range(N_KV):
        put(kat_ref, j, tt[_R_KA + j * HEAD_DIM:_R_KA + (j + 1) * HEAD_DIM])
        k = norm_rope(tt[_R_KB + j * HEAD_DIM:_R_KB + (j + 1) * HEAD_DIM], kg_ref[...])
        put(kbt_ref, j, jnp.concatenate([k, k_extra], axis=0))
        put(vat_ref, j, tt[_R_VA + j * V_LANES:_R_VA + (j + 1) * V_LANES] + ones_row)
        put(vbt_ref, j, tt[_R_VB + j * V_LANES:_R_VB + (j + 1) * V_LANES] + ones_row)
    for hd in range(N_HEADS):
        put(qat_ref, hd, tt[_R_QA + hd * HEAD_DIM:_R_QA + (hd + 1) * HEAD_DIM] * (SCALE * LOG2E))
        q = norm_rope(tt[_R_QB + hd * HEAD_DIM:_R_QB + (hd + 1) * HEAD_DIM], qg_ref[...])
        put(qbt_ref, hd, jnp.concatenate([q * (SCALE * LOG2E), q_extra], axis=0))


def _in_proj(h, g, w, wt, cost, sint, qg, kg, qx, kx):
    ntok = h.shape[0]
    t = DENSE_TILE
    per = t // TOKEN_TILE
    nch = ntok // TOKEN_TILE
    row = lambda i: (i, 0)
    chunk = lambda i: (0, i, 0, 0)
    feat = lambda heads, rows: (jax.ShapeDtypeStruct((heads, nch, rows, TOKEN_TILE), _BF),
                                pl.BlockSpec((heads, per, rows, TOKEN_TILE), chunk))
    outs = [
        feat(N_HEADS, HEAD_DIM),
        feat(N_HEADS, QK_ROWS),
        feat(N_KV, HEAD_DIM),
        feat(N_KV, QK_ROWS),
        feat(N_KV, V_LANES),
        feat(N_KV, V_LANES),
        (jax.ShapeDtypeStruct((ntok, D_MODEL), _BF), pl.BlockSpec((t, D_MODEL), row)),
        (jax.ShapeDtypeStruct((ntok, D_MODEL), _BF), pl.BlockSpec((t, D_MODEL), row)),
    ]
    return pl.pallas_call(
        _in_proj_body,
        grid=(ntok // t,),
        in_specs=[
            pl.BlockSpec((t, D_MODEL), row),
            _const_spec((1, D_MODEL)),
            _const_spec((D_MODEL, _C_END)),
            _const_spec((_R_END, D_MODEL)),
            pl.BlockSpec((HEAD_DIM, t), lambda i: (0, i)),
            pl.BlockSpec((HEAD_DIM, t), lambda i: (0, i)),
            _const_spec((HEAD_DIM, 1)),
            _const_spec((HEAD_DIM, 1)),
            _const_spec((QK_ROWS - HEAD_DIM, 1)),
            _const_spec((QK_ROWS - HEAD_DIM, 1)),
        ],
        out_specs=tuple(o[1] for o in outs),
        out_shape=tuple(o[0] for o in outs),
        compiler_params=_params(("parallel",)),
        name="in_proj",
    )(h, g, w, wt, cost, sint, qg, kg, qx, kx)


_WINDOW_LOOKAHEAD = 3


def _stack_heads(q, j):
    base = j * GROUP * HEAD_DIM
    return jnp.concatenate([q[:, base + g * HEAD_DIM: base + (g + 1) * HEAD_DIM] for g in range(GROUP)], axis=0)


def _sink_rows(sink_ref, j, rows, lanes):
    return jnp.concatenate([jnp.full((rows, lanes), sink_ref[j * GROUP + g], _F32) for g in range(GROUP)], axis=0)


def _window_body(geom, sink_ref, q_ref, kmain_ref, kprev_ref, knext_ref, vmain_ref, vprev_ref, vnext_ref,
                 kmeta_ref, vmeta_ref, bias_ref, _tail_ref, o_ref):
    t1, tpb1, tpb2 = geom
    t = pl.program_id(0)
    in_prompt = t < t1
    is_first = jnp.where(in_prompt, t % tpb1 == 0, (t - t1) % tpb2 == 0)
    is_last = jnp.where(in_prompt, t % tpb1 == tpb1 - 1, (t - t1) % tpb2 == tpb2 - 1)
    nblk = TOKEN_TILE // BLOCK

    def softmax_values(j, r, s, vwin, sink):
        m = jnp.maximum(jnp.max(s, axis=0, keepdims=True), sink)
        acc = _dot(vwin, jnp.exp2(s - m).astype(_BF))
        l = acc[HEAD_DIM:HEAD_DIM + 1] + jnp.exp2(sink - m)
        o = (acc / l).T.astype(_BF)
        for g in range(GROUP):
            c0 = (j * GROUP + g) * HEAD_DIM
            o_ref[r * BLOCK:(r + 1) * BLOCK, c0:c0 + HEAD_DIM] = o[g * BLOCK:(g + 1) * BLOCK, :HEAD_DIM]

    pending = []
    for j in range(N_KV):
        kcat = jnp.concatenate([kprev_ref[j, 0], kmain_ref[j, 0], knext_ref[j, 0]], axis=1)
        vcat = jnp.concatenate([vprev_ref[j, 0], vmain_ref[j, 0], vnext_ref[j, 0]], axis=1)
        kmeta = kmeta_ref[0, j]
        vmeta = vmeta_ref[0, j]
        sink = jnp.concatenate([jnp.full((1, BLOCK), sink_ref[j * GROUP + g], _F32) for g in range(GROUP)], axis=1)
        for r in range(nblk):
            variant = 0
            if r == 0:
                variant = jnp.where(is_first, 1, 0)
            if r == nblk - 1:
                variant = jnp.where(is_last, 2, variant)
            qt = jnp.concatenate([q_ref[j * GROUP + g, 0][:, r * BLOCK:(r + 1) * BLOCK] for g in range(GROUP)],
                                 axis=1)
            kwin = jnp.concatenate([kcat[:, r * BLOCK:(r + 3) * BLOCK], kmeta], axis=1)
            vwin = jnp.concatenate([vcat[:, r * BLOCK:(r + 3) * BLOCK], vmeta], axis=1)
            s = _dot_tn(kwin, qt) + bias_ref[variant, j]
            pending.append((j, r, s, vwin, sink))
            if len(pending) > _WINDOW_LOOKAHEAD:
                softmax_values(*pending.pop(0))
    for item in pending:
        softmax_values(*item)


def _window_attention(sink, qat, kat, vat, kmeta, vmetat, bias, tail, layout):
    n_real, b1, n1, b2, n2 = layout
    t = TOKEN_TILE
    nch = kat.shape[1]
    t1 = b1 * n1 // t
    tpb1, tpb2 = n1 // t, n2 // t
    sub = t // BLOCK

    def bid(i):
        return jnp.where(i < t1, i // tpb1, b1 + (i - t1) // tpb2)

    main = lambda i: (0, i, 0, 0)
    prev = lambda i: (0, jnp.maximum(i - 1, 0), 0, sub - 1)
    nxt = lambda i: (0, jnp.minimum(i + 1, nch - 1), 0, 0)
    return pl.pallas_call(
        functools.partial(_window_body, (t1, tpb1, tpb2)),
        grid=(n_real // t,),
        in_specs=[
            pl.BlockSpec(memory_space=pltpu.SMEM),
            pl.BlockSpec((N_HEADS, 1, HEAD_DIM, t), main),
            pl.BlockSpec((N_KV, 1, HEAD_DIM, t), main),
            pl.BlockSpec((N_KV, 1, HEAD_DIM, BLOCK), prev),
            pl.BlockSpec((N_KV, 1, HEAD_DIM, BLOCK), nxt),
            pl.BlockSpec((N_KV, 1, V_LANES, t), main),
            pl.BlockSpec((N_KV, 1, V_LANES, BLOCK), prev),
            pl.BlockSpec((N_KV, 1, V_LANES, BLOCK), nxt),
            pl.BlockSpec((1, N_KV, HEAD_DIM, BLOCK), lambda i: (bid(i), 0, 0, 0)),
            pl.BlockSpec((1, N_KV, V_LANES, BLOCK), lambda i: (bid(i), 0, 0, 0)),
            _const_spec((3, N_KV, 4 * BLOCK, GROUP * BLOCK)),
            pl.BlockSpec(memory_space=pl.ANY),
        ],
        out_specs=pl.BlockSpec((t, Q_COLS), lambda i: (i, 0)),
        out_shape=jax.ShapeDtypeStruct(tail.shape, _BF),
        input_output_aliases={11: 0},
        compiler_params=_params(("parallel",)),
        name="window_attn",
    )(sink, qat, kat, kat, kat, vat, vat, vat, kmeta, vmetat, bias, tail)


def _window_meta_body(nb, sink_ref, q_ref, kfirst_ref, vfirst_ref, kmeta_ref, vmeta_ref, o_ref):
    b = pl.program_id(0)

    @pl.when(b < nb)
    def _():
        q_all = q_ref[...]
        for j in range(N_KV):
            qs = _stack_heads(q_all, j)
            sink = _sink_rows(sink_ref, j, N_META, 1)
            sm = _dot(qs, kmeta_ref[0, j])
            sf = _dot(qs, kfirst_ref[j, 0])
            m = jnp.maximum(jnp.maximum(sm.max(axis=1, keepdims=True), sf.max(axis=1, keepdims=True)), sink)
            pm = jnp.exp2(sm - m).astype(_BF)
            pf = jnp.exp2(sf - m).astype(_BF)
            acc = _dot(pm, vmeta_ref[0, j]) + _dot_nt(pf, vfirst_ref[j, 0])
            l = acc[:, HEAD_DIM:HEAD_DIM + 1] + jnp.exp2(sink - m)
            o = (acc[:, :HEAD_DIM] / l).astype(_BF)
            for g in range(GROUP):
                c0 = (j * GROUP + g) * HEAD_DIM
                o_ref[:, c0:c0 + HEAD_DIM] = o[g * N_META:(g + 1) * N_META]

    @pl.when(b >= nb)
    def _():
        o_ref[...] = jnp.zeros(o_ref.shape, o_ref.dtype)


def _window_meta_attention(sink, q_rows, kat, vat, kmeta, vmeta, layout, out_rows):
    n_real, b1, n1, b2, n2 = layout
    nb = b1 + b2
    t = TOKEN_TILE

    def start_chunk(b):
        bc = jnp.minimum(b, nb - 1)
        return jnp.where(bc < b1, bc * (n1 // t), b1 * (n1 // t) + (bc - b1) * (n2 // t))

    return pl.pallas_call(
        functools.partial(_window_meta_body, nb),
        grid=((out_rows - n_real) // N_META,),
        in_specs=[
            pl.BlockSpec(memory_space=pltpu.SMEM),
            pl.BlockSpec((N_META, Q_COLS), lambda b: (b, 0)),
            pl.BlockSpec((N_KV, 1, HEAD_DIM, BLOCK), lambda b: (0, start_chunk(b), 0, 0)),
            pl.BlockSpec((N_KV, 1, V_LANES, BLOCK), lambda b: (0, start_chunk(b), 0, 0)),
            pl.BlockSpec((1, N_KV, HEAD_DIM, N_META), lambda b: (jnp.minimum(b, nb - 1), 0, 0, 0)),
            pl.BlockSpec((1, N_KV, N_META, V_LANES), lambda b: (jnp.minimum(b, nb - 1), 0, 0, 0)),
        ],
        out_specs=pl.BlockSpec((N_META, Q_COLS), lambda b: (n_real // N_META + b, 0)),
        out_shape=jax.ShapeDtypeStruct((out_rows, Q_COLS), _BF),
        compiler_params=_params(("arbitrary",)),
        name="window_meta_attn",
    )(sink, q_rows, kat, vat, kmeta, vmeta)


_KEY_UNROLL = 4
_QUERY_BLOCK = 256
_SCORE_LOOKAHEAD = 3


def _global_body(nvalid, nchunks, tq, q_ref, kt_ref, vt_ref, kmeta_ref, vmetat_ref, _prev_ref, o_ref,
                 qt_ref, s0_ref, s1_ref, m_ref, acc_ref):
    b = pl.program_id(0)

    cols = qt_ref.shape[1]
    blocks = [slice(c0, min(c0 + _QUERY_BLOCK, cols)) for c0 in range(0, cols, _QUERY_BLOCK)]

    def scores(c, s_ref):
        for sl in blocks:
            s_ref[:, sl] = _dot_tn(kt_ref[0, c], qt_ref[:, sl])

    def softmax_pv(c, s_ref):
        for sl in blocks:
            s = s_ref[:, sl]
            m_prev = m_ref[:, sl]
            m_new = jnp.maximum(m_prev, jnp.max(s, axis=0, keepdims=True))
            p = jnp.exp2(s - m_new).astype(_BF)
            acc_ref[:, sl] = jnp.exp2(m_prev - m_new) * acc_ref[:, sl] + _dot(vt_ref[0, c], p)
            m_ref[:, sl] = m_new

    @pl.when(b < nvalid)
    def _():
        _stack_queries(q_ref, qt_ref)
        sm = _dot(kmeta_ref[0, 0], qt_ref[...])
        m0 = jnp.max(sm, axis=0, keepdims=True)
        m_ref[...] = m0
        acc_ref[...] = _dot(vmetat_ref[0, 0], jnp.exp2(sm - m0).astype(_BF))
        bufs = (s0_ref, s1_ref)
        scores(0, s0_ref)

        unroll = _KEY_UNROLL if nchunks > _KEY_UNROLL else 2

        def group(i, carry):
            c0 = unroll * i
            for u in range(unroll):
                scores(c0 + u + 1, bufs[(u + 1) % 2])
                softmax_pv(c0 + u, bufs[u % 2])
            return carry

        full = (nchunks - 1) // unroll
        lax.fori_loop(0, full, group, 0)
        for c in range(full * unroll, nchunks):
            if c + 1 < nchunks:
                scores(c + 1, bufs[(c + 1) % 2])
            softmax_pv(c, bufs[c % 2])
        _write_output(acc_ref, o_ref, tq)

    @pl.when(b >= nvalid)
    def _():
        o_ref[...] = jnp.zeros(o_ref.shape, o_ref.dtype)


def _global_fast_body(nvalid, nchunks, tq, q_ref, kt_ref, vt_ref, kmeta_ref, vmetat_ref, _prev_ref, o_ref,
                      qt_ref, acc_ref):
    b = pl.program_id(0)
    cols = qt_ref.shape[1]
    blocks = [slice(c0, min(c0 + _QUERY_BLOCK, cols)) for c0 in range(0, cols, _QUERY_BLOCK)]

    def chunks(cs):
        pending = []
        for c in cs:
            for sl in blocks:
                pending.append((c, sl, _dot_tn(kt_ref[0, c], qt_ref[:, sl])))
                if len(pending) > _SCORE_LOOKAHEAD:
                    values(*pending.pop(0))
        for item in pending:
            values(*item)

    def values(c, sl, s):
        acc_ref[:, sl] += _dot(vt_ref[0, c], jnp.exp2(s).astype(_BF))

    @pl.when(b < nvalid)
    def _():
        _stack_queries(q_ref, qt_ref)
        acc_ref[...] = _dot(vmetat_ref[0, 0], jnp.exp2(_dot(kmeta_ref[0, 0], qt_ref[...])).astype(_BF))

        def group(i, carry):
            chunks([_KEY_UNROLL * i + u for u in range(_KEY_UNROLL)])
            return carry

        full = nchunks // _KEY_UNROLL
        lax.fori_loop(0, full, group, 0)
        if full * _KEY_UNROLL < nchunks:
            chunks(range(full * _KEY_UNROLL, nchunks))
        _write_output(acc_ref, o_ref, tq)

    @pl.when(b >= nvalid)
    def _():
        o_ref[...] = jnp.zeros(o_ref.shape, o_ref.dtype)


def _stack_queries(q_ref, qt_ref):
    if q_ref.shape[0] == GROUP:
        qt_ref[...] = jnp.concatenate([q_ref[g, 0] for g in range(GROUP)], axis=1)
    else:
        qt_ref[...] = q_ref[0, 0]


def _write_output(acc_ref, o_ref, tq):
    acc = acc_ref[...]
    o = (acc / acc[HEAD_DIM:HEAD_DIM + 1]).T.astype(_BF)
    for g in range(GROUP):
        o_ref[:, g * HEAD_DIM:(g + 1) * HEAD_DIM] = o[g * tq:(g + 1) * tq, :HEAD_DIM]


def _global_attention(q, kbt, vbt, kmeta, vmetat, prev, out_rows, *, tq, q_row0, q_rows_per_batch, grid_batches,
                      valid_batches, kv_batch0, kv_row0, n, offset_is_bound=False):
    t = TOKEN_TILE
    nchunks = n // t
    qt = q_rows_per_batch // tq
    assert q_row0 % tq == 0 and kv_row0 % n == 0 and q_rows_per_batch % tq == 0
    kvb = lambda b: jnp.minimum(b, valid_batches - 1)
    if tq == N_META:
        cols = V_LANES
        q_spec = pl.BlockSpec((1, 1, QK_ROWS, cols), lambda b, j, i: (kv_batch0 + kvb(b), j, 0, 0))
    else:
        cols = GROUP * tq
        assert t % tq == 0 and q_row0 % t == 0 and q_rows_per_batch % t == 0
        q_spec = pl.BlockSpec(
            (GROUP, 1, QK_ROWS, tq),
            lambda b, j, i: (j, (q_row0 + b * q_rows_per_batch) // t + i // (t // tq), 0, i % (t // tq)))
    in_specs = [
        q_spec,
        pl.BlockSpec((1, nchunks, QK_ROWS, t), lambda b, j, i: (j, kv_row0 // n + kvb(b), 0, 0)),
        pl.BlockSpec((1, nchunks, V_LANES, t), lambda b, j, i: (j, kv_row0 // n + kvb(b), 0, 0)),
        pl.BlockSpec((1, 1, N_META, QK_ROWS), lambda b, j, i: (kv_batch0 + kvb(b), j, 0, 0)),
        pl.BlockSpec((1, 1, V_LANES, N_META), lambda b, j, i: (kv_batch0 + kvb(b), j, 0, 0)),
    ]
    args = [q, kbt, vbt, kmeta, vmetat]
    aliases = {}
    body = functools.partial(_global_fast_body if offset_is_bound else _global_body, valid_batches, nchunks, tq)
    score_bufs = [] if offset_is_bound else [pltpu.VMEM((t, cols), _F32), pltpu.VMEM((t, cols), _F32),
                                             pltpu.VMEM((1, cols), _F32)]
    if prev is None:
        body = functools.partial(_global_body_noprev, body)
    else:
        in_specs.append(pl.BlockSpec(memory_space=pl.ANY))
        args.append(prev)
        aliases = {5: 0}
    return pl.pallas_call(
        body,
        grid=(grid_batches, N_KV, qt),
        in_specs=in_specs,
        out_specs=pl.BlockSpec((tq, GROUP * HEAD_DIM), lambda b, j, i: (q_row0 // tq + b * qt + i, j)),
        out_shape=jax.ShapeDtypeStruct((out_rows, Q_COLS), _BF),
        scratch_shapes=[pltpu.VMEM((QK_ROWS, cols), _BF)] + score_bufs + [pltpu.VMEM((V_LANES, cols), _F32)],
        input_output_aliases=aliases,
        compiler_params=_params(("parallel", "parallel", "arbitrary")),
        name=f"global_attn{'_fast' if offset_is_bound else ''}_tq{tq}_n{n}",
    )(*args)


def _global_body_noprev(body, q_ref, kt_ref, vt_ref, kmeta_ref, vmetat_ref, o_ref, *scratch):
    body(q_ref, kt_ref, vt_ref, kmeta_ref, vmetat_ref, None, o_ref, *scratch)


def _mix_body(h_ref, oa_ref, ob_ref, ga_ref, gb_ref, wa_ref, wb_ref, wo_ref, g_ref, o_ref):
    mix = (ga_ref[...].astype(_F32) * _dot(oa_ref[...], wa_ref[...])
           + gb_ref[...].astype(_F32) * _dot(ob_ref[...], wb_ref[...]))
    u = _dot(mix.astype(_BF), wo_ref[...])
    ms = jnp.mean(u * u, axis=-1, keepdims=True)
    o_ref[...] = h_ref[...] + u * lax.rsqrt(ms + EPS) * g_ref[...]


def _mix(h, oa, ob, ga, gb, wa, wb, wo, g):
    ntok = h.shape[0]
    t = DENSE_TILE
    row = lambda i: (i, 0)
    return pl.pallas_call(
        _mix_body,
        grid=(ntok // t,),
        in_specs=[
            pl.BlockSpec((t, D_MODEL), row),
            pl.BlockSpec((t, Q_COLS), row), pl.BlockSpec((t, Q_COLS), row),
            pl.BlockSpec((t, D_MODEL), row), pl.BlockSpec((t, D_MODEL), row),
            _const_spec((Q_COLS, D_MODEL)), _const_spec((Q_COLS, D_MODEL)),
            _const_spec((D_MODEL, D_MODEL)), _const_spec((1, D_MODEL)),
        ],
        out_specs=pl.BlockSpec((t, D_MODEL), row),
        out_shape=jax.ShapeDtypeStruct(h.shape, _F32),
        input_output_aliases={0: 0},
        compiler_params=_params(("parallel",)),
        name="branch_mix",
    )(h, oa, ob, ga, gb, wa, wb, wo, g)


_FF_CHUNK = 256


def _ffn_body(h_ref, gpre_ref, wup_ref, wdown_ref, gpost_ref, o_ref, act_ref):
    h = h_ref[...]
    ms = jnp.mean(h * h, axis=-1, keepdims=True)
    xn = (h * lax.rsqrt(ms + EPS) * gpre_ref[...]).astype(_BF)
    for c in range(0, D_FF, _FF_CHUNK):
        a = _dot(xn, wup_ref[:, c:c + _FF_CHUNK])
        b = _dot(xn, wup_ref[:, D_FF + c:D_FF + c + _FF_CHUNK])
        act_ref[:, c:c + _FF_CHUNK] = (a * jax.nn.sigmoid(a) * b).astype(_BF)
    u = _dot(act_ref[...], wdown_ref[...])
    ms = jnp.mean(u * u, axis=-1, keepdims=True)
    o_ref[...] = h + u * lax.rsqrt(ms + EPS) * gpost_ref[...]


def _ffn(h, gpre, wup, wdown, gpost):
    ntok = h.shape[0]
    t = DENSE_TILE
    row = lambda i: (i, 0)
    return pl.pallas_call(
        _ffn_body,
        grid=(ntok // t,),
        in_specs=[
            pl.BlockSpec((t, D_MODEL), row),
            _const_spec((1, D_MODEL)),
            _const_spec((D_MODEL, 2 * D_FF)),
            _const_spec((D_FF, D_MODEL)),
            _const_spec((1, D_MODEL)),
        ],
        out_specs=pl.BlockSpec((t, D_MODEL), row),
        out_shape=jax.ShapeDtypeStruct(h.shape, _F32),
        scratch_shapes=[pltpu.VMEM((t, D_FF), _BF)],
        input_output_aliases={0: 0},
        compiler_params=_params(("parallel",)),
        name="swiglu_ffn",
    )(h, gpre, wup, wdown, gpost)


def _rope_tables(layout, ntok):
    n_real, b1, n1, b2, n2 = layout
    idx = np.zeros((ntok,), np.int64)
    idx[:b1 * n1] = np.arange(b1 * n1) % n1
    idx[b1 * n1:n_real] = np.arange(b2 * n2) % n2
    rows = (idx // GRID_W).astype(np.float32)
    cols = (idx % GRID_W).astype(np.float32)
    rows[n_real:] = 0.0
    cols[n_real:] = 0.0
    freqs = ROPE_BASE ** (-jnp.arange(ROPE_FREQS, dtype=_F32) / ROPE_FREQS)
    ang_r = jnp.asarray(rows)[:, None] * freqs[None, :]
    ang_c = jnp.asarray(cols)[:, None] * freqs[None, :]
    cr, sr, cc, sc = jnp.cos(ang_r), jnp.sin(ang_r), jnp.cos(ang_c), jnp.sin(ang_c)
    cos64 = jnp.concatenate([cr, cr, cc, cc], axis=1)
    sin64 = jnp.concatenate([-sr, sr, -sc, sc], axis=1)
    return cos64.T, sin64.T


def _window_bias():
    slopes = 2.0 ** (-8.0 * np.arange(1, N_HEADS + 1, dtype=np.float64) / N_HEADS)
    rel = np.arange(BLOCK)[:, None] - (np.arange(3 * BLOCK) - BLOCK)[None, :]
    dist = np.abs(rel)
    band = dist <= BLOCK
    local = np.where(band[None], -slopes[:, None, None] * dist[None].astype(np.float64) * LOG2E, NEG_INF)
    extra = np.full((N_HEADS, BLOCK, BLOCK), NEG_INF)
    extra[:, :, :N_META] = 0.0
    base = np.concatenate([local, extra], axis=2)
    no_prev = base.copy()
    no_prev[:, :, :BLOCK] = NEG_INF
    no_next = base.copy()
    no_next[:, :, 2 * BLOCK:3 * BLOCK] = NEG_INF
    out = np.stack([base, no_prev, no_next]).reshape(3, N_KV, GROUP * BLOCK, 4 * BLOCK)
    return jnp.asarray(np.swapaxes(out, -1, -2), _F32)


def _rearranged_w_in(w_in):
    o = np.cumsum([0, Q_COLS, KV_COLS, KV_COLS, Q_COLS, KV_COLS, KV_COLS, D_MODEL, D_MODEL])
    qa, ka, va, qb, kb, vb, ga, gb = [w_in[..., o[i]:o[i + 1]] for i in range(8)]

    def pad_v(v):
        z = jnp.zeros(v.shape[:-1] + (V_LANES - HEAD_DIM,), v.dtype)
        return jnp.concatenate([v[..., :HEAD_DIM], z, v[..., HEAD_DIM:], z], axis=-1)

    w_rows = jnp.concatenate([ga, gb], axis=-1).astype(_BF)
    w_t = jnp.swapaxes(jnp.concatenate([ka, kb, pad_v(vb), qb, pad_v(va), qa], axis=-1), -1, -2).astype(_BF)
    return w_rows, w_t


def _tail_features(xt, n_real, nb):
    tail = xt[:, n_real // TOKEN_TILE:]
    tail = jnp.moveaxis(tail, 2, 1).reshape(xt.shape[0], xt.shape[2], -1)[:, :, :nb * N_META]
    return tail.reshape(xt.shape[0], xt.shape[2], nb, N_META)


def kernel(x_prompt, x_sample, meta_tokens, g_mix_pre, g_mix_post, g_ffn_pre, g_ffn_post, w_in, q_norm_b,
           k_norm_b, sink_a, w_branch_a, w_branch_b, w_out, w_ffn_up, w_ffn_down):
    b1, n1, _ = x_prompt.shape
    b2, n2, _ = x_sample.shape
    depth = w_in.shape[0]
    t = TOKEN_TILE
    assert n1 % t == 0 and n2 % t == 0 and (b1 * n1) % n2 == 0 and n1 % GRID_W == 0 and n2 % GRID_W == 0
    nb = b1 + b2
    n_real = b1 * n1 + b2 * n2
    ntok = -(-(n_real + nb * N_META) // DENSE_TILE) * DENSE_TILE
    tail = ntok - n_real
    assert tail % t == 0
    layout = (n_real, b1, n1, b2, n2)

    h = jnp.concatenate([
        x_prompt.reshape(b1 * n1, D_MODEL), x_sample.reshape(b2 * n2, D_MODEL),
        jnp.tile(meta_tokens.astype(x_prompt.dtype), (nb, 1)),
        jnp.zeros((tail - nb * N_META, D_MODEL), x_prompt.dtype)], axis=0)

    cost, sint = _rope_tables(layout, ntok)
    bias = _window_bias()
    w_rows, w_t = _rearranged_w_in(w_in)
    wa, wb, wo = w_branch_a.astype(_BF), w_branch_b.astype(_BF), w_out.astype(_BF)
    wup, wdown = w_ffn_up.astype(_BF), w_ffn_down.astype(_BF)
    row = lambda g: g.reshape(1, -1).astype(_F32)
    col = lambda g: g.reshape(-1, 1).astype(_F32)

    tq_real = 512
    for l in range(depth):
        bound = (HEAD_DIM * SCALE * LOG2E * 1.02) * jnp.max(jnp.abs(q_norm_b[l])) * jnp.max(jnp.abs(k_norm_b[l]))
        unit = (jnp.arange(QK_ROWS - HEAD_DIM) == 0).astype(_F32).reshape(-1, 1)
        qat, qbt, kat, kbt, vat, vbt, ga, gb = _in_proj(
            h, row(g_mix_pre[l]), w_rows[l], w_t[l], cost, sint, col(q_norm_b[l]), col(k_norm_b[l]),
            -bound.astype(_F32) * unit, unit)
        sink = sink_a[l].astype(_F32) * LOG2E
        ka_tail, va_tail = _tail_features(kat, n_real, nb), _tail_features(vat, n_real, nb)
        kameta = jnp.transpose(ka_tail, (2, 0, 1, 3))
        vameta = jnp.transpose(va_tail, (2, 0, 3, 1))
        lane_pad = ((0, 0), (0, 0), (0, 0), (0, BLOCK - N_META))
        kameta_pad = jnp.pad(kameta, lane_pad)
        vametat_pad = jnp.pad(jnp.transpose(va_tail, (2, 0, 1, 3)), lane_pad)
        qa_meta = jnp.transpose(_tail_features(qat, n_real, nb), (2, 3, 0, 1)).reshape(nb * N_META, Q_COLS)
        qa_meta = jnp.pad(qa_meta, ((0, tail - nb * N_META), (0, 0)))

        oa = _window_meta_attention(sink, qa_meta, kat, vat, kameta, vameta, layout, ntok)
        oa = _window_attention(sink, qat, kat, vat, kameta_pad, vametat_pad, bias, oa, layout)

        kbmeta = jnp.transpose(_tail_features(kbt, n_real, nb), (2, 0, 3, 1))
        vbmetat = jnp.transpose(_tail_features(vbt, n_real, nb), (2, 0, 1, 3))
        qmeta = _tail_features(qbt, n_real, nb).reshape(N_KV, GROUP, QK_ROWS, nb, N_META)
        qmeta = jnp.transpose(qmeta, (3, 0, 2, 1, 4)).reshape(nb, N_KV, QK_ROWS, GROUP * N_META)
        qmeta = jnp.pad(qmeta, ((0, 0), (0, 0), (0, 0), (0, V_LANES - GROUP * N_META)))

        glob = functools.partial(_global_attention, kbt=kbt, vbt=vbt, kmeta=kbmeta, vmetat=vbmetat, out_rows=ntok)
        ob = glob(qmeta, prev=None, tq=N_META, q_row0=n_real, q_rows_per_batch=N_META, grid_batches=b1,
                  valid_batches=b1, kv_batch0=0, kv_row0=0, n=n1)
        ob = glob(qmeta, prev=ob, tq=N_META, q_row0=n_real + b1 * N_META, q_rows_per_batch=N_META,
                  grid_batches=tail // N_META - b1, valid_batches=b2, kv_batch0=b1, kv_row0=b1 * n1, n=n2)

        def real_queries(prev, offset_is_bound):
            out = glob(qbt, prev=prev, tq=tq_real, q_row0=0, q_rows_per_batch=n1, grid_batches=b1, valid_batches=b1,
                       kv_batch0=0, kv_row0=0, n=n1, offset_is_bound=offset_is_bound)
            return glob(qbt, prev=out, tq=tq_real, q_row0=b1 * n1, q_rows_per_batch=n2, grid_batches=b2,
                        valid_batches=b2, kv_batch0=b1, kv_row0=b1 * n1, n=n2, offset_is_bound=offset_is_bound)

        ob = lax.cond(bound <= SAFE_OFFSET_MAX, functools.partial(real_queries, offset_is_bound=True),
                      functools.partial(real_queries, offset_is_bound=False), ob)

        h = _mix(h, oa, ob, ga, gb, wa[l], wb[l], wo[l], row(g_mix_post[l]))
        h = _ffn(h, row(g_ffn_pre[l]), wup[l], wdown[l], row(g_ffn_post[l]))

    y_prompt = h[:b1 * n1].reshape(b1, n1, D_MODEL)
    y_sample = h[b1 * n1:n_real].reshape(b2, n2, D_MODEL)
    return (y_prompt, y_sample)
```

```python
import functools
import math

import jax
import jax.numpy as jnp
import numpy as np
from jax import lax
from jax.experimental import pallas as pl
from jax.experimental.pallas import tpu as pltpu

D_MODEL = 1024
HEAD_DIM = 64
N_HEADS = 8
N_KV = 2
GROUP = N_HEADS // N_KV
Q_COLS = N_HEADS * HEAD_DIM
KV_COLS = N_KV * HEAD_DIM
N_META = 16
BLOCK = 128
GRID_W = 64
ROPE_BASE = 10000.0
ROPE_FREQS = HEAD_DIM // 4
D_FF = 2816
EPS = 1e-6
NEG_INF = -1e30
SCALE = HEAD_DIM ** -0.5
LOG2E = math.log2(math.e)

QK_ROWS = HEAD_DIM + 16
SAFE_OFFSET_MAX = 40.0
V_LANES = 128
TOKEN_TILE = 512
DENSE_TILE = 1024
VMEM_LIMIT = 56 * 1024 * 1024

_C_GA = 0
_C_GB = _C_GA + D_MODEL
_C_END = _C_GB + D_MODEL
_R_KA = 0
_R_KB = _R_KA + KV_COLS
_R_VB = _R_KB + KV_COLS
_R_QB = _R_VB + N_KV * V_LANES
_R_VA = _R_QB + Q_COLS
_R_QA = _R_VA + N_KV * V_LANES
_R_END = _R_QA + Q_COLS

_BF = jnp.bfloat16
_F32 = jnp.float32


def _dot(a, b):
    return jnp.dot(a, b, preferred_element_type=_F32)


def _dot_nt(a, b):
    return lax.dot_general(a, b, (((1,), (1,)), ((), ())), preferred_element_type=_F32)


def _dot_tn(a, b):
    return lax.dot_general(a, b, (((0,), (0,)), ((), ())), preferred_element_type=_F32)


def _params(sem, vmem=VMEM_LIMIT):
    return pltpu.CompilerParams(dimension_semantics=sem, vmem_limit_bytes=vmem)


def _const_spec(shape):
    nd = len(shape)
    return pl.BlockSpec(shape, lambda *_: (0,) * nd, pipeline_mode=pl.Buffered(1))


def _in_proj_body(h_ref, g_ref, w_ref, wt_ref, cost_ref, sint_ref, qg_ref, kg_ref, qx_ref, kx_ref,
                  qat_ref, qbt_ref, kat_ref, kbt_ref, vat_ref, vbt_ref, ga_ref, gb_ref):
    f = ROPE_FREQS
    ones_row = (lax.broadcasted_iota(jnp.int32, (V_LANES, 1), 0) == HEAD_DIM).astype(_F32)
    q_extra = jnp.broadcast_to(qx_ref[...], (QK_ROWS - HEAD_DIM, TOKEN_TILE))
    k_extra = jnp.broadcast_to(kx_ref[...], (QK_ROWS - HEAD_DIM, TOKEN_TILE))

    for cc in range(h_ref.shape[0] // TOKEN_TILE):
        rows = slice(cc * TOKEN_TILE, (cc + 1) * TOKEN_TILE)
        h = h_ref[rows, :]
        ms = jnp.mean(h * h, axis=-1, keepdims=True)
        xn = (h * lax.rsqrt(ms + EPS) * g_ref[...]).astype(_BF)

        ga_ref[rows, :] = jax.nn.sigmoid(_dot(xn, w_ref[:, _C_GA:_C_GA + D_MODEL])).astype(_BF)
        gb_ref[rows, :] = jax.nn.sigmoid(_dot(xn, w_ref[:, _C_GB:_C_GB + D_MODEL])).astype(_BF)

        tt = _dot_nt(wt_ref[...], xn)
        cost = cost_ref[:, rows]
        sint = sint_ref[:, rows]

        def norm_rope(x, gain):
            x = x * lax.rsqrt(jnp.mean(x * x, axis=0, keepdims=True) + EPS) * gain
            partner = jnp.concatenate([x[f:2 * f], x[0:f], x[3 * f:4 * f], x[2 * f:3 * f]], axis=0)
            return x * cost + partner * sint

        for j in range(N_KV):
            kat_ref[j, cc] = tt[_R_KA + j * HEAD_DIM:_R_KA + (j + 1) * HEAD_DIM].astype(_BF)
            k = norm_rope(tt[_R_KB + j * HEAD_DIM:_R_KB + (j + 1) * HEAD_DIM], kg_ref[...])
            kbt_ref[j, cc] = jnp.concatenate([k, k_extra], axis=0).astype(_BF)
            vat_ref[j, cc] = (tt[_R_VA + j * V_LANES:_R_VA + (j + 1) * V_LANES] + ones_row).astype(_BF)
            vbt_ref[j, cc] = (tt[_R_VB + j * V_LANES:_R_VB + (j + 1) * V_LANES] + ones_row).astype(_BF)
        for hd in range(N_HEADS):
            qa = tt[_R_QA + hd * HEAD_DIM:_R_QA + (hd + 1) * HEAD_DIM] * (SCALE * LOG2E)
            qat_ref[hd, cc] = qa.astype(_BF)
            q = norm_rope(tt[_R_QB + hd * HEAD_DIM:_R_QB + (hd + 1) * HEAD_DIM], qg_ref[...])
            qbt_ref[hd, cc] = jnp.concatenate([q * (SCALE * LOG2E), q_extra], axis=0).astype(_BF)


def _in_proj_first_body(blocks, xp_ref, xs_ref, tail_ref, *refs):
    p, sm = blocks
    i = pl.program_id(0)
    *rest, h_ref = refs
    h_ref[...] = jnp.where(i < p, xp_ref[...], jnp.where(i < p + sm, xs_ref[...], tail_ref[...]))
    _in_proj_body(h_ref, *rest)


def _in_proj(h, g, w, wt, cost, sint, qg, kg, qx, kx):
    first = isinstance(h, tuple)
    t = TOKEN_TILE if first else DENSE_TILE
    ntok = sum(x.shape[0] for x in h) if first else h.shape[0]
    per = t // TOKEN_TILE
    nch = ntok // TOKEN_TILE
    row = lambda i: (i, 0)
    chunk = lambda i: (0, i, 0, 0)
    feat = lambda heads, rows: (jax.ShapeDtypeStruct((heads, nch, rows, TOKEN_TILE), _BF),
                                pl.BlockSpec((heads, per, rows, TOKEN_TILE), chunk))
    outs = [
        feat(N_HEADS, HEAD_DIM),
        feat(N_HEADS, QK_ROWS),
        feat(N_KV, HEAD_DIM),
        feat(N_KV, QK_ROWS),
        feat(N_KV, V_LANES),
        feat(N_KV, V_LANES),
        (jax.ShapeDtypeStruct((ntok, D_MODEL), _BF), pl.BlockSpec((t, D_MODEL), row)),
        (jax.ShapeDtypeStruct((ntok, D_MODEL), _BF), pl.BlockSpec((t, D_MODEL), row)),
    ]
    if first:
        p, sm, tl = (x.shape[0] // t for x in h)
        body = functools.partial(_in_proj_first_body, (p, sm))
        h_specs = [pl.BlockSpec((t, D_MODEL), lambda i: (jnp.minimum(i, p - 1), 0)),
                   pl.BlockSpec((t, D_MODEL), lambda i: (jnp.clip(i - p, 0, sm - 1), 0)),
                   pl.BlockSpec((t, D_MODEL), lambda i: (jnp.clip(i - p - sm, 0, tl - 1), 0))]
        outs.append((jax.ShapeDtypeStruct((ntok, D_MODEL), _F32), pl.BlockSpec((t, D_MODEL), row)))
        h_args = list(h)
    else:
        body, h_specs, h_args = _in_proj_body, [pl.BlockSpec((t, D_MODEL), row)], [h]
    return pl.pallas_call(
        body,
        grid=(ntok // t,),
        in_specs=h_specs + [
            _const_spec((1, D_MODEL)),
            _const_spec((D_MODEL, _C_END)),
            _const_spec((_R_END, D_MODEL)),
            pl.BlockSpec((HEAD_DIM, t), lambda i: (0, i)),
            pl.BlockSpec((HEAD_DIM, t), lambda i: (0, i)),
            _const_spec((HEAD_DIM, 1)),
            _const_spec((HEAD_DIM, 1)),
            _const_spec((QK_ROWS - HEAD_DIM, 1)),
            _const_spec((QK_ROWS - HEAD_DIM, 1)),
        ],
        out_specs=tuple(o[1] for o in outs),
        out_shape=tuple(o[0] for o in outs),
        compiler_params=_params(("parallel",)),
        name="in_proj_first" if first else "in_proj",
    )(*h_args, g, w, wt, cost, sint, qg, kg, qx, kx)


_WINDOW_LOOKAHEAD = 3


def _stack_heads(q, j):
    base = j * GROUP * HEAD_DIM
    return jnp.concatenate([q[:, base + g * HEAD_DIM: base + (g + 1) * HEAD_DIM] for g in range(GROUP)], axis=0)


def _sink_rows(sink_ref, j, rows, lanes):
    return jnp.concatenate([jnp.full((rows, lanes), sink_ref[j * GROUP + g], _F32) for g in range(GROUP)], axis=0)


def _window_body(geom, sink_ref, q_ref, kmain_ref, kprev_ref, knext_ref, vmain_ref, vprev_ref, vnext_ref,
                 kmeta_ref, vmeta_ref, bias_ref, _tail_ref, o_ref):
    t1, tpb1, tpb2 = geom
    t = pl.program_id(0)
    in_prompt = t < t1
    is_first = jnp.where(in_prompt, t % tpb1 == 0, (t - t1) % tpb2 == 0)
    is_last = jnp.where(in_prompt, t % tpb1 == tpb1 - 1, (t - t1) % tpb2 == tpb2 - 1)
    nblk = TOKEN_TILE // BLOCK

    def softmax_values(j, r, s, vwin, sink):
        m = jnp.maximum(jnp.max(s, axis=0, keepdims=True), sink)
        acc = _dot(vwin, jnp.exp2(s - m).astype(_BF))
        l = acc[HEAD_DIM:HEAD_DIM + 1] + jnp.exp2(sink - m)
        o = (acc / l).T.astype(_BF)
        for g in range(GROUP):
            c0 = (j * GROUP + g) * HEAD_DIM
            o_ref[r * BLOCK:(r + 1) * BLOCK, c0:c0 + HEAD_DIM] = o[g * BLOCK:(g + 1) * BLOCK, :HEAD_DIM]

    pending = []
    for j in range(N_KV):
        kcat = jnp.concatenate([kprev_ref[j, 0], kmain_ref[j, 0], knext_ref[j, 0]], axis=1)
        vcat = jnp.concatenate([vprev_ref[j, 0], vmain_ref[j, 0], vnext_ref[j, 0]], axis=1)
        kmeta = kmeta_ref[0, j]
        vmeta = vmeta_ref[0, j]
        sink = jnp.concatenate([jnp.full((1, BLOCK), sink_ref[j * GROUP + g], _F32) for g in range(GROUP)], axis=1)
        for r in range(nblk):
            variant = 0
            if r == 0:
                variant = jnp.where(is_first, 1, 0)
            if r == nblk - 1:
                variant = jnp.where(is_last, 2, variant)
            qt = jnp.concatenate([q_ref[j * GROUP + g, 0][:, r * BLOCK:(r + 1) * BLOCK] for g in range(GROUP)],
                                 axis=1)
            kwin = jnp.concatenate([kcat[:, r * BLOCK:(r + 3) * BLOCK], kmeta], axis=1)
            vwin = jnp.concatenate([vcat[:, r * BLOCK:(r + 3) * BLOCK], vmeta], axis=1)
            s = _dot_tn(kwin, qt) + bias_ref[variant, j]
            pending.append((j, r, s, vwin, sink))
            if len(pending) > _WINDOW_LOOKAHEAD:
                softmax_values(*pending.pop(0))
    for item in pending:
        softmax_values(*item)


def _window_attention(sink, qat, kat, vat, kmeta, vmetat, bias, tail, layout):
    n_real, b1, n1, b2, n2 = layout
    t = TOKEN_TILE
    nch = kat.shape[1]
    t1 = b1 * n1 // t
    tpb1, tpb2 = n1 // t, n2 // t
    sub = t // BLOCK

    def bid(i):
        return jnp.where(i < t1, i // tpb1, b1 + (i - t1) // tpb2)

    main = lambda i: (0, i, 0, 0)
    prev = lambda i: (0, jnp.maximum(i - 1, 0), 0, sub - 1)
    nxt = lambda i: (0, jnp.minimum(i + 1, nch - 1), 0, 0)
    return pl.pallas_call(
        functools.partial(_window_body, (t1, tpb1, tpb2)),
        grid=(n_real // t,),
        in_specs=[
            pl.BlockSpec(memory_space=pltpu.SMEM),
            pl.BlockSpec((N_HEADS, 1, HEAD_DIM, t), main),
            pl.BlockSpec((N_KV, 1, HEAD_DIM, t), main),
            pl.BlockSpec((N_KV, 1, HEAD_DIM, BLOCK), prev),
            pl.BlockSpec((N_KV, 1, HEAD_DIM, BLOCK), nxt),
            pl.BlockSpec((N_KV, 1, V_LANES, t), main),
            pl.BlockSpec((N_KV, 1, V_LANES, BLOCK), prev),
            pl.BlockSpec((N_KV, 1, V_LANES, BLOCK), nxt),
            pl.BlockSpec((1, N_KV, HEAD_DIM, BLOCK), lambda i: (bid(i), 0, 0, 0)),
            pl.BlockSpec((1, N_KV, V_LANES, BLOCK), lambda i: (bid(i), 0, 0, 0)),
            _const_spec((3, N_KV, 4 * BLOCK, GROUP * BLOCK)),
            pl.BlockSpec(memory_space=pl.ANY),
        ],
        out_specs=pl.BlockSpec((t, Q_COLS), lambda i: (i, 0)),
        out_shape=jax.ShapeDtypeStruct(tail.shape, _BF),
        input_output_aliases={11: 0},
        compiler_params=_params(("parallel",)),
        name="window_attn",
    )(sink, qat, kat, kat, kat, vat, vat, vat, kmeta, vmetat, bias, tail)


def _window_meta_body(nb, sink_ref, q_ref, kfirst_ref, vfirst_ref, kmeta_ref, vmeta_ref, o_ref):
    b = pl.program_id(0)

    @pl.when(b < nb)
    def _():
        q_all = q_ref[...]
        for j in range(N_KV):
            qs = _stack_heads(q_all, j)
            sink = _sink_rows(sink_ref, j, N_META, 1)
            sm = _dot(qs, kmeta_ref[0, j])
            sf = _dot(qs, kfirst_ref[j, 0])
            m = jnp.maximum(jnp.maximum(sm.max(axis=1, keepdims=True), sf.max(axis=1, keepdims=True)), sink)
            pm = jnp.exp2(sm - m).astype(_BF)
            pf = jnp.exp2(sf - m).astype(_BF)
            acc = _dot(pm, vmeta_ref[0, j]) + _dot_nt(pf, vfirst_ref[j, 0])
            l = acc[:, HEAD_DIM:HEAD_DIM + 1] + jnp.exp2(sink - m)
            o = (acc[:, :HEAD_DIM] / l).astype(_BF)
            for g in range(GROUP):
                c0 = (j * GROUP + g) * HEAD_DIM
                o_ref[:, c0:c0 + HEAD_DIM] = o[g * N_META:(g + 1) * N_META]

    @pl.when(b >= nb)
    def _():
        o_ref[...] = jnp.zeros(o_ref.shape, o_ref.dtype)


def _window_meta_attention(sink, q_rows, kat, vat, kmeta, vmeta, layout, out_rows):
    n_real, b1, n1, b2, n2 = layout
    nb = b1 + b2
    t = TOKEN_TILE

    def start_chunk(b):
        bc = jnp.minimum(b, nb - 1)
        return jnp.where(bc < b1, bc * (n1 // t), b1 * (n1 // t) + (bc - b1) * (n2 // t))

    return pl.pallas_call(
        functools.partial(_window_meta_body, nb),
        grid=((out_rows - n_real) // N_META,),
        in_specs=[
            pl.BlockSpec(memory_space=pltpu.SMEM),
            pl.BlockSpec((N_META, Q_COLS), lambda b: (b, 0)),
            pl.BlockSpec((N_KV, 1, HEAD_DIM, BLOCK), lambda b: (0, start_chunk(b), 0, 0)),
            pl.BlockSpec((N_KV, 1, V_LANES, BLOCK), lambda b: (0, start_chunk(b), 0, 0)),
            pl.BlockSpec((1, N_KV, HEAD_DIM, N_META), lambda b: (jnp.minimum(b, nb - 1), 0, 0, 0)),
            pl.BlockSpec((1, N_KV, N_META, V_LANES), lambda b: (jnp.minimum(b, nb - 1), 0, 0, 0)),
        ],
        out_specs=pl.BlockSpec((N_META, Q_COLS), lambda b: (n_real // N_META + b, 0)),
        out_shape=jax.ShapeDtypeStruct((out_rows, Q_COLS), _BF),
        compiler_params=_params(("arbitrary",)),
        name="window_meta_attn",
    )(sink, q_rows, kat, vat, kmeta, vmeta)


_KEY_UNROLL = 4
_QUERY_BLOCK = 256
_SCORE_LOOKAHEAD = 3


def _global_body(nvalid, nchunks, tq, q_ref, kt_ref, vt_ref, kmeta_ref, vmetat_ref, _prev_ref, o_ref,
                 qt_ref, s0_ref, s1_ref, m_ref, acc_ref):
    b = pl.program_id(0)

    cols = qt_ref.shape[1]
    blocks = [slice(c0, min(c0 + _QUERY_BLOCK, cols)) for c0 in range(0, cols, _QUERY_BLOCK)]

    def scores(c, s_ref):
        for sl in blocks:
            s_ref[:, sl] = _dot_tn(kt_ref[0, c], qt_ref[:, sl])

    def softmax_pv(c, s_ref):
        for sl in blocks:
            s = s_ref[:, sl]
            m_prev = m_ref[:, sl]
            m_new = jnp.maximum(m_prev, jnp.max(s, axis=0, keepdims=True))
            p = jnp.exp2(s - m_new).astype(_BF)
            acc_ref[:, sl] = jnp.exp2(m_prev - m_new) * acc_ref[:, sl] + _dot(vt_ref[0, c], p)
            m_ref[:, sl] = m_new

    @pl.when(b < nvalid)
    def _():
        _stack_queries(q_ref, qt_ref)
        sm = _dot(kmeta_ref[0, 0], qt_ref[...])
        m0 = jnp.max(sm, axis=0, keepdims=True)
        m_ref[...] = m0
        acc_ref[...] = _dot(vmetat_ref[0, 0], jnp.exp2(sm - m0).astype(_BF))
        bufs = (s0_ref, s1_ref)
        scores(0, s0_ref)

        unroll = _KEY_UNROLL if nchunks > _KEY_UNROLL else 2

        def group(i, carry):
            c0 = unroll * i
            for u in range(unroll):
                scores(c0 + u + 1, bufs[(u + 1) % 2])
                softmax_pv(c0 + u, bufs[u % 2])
            return carry

        full = (nchunks - 1) // unroll
        lax.fori_loop(0, full, group, 0)
        for c in range(full * unroll, nchunks):
            if c + 1 < nchunks:
                scores(c + 1, bufs[(c + 1) % 2])
            softmax_pv(c, bufs[c % 2])
        _write_output(acc_ref, o_ref, tq)

    @pl.when(b >= nvalid)
    def _():
        o_ref[...] = jnp.zeros(o_ref.shape, o_ref.dtype)


def _global_fast_body(nvalid, nchunks, tq, q_ref, kt_ref, vt_ref, kmeta_ref, vmetat_ref, _prev_ref, o_ref,
                      qt_ref, acc_ref):
    b = pl.program_id(0)
    cols = qt_ref.shape[1]
    blocks = [slice(c0, min(c0 + _QUERY_BLOCK, cols)) for c0 in range(0, cols, _QUERY_BLOCK)]

    def chunks(cs):
        pending = []
        for c in cs:
            for sl in blocks:
                pending.append((c, sl, _dot_tn(kt_ref[0, c], qt_ref[:, sl])))
                if len(pending) > _SCORE_LOOKAHEAD:
                    values(*pending.pop(0))
        for item in pending:
            values(*item)

    def values(c, sl, s):
        acc_ref[:, sl] += _dot(vt_ref[0, c], jnp.exp2(s).astype(_BF))

    @pl.when(b < nvalid)
    def _():
        _stack_queries(q_ref, qt_ref)
        acc_ref[...] = _dot(vmetat_ref[0, 0], jnp.exp2(_dot(kmeta_ref[0, 0], qt_ref[...])).astype(_BF))

        def group(i, carry):
            chunks([_KEY_UNROLL * i + u for u in range(_KEY_UNROLL)])
            return carry

        full = nchunks // _KEY_UNROLL
        lax.fori_loop(0, full, group, 0)
        if full * _KEY_UNROLL < nchunks:
            chunks(range(full * _KEY_UNROLL, nchunks))
        _write_output(acc_ref, o_ref, tq)

    @pl.when(b >= nvalid)
    def _():
        o_ref[...] = jnp.zeros(o_ref.shape, o_ref.dtype)


def _stack_queries(q_ref, qt_ref):
    if q_ref.shape[0] == GROUP:
        qt_ref[...] = jnp.concatenate([q_ref[g, 0] for g in range(GROUP)], axis=1)
    else:
        qt_ref[...] = q_ref[0, 0]


def _write_output(acc_ref, o_ref, tq):
    acc = acc_ref[...]
    o = (acc / acc[HEAD_DIM:HEAD_DIM + 1]).T.astype(_BF)
    for g in range(GROUP):
        o_ref[:, g * HEAD_DIM:(g + 1) * HEAD_DIM] = o[g * tq:(g + 1) * tq, :HEAD_DIM]


def _global_attention(q, kbt, vbt, kmeta, vmetat, prev, out_rows, *, tq, q_row0, q_rows_per_batch, grid_batches,
                      valid_batches, kv_batch0, kv_row0, n, offset_is_bound=False):
    t = TOKEN_TILE
    nchunks = n // t
    qt = q_rows_per_batch // tq
    assert q_row0 % tq == 0 and kv_row0 % n == 0 and q_rows_per_batch % tq == 0
    kvb = lambda b: jnp.minimum(b, valid_batches - 1)
    if tq == N_META:
        cols = V_LANES
        q_spec = pl.BlockSpec((1, 1, QK_ROWS, cols), lambda b, j, i: (kv_batch0 + kvb(b), j, 0, 0))
    else:
        cols = GROUP * tq
        assert t % tq == 0 and q_row0 % t == 0 and q_rows_per_batch % t == 0
        q_spec = pl.BlockSpec(
            (GROUP, 1, QK_ROWS, tq),
            lambda b, j, i: (j, (q_row0 + b * q_rows_per_batch) // t + i // (t // tq), 0, i % (t // tq)))
    in_specs = [
        q_spec,
        pl.BlockSpec((1, nchunks, QK_ROWS, t), lambda b, j, i: (j, kv_row0 // n + kvb(b), 0, 0)),
        pl.BlockSpec((1, nchunks, V_LANES, t), lambda b, j, i: (j, kv_row0 // n + kvb(b), 0, 0)),
        pl.BlockSpec((1, 1, N_META, QK_ROWS), lambda b, j, i: (kv_batch0 + kvb(b), j, 0, 0)),
        pl.BlockSpec((1, 1, V_LANES, N_META), lambda b, j, i: (kv_batch0 + kvb(b), j, 0, 0)),
    ]
    args = [q, kbt, vbt, kmeta, vmetat]
    aliases = {}
    body = functools.partial(_global_fast_body if offset_is_bound else _global_body, valid_batches, nchunks, tq)
    score_bufs = [] if offset_is_bound else [pltpu.VMEM((t, cols), _F32), pltpu.VMEM((t, cols), _F32),
                                             pltpu.VMEM((1, cols), _F32)]
    if prev is None:
        body = functools.partial(_global_body_noprev, body)
    else:
        in_specs.append(pl.BlockSpec(memory_space=pl.ANY))
        args.append(prev)
        aliases = {5: 0}
    return pl.pallas_call(
        body,
        grid=(grid_batches, N_KV, qt),
        in_specs=in_specs,
        out_specs=pl.BlockSpec((tq, GROUP * HEAD_DIM), lambda b, j, i: (q_row0 // tq + b * qt + i, j)),
        out_shape=jax.ShapeDtypeStruct((out_rows, Q_COLS), _BF),
        scratch_shapes=[pltpu.VMEM((QK_ROWS, cols), _BF)] + score_bufs + [pltpu.VMEM((V_LANES, cols), _F32)],
        input_output_aliases=aliases,
        compiler_params=_params(("parallel", "parallel", "arbitrary")),
        name=f"global_attn{'_fast' if offset_is_bound else ''}_tq{tq}_n{n}",
    )(*args)


def _global_body_noprev(body, q_ref, kt_ref, vt_ref, kmeta_ref, vmetat_ref, o_ref, *scratch):
    body(q_ref, kt_ref, vt_ref, kmeta_ref, vmetat_ref, None, o_ref, *scratch)


_MIX_ROWS = 256


def _mix_body(h_ref, oa_ref, ob_ref, ga_ref, gb_ref, wa_ref, wb_ref, wo_ref, g_ref, o_ref):
    def project(rows, mix):
        u = _dot(mix.astype(_BF), wo_ref[...])
        ms = jnp.mean(u * u, axis=-1, keepdims=True)
        o_ref[rows, :] = h_ref[rows, :] + u * lax.rsqrt(ms + EPS) * g_ref[...]

    pending = None
    for r0 in range(0, h_ref.shape[0], _MIX_ROWS):
        rows = slice(r0, r0 + _MIX_ROWS)
        mix = (ga_ref[rows, :].astype(_F32) * _dot(oa_ref[rows, :], wa_ref[...])
               + gb_ref[rows, :].astype(_F32) * _dot(ob_ref[rows, :], wb_ref[...]))
        if pending is not None:
            project(*pending)
        pending = (rows, mix)
    project(*pending)


def _mix(h, oa, ob, ga, gb, wa, wb, wo, g):
    ntok = h.shape[0]
    t = DENSE_TILE
    row = lambda i: (i, 0)
    return pl.pallas_call(
        _mix_body,
        grid=(ntok // t,),
        in_specs=[
            pl.BlockSpec((t, D_MODEL), row),
            pl.BlockSpec((t, Q_COLS), row), pl.BlockSpec((t, Q_COLS), row),
            pl.BlockSpec((t, D_MODEL), row), pl.BlockSpec((t, D_MODEL), row),
            _const_spec((Q_COLS, D_MODEL)), _const_spec((Q_COLS, D_MODEL)),
            _const_spec((D_MODEL, D_MODEL)), _const_spec((1, D_MODEL)),
        ],
        out_specs=pl.BlockSpec((t, D_MODEL), row),
        out_shape=jax.ShapeDtypeStruct(h.shape, _F32),
        input_output_aliases={0: 0},
        compiler_params=_params(("parallel",)),
        name="branch_mix",
    )(h, oa, ob, ga, gb, wa, wb, wo, g)


_FF_CHUNK = 256


def _ffn_body(h_ref, gpre_ref, wup_ref, wdown_ref, gpost_ref, o_ref, act_ref):
    for r0 in range(0, h_ref.shape[0], TOKEN_TILE):
        rows = slice(r0, r0 + TOKEN_TILE)
        h = h_ref[rows, :]
        ms = jnp.mean(h * h, axis=-1, keepdims=True)
        xn = (h * lax.rsqrt(ms + EPS) * gpre_ref[...]).astype(_BF)
        for c in range(0, D_FF, _FF_CHUNK):
            a = _dot(xn, wup_ref[:, c:c + _FF_CHUNK])
            b = _dot(xn, wup_ref[:, D_FF + c:D_FF + c + _FF_CHUNK])
            act_ref[rows, c:c + _FF_CHUNK] = (a * jax.nn.sigmoid(a) * b).astype(_BF)
        u = _dot(act_ref[rows, :], wdown_ref[...])
        ms = jnp.mean(u * u, axis=-1, keepdims=True)
        o_ref[rows, :] = h + u * lax.rsqrt(ms + EPS) * gpost_ref[...]


def _ffn_last_body(blocks, h_ref, gpre_ref, wup_ref, wdown_ref, gpost_ref, yp_ref, ys_ref, act_ref):
    p, sm = blocks
    i = pl.program_id(0)

    @pl.when(i < p)
    def _():
        _ffn_body(h_ref, gpre_ref, wup_ref, wdown_ref, gpost_ref, yp_ref, act_ref)

    @pl.when(jnp.logical_and(i >= p, i < p + sm))
    def _():
        _ffn_body(h_ref, gpre_ref, wup_ref, wdown_ref, gpost_ref, ys_ref, act_ref)


def _ffn(h, gpre, wup, wdown, gpost, split_rows=None):
    ntok = h.shape[0]
    t = DENSE_TILE
    row = lambda i: (i, 0)
    if split_rows is None:
        body, aliases = _ffn_body, {0: 0}
        out_specs = pl.BlockSpec((t, D_MODEL), row)
        out_shape = jax.ShapeDtypeStruct(h.shape, _F32)
    else:
        p, sm = (r // t for r in split_rows)
        assert p * t == split_rows[0] and sm * t == split_rows[1]
        body, aliases = functools.partial(_ffn_last_body, (p, sm)), {}
        out_specs = (pl.BlockSpec((t, D_MODEL), lambda i: (jnp.minimum(i, p - 1), 0)),
                     pl.BlockSpec((t, D_MODEL), lambda i: (jnp.clip(i - p, 0, sm - 1), 0)))
        out_shape = (jax.ShapeDtypeStruct((split_rows[0], D_MODEL), _F32),
                     jax.ShapeDtypeStruct((split_rows[1], D_MODEL), _F32))
    return pl.pallas_call(
        body,
        grid=(ntok // t,),
        in_specs=[
            pl.BlockSpec((t, D_MODEL), row),
            _const_spec((1, D_MODEL)),
            _const_spec((D_MODEL, 2 * D_FF)),
            _const_spec((D_FF, D_MODEL)),
            _const_spec((1, D_MODEL)),
        ],
        out_specs=out_specs,
        out_shape=out_shape,
        scratch_shapes=[pltpu.VMEM((t, D_FF), _BF)],
        input_output_aliases=aliases,
        compiler_params=_params(("arbitrary",)),
        name="swiglu_ffn" if split_rows is None else "swiglu_ffn_last",
    )(h, gpre, wup, wdown, gpost)


def _rope_tables(layout, ntok):
    n_real, b1, n1, b2, n2 = layout
    idx = np.zeros((ntok,), np.int64)
    idx[:b1 * n1] = np.arange(b1 * n1) % n1
    idx[b1 * n1:n_real] = np.arange(b2 * n2) % n2
    rows = (idx // GRID_W).astype(np.float32)
    cols = (idx % GRID_W).astype(np.float32)
    rows[n_real:] = 0.0
    cols[n_real:] = 0.0
    freqs = ROPE_BASE ** (-jnp.arange(ROPE_FREQS, dtype=_F32) / ROPE_FREQS)
    ang_r = jnp.asarray(rows)[:, None] * freqs[None, :]
    ang_c = jnp.asarray(cols)[:, None] * freqs[None, :]
    cr, sr, cc, sc = jnp.cos(ang_r), jnp.sin(ang_r), jnp.cos(ang_c), jnp.sin(ang_c)
    cos64 = jnp.concatenate([cr, cr, cc, cc], axis=1)
    sin64 = jnp.concatenate([-sr, sr, -sc, sc], axis=1)
    return cos64.T, sin64.T


def _window_bias():
    slopes = 2.0 ** (-8.0 * np.arange(1, N_HEADS + 1, dtype=np.float64) / N_HEADS)
    rel = np.arange(BLOCK)[:, None] - (np.arange(3 * BLOCK) - BLOCK)[None, :]
    dist = np.abs(rel)
    band = dist <= BLOCK
    local = np.where(band[None], -slopes[:, None, None] * dist[None].astype(np.float64) * LOG2E, NEG_INF)
    extra = np.full((N_HEADS, BLOCK, BLOCK), NEG_INF)
    extra[:, :, :N_META] = 0.0
    base = np.concatenate([local, extra], axis=2)
    no_prev = base.copy()
    no_prev[:, :, :BLOCK] = NEG_INF
    no_next = base.copy()
    no_next[:, :, 2 * BLOCK:3 * BLOCK] = NEG_INF
    out = np.stack([base, no_prev, no_next]).reshape(3, N_KV, GROUP * BLOCK, 4 * BLOCK)
    return jnp.asarray(np.swapaxes(out, -1, -2), _F32)


def _rearranged_w_in(w_in):
    o = np.cumsum([0, Q_COLS, KV_COLS, KV_COLS, Q_COLS, KV_COLS, KV_COLS, D_MODEL, D_MODEL])
    qa, ka, va, qb, kb, vb, ga, gb = [w_in[..., o[i]:o[i + 1]] for i in range(8)]

    def pad_v(v):
        z = jnp.zeros(v.shape[:-1] + (V_LANES - HEAD_DIM,), v.dtype)
        return jnp.concatenate([v[..., :HEAD_DIM], z, v[..., HEAD_DIM:], z], axis=-1)

    w_rows = jnp.concatenate([ga, gb], axis=-1).astype(_BF)
    w_t = jnp.swapaxes(jnp.concatenate([ka, kb, pad_v(vb), qb, pad_v(va), qa], axis=-1), -1, -2).astype(_BF)
    return w_rows, w_t


def _tail_features(xt, n_real, nb):
    tail = xt[:, n_real // TOKEN_TILE:]
    tail = jnp.moveaxis(tail, 2, 1).reshape(xt.shape[0], xt.shape[2], -1)[:, :, :nb * N_META]
    return tail.reshape(xt.shape[0], xt.shape[2], nb, N_META)


def kernel(x_prompt, x_sample, meta_tokens, g_mix_pre, g_mix_post, g_ffn_pre, g_ffn_post, w_in, q_norm_b,
           k_norm_b, sink_a, w_branch_a, w_branch_b, w_out, w_ffn_up, w_ffn_down):
    b1, n1, _ = x_prompt.shape
    b2, n2, _ = x_sample.shape
    depth = w_in.shape[0]
    t = TOKEN_TILE
    assert n1 % t == 0 and n2 % t == 0 and (b1 * n1) % n2 == 0 and n1 % GRID_W == 0 and n2 % GRID_W == 0
    assert (b1 * n1) % DENSE_TILE == 0 and (b2 * n2) % DENSE_TILE == 0
    nb = b1 + b2
    n_real = b1 * n1 + b2 * n2
    ntok = -(-(n_real + nb * N_META) // DENSE_TILE) * DENSE_TILE
    tail = ntok - n_real
    assert tail % t == 0
    layout = (n_real, b1, n1, b2, n2)

    h = (x_prompt.reshape(b1 * n1, D_MODEL).astype(_F32), x_sample.reshape(b2 * n2, D_MODEL).astype(_F32),
         jnp.concatenate([jnp.tile(meta_tokens.astype(_F32), (nb, 1)),
                          jnp.zeros((tail - nb * N_META, D_MODEL), _F32)], axis=0))

    cost, sint = _rope_tables(layout, ntok)
    bias = _window_bias()
    w_rows, w_t = _rearranged_w_in(w_in)
    wa, wb, wo = w_branch_a.astype(_BF), w_branch_b.astype(_BF), w_out.astype(_BF)
    wup, wdown = w_ffn_up.astype(_BF), w_ffn_down.astype(_BF)
    row = lambda g: g.reshape(1, -1).astype(_F32)
    col = lambda g: g.reshape(-1, 1).astype(_F32)

    tq_real = 512
    for l in range(depth):
        bound = (HEAD_DIM * SCALE * LOG2E * 1.02) * jnp.max(jnp.abs(q_norm_b[l])) * jnp.max(jnp.abs(k_norm_b[l]))
        unit = (jnp.arange(QK_ROWS - HEAD_DIM) == 0).astype(_F32).reshape(-1, 1)
        proj = _in_proj(h, row(g_mix_pre[l]), w_rows[l], w_t[l], cost, sint, col(q_norm_b[l]), col(k_norm_b[l]),
                        -bound.astype(_F32) * unit, unit)
        if l == 0:
            *proj, h = proj
        qat, qbt, kat, kbt, vat, vbt, ga, gb = proj
        sink = sink_a[l].astype(_F32) * LOG2E
        ka_tail, va_tail = _tail_features(kat, n_real, nb), _tail_features(vat, n_real, nb)
        kameta = jnp.transpose(ka_tail, (2, 0, 1, 3))
        vameta = jnp.transpose(va_tail, (2, 0, 3, 1))
        lane_pad = ((0, 0), (0, 0), (0, 0), (0, BLOCK - N_META))
        kameta_pad = jnp.pad(kameta, lane_pad)
        vametat_pad = jnp.pad(jnp.transpose(va_tail, (2, 0, 1, 3)), lane_pad)
        qa_meta = jnp.transpose(_tail_features(qat, n_real, nb), (2, 3, 0, 1)).reshape(nb * N_META, Q_COLS)
        qa_meta = jnp.pad(qa_meta, ((0, tail - nb * N_META), (0, 0)))

        oa = _window_meta_attention(sink, qa_meta, kat, vat, kameta, vameta, layout, ntok)
        oa = _window_attention(sink, qat, kat, vat, kameta_pad, vametat_pad, bias, oa, layout)

        kbmeta = jnp.transpose(_tail_features(kbt, n_real, nb), (2, 0, 3, 1))
        vbmetat = jnp.transpose(_tail_features(vbt, n_real, nb), (2, 0, 1, 3))
        qmeta = _tail_features(qbt, n_real, nb).reshape(N_KV, GROUP, QK_ROWS, nb, N_META)
        qmeta = jnp.transpose(qmeta, (3, 0, 2, 1, 4)).reshape(nb, N_KV, QK_ROWS, GROUP * N_META)
        qmeta = jnp.pad(qmeta, ((0, 0), (0, 0), (0, 0), (0, V_LANES - GROUP * N_META)))

        glob = functools.partial(_global_attention, kbt=kbt, vbt=vbt, kmeta=kbmeta, vmetat=vbmetat, out_rows=ntok)
        ob = glob(qmeta, prev=None, tq=N_META, q_row0=n_real, q_rows_per_batch=N_META, grid_batches=b1,
                  valid_batches=b1, kv_batch0=0, kv_row0=0, n=n1)
        ob = glob(qmeta, prev=ob, tq=N_META, q_row0=n_real + b1 * N_META, q_rows_per_batch=N_META,
                  grid_batches=tail // N_META - b1, valid_batches=b2, kv_batch0=b1, kv_row0=b1 * n1, n=n2)

        def real_queries(prev, offset_is_bound):
            out = glob(qbt, prev=prev, tq=tq_real, q_row0=0, q_rows_per_batch=n1, grid_batches=b1, valid_batches=b1,
                       kv_batch0=0, kv_row0=0, n=n1, offset_is_bound=offset_is_bound)
            return glob(qbt, prev=out, tq=tq_real, q_row0=b1 * n1, q_rows_per_batch=n2, grid_batches=b2,
                        valid_batches=b2, kv_batch0=b1, kv_row0=b1 * n1, n=n2, offset_is_bound=offset_is_bound)

        ob = lax.cond(bound <= SAFE_OFFSET_MAX, functools.partial(real_queries, offset_is_bound=True),
                      functools.partial(real_queries, offset_is_bound=False), ob)

        h = _mix(h, oa, ob, ga, gb, wa[l], wb[l], wo[l], row(g_mix_post[l]))
        h = _ffn(h, row(g_ffn_pre[l]), wup[l], wdown[l], row(g_ffn_post[l]),
                 split_rows=(b1 * n1, b2 * n2) if l == depth - 1 else None)

    y_prompt, y_sample = h
    return (y_prompt.reshape(b1, n1, D_MODEL).astype(x_prompt.dtype),
            y_sample.reshape(b2, n2, D_MODEL).astype(x_sample.dtype))
```

```python
import functools
import math

import jax
import jax.numpy as jnp
import numpy as np
from jax import lax
from jax.experimental import pallas as pl
from jax.experimental.pallas import tpu as pltpu

D_MODEL = 1024
HEAD_DIM = 64
N_HEADS = 8
N_KV = 2
GROUP = N_HEADS // N_KV
Q_COLS = N_HEADS * HEAD_DIM
KV_COLS = N_KV * HEAD_DIM
N_META = 16
BLOCK = 128
GRID_W = 64
ROPE_BASE = 10000.0
ROPE_FREQS = HEAD_DIM // 4
D_FF = 2816
EPS = 1e-6
NEG_INF = -1e30
SCALE = HEAD_DIM ** -0.5
LOG2E = math.log2(math.e)

QK_ROWS = HEAD_DIM + 16
SAFE_OFFSET_MAX = 40.0
V_LANES = 128
TOKEN_TILE = 512
DENSE_TILE = 1024
WINDOW_TILE = 1024
VMEM_LIMIT = 56 * 1024 * 1024

_C_GA = 0
_C_GB = _C_GA + D_MODEL
_C_END = _C_GB + D_MODEL
_R_KA = 0
_R_KB = _R_KA + KV_COLS
_R_VB = _R_KB + KV_COLS
_R_QB = _R_VB + N_KV * V_LANES
_R_VA = _R_QB + Q_COLS
_R_QA = _R_VA + N_KV * V_LANES
_R_END = _R_QA + Q_COLS

_BF = jnp.bfloat16
_F32 = jnp.float32


def _dot(a, b):
    return jnp.dot(a, b, preferred_element_type=_F32)


def _dot_nt(a, b):
    return lax.dot_general(a, b, (((1,), (1,)), ((), ())), preferred_element_type=_F32)


def _dot_tn(a, b):
    return lax.dot_general(a, b, (((0,), (0,)), ((), ())), preferred_element_type=_F32)


def _params(sem, vmem=VMEM_LIMIT):
    return pltpu.CompilerParams(dimension_semantics=sem, vmem_limit_bytes=vmem)


def _const_spec(shape):
    nd = len(shape)
    return pl.BlockSpec(shape, lambda *_: (0,) * nd, pipeline_mode=pl.Buffered(1))


def _in_proj_body(h_ref, g_ref, w_ref, wt_ref, cost_ref, sint_ref, qg_ref, kg_ref, qx_ref, kx_ref,
                  qat_ref, qbt_ref, kat_ref, kbt_ref, vat_ref, vbt_ref, ga_ref, gb_ref):
    f = ROPE_FREQS
    ones_row = (lax.broadcasted_iota(jnp.int32, (V_LANES, 1), 0) == HEAD_DIM).astype(_F32)
    q_extra = jnp.broadcast_to(qx_ref[...], (QK_ROWS - HEAD_DIM, TOKEN_TILE))
    k_extra = jnp.broadcast_to(kx_ref[...], (QK_ROWS - HEAD_DIM, TOKEN_TILE))

    for cc in range(h_ref.shape[0] // TOKEN_TILE):
        rows = slice(cc * TOKEN_TILE, (cc + 1) * TOKEN_TILE)
        h = h_ref[rows, :]
        ms = jnp.mean(h * h, axis=-1, keepdims=True)
        xn = (h * lax.rsqrt(ms + EPS) * g_ref[...]).astype(_BF)

        ga_ref[rows, :] = jax.nn.sigmoid(_dot(xn, w_ref[:, _C_GA:_C_GA + D_MODEL])).astype(_BF)
        gb_ref[rows, :] = jax.nn.sigmoid(_dot(xn, w_ref[:, _C_GB:_C_GB + D_MODEL])).astype(_BF)

        tt = _dot_nt(wt_ref[...], xn)
        cost = cost_ref[:, rows]
        sint = sint_ref[:, rows]

        def norm_rope(x, gain):
            x = x * lax.rsqrt(jnp.mean(x * x, axis=0, keepdims=True) + EPS) * gain
            partner = jnp.concatenate([x[f:2 * f], x[0:f], x[3 * f:4 * f], x[2 * f:3 * f]], axis=0)
            return x * cost + partner * sint

        for j in range(N_KV):
            kat_ref[j, cc] = tt[_R_KA + j * HEAD_DIM:_R_KA + (j + 1) * HEAD_DIM].astype(_BF)
            k = norm_rope(tt[_R_KB + j * HEAD_DIM:_R_KB + (j + 1) * HEAD_DIM], kg_ref[...])
            kbt_ref[j, cc] = jnp.concatenate([k, k_extra], axis=0).astype(_BF)
            vat_ref[j, cc] = (tt[_R_VA + j * V_LANES:_R_VA + (j + 1) * V_LANES] + ones_row).astype(_BF)
            vbt_ref[j, cc] = (tt[_R_VB + j * V_LANES:_R_VB + (j + 1) * V_LANES] + ones_row).astype(_BF)
        for hd in range(N_HEADS):
            qa = tt[_R_QA + hd * HEAD_DIM:_R_QA + (hd + 1) * HEAD_DIM] * (SCALE * LOG2E)
            qat_ref[hd, cc] = qa.astype(_BF)
            q = norm_rope(tt[_R_QB + hd * HEAD_DIM:_R_QB + (hd + 1) * HEAD_DIM], qg_ref[...])
            qbt_ref[hd, cc] = jnp.concatenate([q * (SCALE * LOG2E), q_extra], axis=0).astype(_BF)


def _in_proj_first_body(blocks, xp_ref, xs_ref, tail_ref, *refs):
    p, sm = blocks
    i = pl.program_id(0)
    *rest, h_ref = refs
    h_ref[...] = jnp.where(i < p, xp_ref[...], jnp.where(i < p + sm, xs_ref[...], tail_ref[...]))
    _in_proj_body(h_ref, *rest)


def _in_proj(h, g, w, wt, cost, sint, qg, kg, qx, kx):
    first = isinstance(h, tuple)
    t = TOKEN_TILE if first else DENSE_TILE
    ntok = sum(x.shape[0] for x in h) if first else h.shape[0]
    per = t // TOKEN_TILE
    nch = ntok // TOKEN_TILE
    row = lambda i: (i, 0)
    chunk = lambda i: (0, i, 0, 0)
    feat = lambda heads, rows: (jax.ShapeDtypeStruct((heads, nch, rows, TOKEN_TILE), _BF),
                                pl.BlockSpec((heads, per, rows, TOKEN_TILE), chunk))
    outs = [
        feat(N_HEADS, HEAD_DIM),
        feat(N_HEADS, QK_ROWS),
        feat(N_KV, HEAD_DIM),
        feat(N_KV, QK_ROWS),
        feat(N_KV, V_LANES),
        feat(N_KV, V_LANES),
        (jax.ShapeDtypeStruct((ntok, D_MODEL), _BF), pl.BlockSpec((t, D_MODEL), row)),
        (jax.ShapeDtypeStruct((ntok, D_MODEL), _BF), pl.BlockSpec((t, D_MODEL), row)),
    ]
    if first:
        p, sm, tl = (x.shape[0] // t for x in h)
        body = functools.partial(_in_proj_first_body, (p, sm))
        h_specs = [pl.BlockSpec((t, D_MODEL), lambda i: (jnp.minimum(i, p - 1), 0)),
                   pl.BlockSpec((t, D_MODEL), lambda i: (jnp.clip(i - p, 0, sm - 1), 0)),
                   pl.BlockSpec((t, D_MODEL), lambda i: (jnp.clip(i - p - sm, 0, tl - 1), 0))]
        outs.append((jax.ShapeDtypeStruct((ntok, D_MODEL), _F32), pl.BlockSpec((t, D_MODEL), row)))
        h_args = list(h)
    else:
        body, h_specs, h_args = _in_proj_body, [pl.BlockSpec((t, D_MODEL), row)], [h]
    return pl.pallas_call(
        body,
        grid=(ntok // t,),
        in_specs=h_specs + [
            _const_spec((1, D_MODEL)),
            _const_spec((D_MODEL, _C_END)),
            _const_spec((_R_END, D_MODEL)),
            pl.BlockSpec((HEAD_DIM, t), lambda i: (0, i)),
            pl.BlockSpec((HEAD_DIM, t), lambda i: (0, i)),
            _const_spec((HEAD_DIM, 1)),
            _const_spec((HEAD_DIM, 1)),
            _const_spec((QK_ROWS - HEAD_DIM, 1)),
            _const_spec((QK_ROWS - HEAD_DIM, 1)),
        ],
        out_specs=tuple(o[1] for o in outs),
        out_shape=tuple(o[0] for o in outs),
        compiler_params=_params(("parallel",)),
        name="in_proj_first" if first else "in_proj",
    )(*h_args, g, w, wt, cost, sint, qg, kg, qx, kx)


_WINDOW_LOOKAHEAD = 3


def _stack_heads(q, j):
    base = j * GROUP * HEAD_DIM
    return jnp.concatenate([q[:, base + g * HEAD_DIM: base + (g + 1) * HEAD_DIM] for g in range(GROUP)], axis=0)


def _sink_rows(sink_ref, j, rows, lanes):
    return jnp.concatenate([jnp.full((rows, lanes), sink_ref[j * GROUP + g], _F32) for g in range(GROUP)], axis=0)


def _window_body(geom, sink_ref, q_ref, kmain_ref, kprev_ref, knext_ref, vmain_ref, vprev_ref, vnext_ref,
                 kmeta_ref, vmeta_ref, bias_ref, _tail_ref, o_ref):
    t1, tpb1, tpb2 = geom
    t = pl.program_id(0)
    in_prompt = t < t1
    is_first = jnp.where(in_prompt, t % tpb1 == 0, (t - t1) % tpb2 == 0)
    is_last = jnp.where(in_prompt, t % tpb1 == tpb1 - 1, (t - t1) % tpb2 == tpb2 - 1)
    nblk = WINDOW_TILE // BLOCK
    per_chunk = TOKEN_TILE // BLOCK

    def softmax_values(j, r, s, vwin, sink):
        m = jnp.maximum(jnp.max(s, axis=0, keepdims=True), sink)
        acc = _dot(vwin, jnp.exp2(s - m).astype(_BF))
        l = acc[HEAD_DIM:HEAD_DIM + 1] + jnp.exp2(sink - m)
        o = (acc / l).T.astype(_BF)
        for g in range(GROUP):
            c0 = (j * GROUP + g) * HEAD_DIM
            o_ref[r * BLOCK:(r + 1) * BLOCK, c0:c0 + HEAD_DIM] = o[g * BLOCK:(g + 1) * BLOCK, :HEAD_DIM]

    pending = []
    for j in range(N_KV):
        chunks = range(kmain_ref.shape[1])
        kcat = jnp.concatenate([kprev_ref[j, 0]] + [kmain_ref[j, cc] for cc in chunks] + [knext_ref[j, 0]], axis=1)
        vcat = jnp.concatenate([vprev_ref[j, 0]] + [vmain_ref[j, cc] for cc in chunks] + [vnext_ref[j, 0]], axis=1)
        kmeta = kmeta_ref[0, j]
        vmeta = vmeta_ref[0, j]
        sink = jnp.concatenate([jnp.full((1, BLOCK), sink_ref[j * GROUP + g], _F32) for g in range(GROUP)], axis=1)
        for r in range(nblk):
            variant = 0
            if r == 0:
                variant = jnp.where(is_first, 1, 0)
            if r == nblk - 1:
                variant = jnp.where(is_last, 2, variant)
            c0 = (r % per_chunk) * BLOCK
            qt = jnp.concatenate([q_ref[j * GROUP + g, r // per_chunk][:, c0:c0 + BLOCK] for g in range(GROUP)],
                                 axis=1)
            kwin = jnp.concatenate([kcat[:, r * BLOCK:(r + 3) * BLOCK], kmeta], axis=1)
            vwin = jnp.concatenate([vcat[:, r * BLOCK:(r + 3) * BLOCK], vmeta], axis=1)
            s = _dot_tn(kwin, qt) + bias_ref[variant, j]
            pending.append((j, r, s, vwin, sink))
            if len(pending) > _WINDOW_LOOKAHEAD:
                softmax_values(*pending.pop(0))
    for item in pending:
        softmax_values(*item)


def _window_attention(sink, qat, kat, vat, kmeta, vmetat, bias, tail, layout):
    n_real, b1, n1, b2, n2 = layout
    t = WINDOW_TILE
    per = t // TOKEN_TILE
    nch = kat.shape[1]
    assert n1 % t == 0 and n2 % t == 0
    t1 = b1 * n1 // t
    tpb1, tpb2 = n1 // t, n2 // t
    sub = TOKEN_TILE // BLOCK

    def bid(i):
        return jnp.where(i < t1, i // tpb1, b1 + (i - t1) // tpb2)

    main = lambda i: (0, i, 0, 0)
    prev = lambda i: (0, jnp.maximum(i * per - 1, 0), 0, sub - 1)
    nxt = lambda i: (0, jnp.minimum((i + 1) * per, nch - 1), 0, 0)
    return pl.pallas_call(
        functools.partial(_window_body, (t1, tpb1, tpb2)),
        grid=(n_real // t,),
        in_specs=[
            pl.BlockSpec(memory_space=pltpu.SMEM),
            pl.BlockSpec((N_HEADS, per, HEAD_DIM, TOKEN_TILE), main),
            pl.BlockSpec((N_KV, per, HEAD_DIM, TOKEN_TILE), main),
            pl.BlockSpec((N_KV, 1, HEAD_DIM, BLOCK), prev),
            pl.BlockSpec((N_KV, 1, HEAD_DIM, BLOCK), nxt),
            pl.BlockSpec((N_KV, per, V_LANES, TOKEN_TILE), main),
            pl.BlockSpec((N_KV, 1, V_LANES, BLOCK), prev),
            pl.BlockSpec((N_KV, 1, V_LANES, BLOCK), nxt),
            pl.BlockSpec((1, N_KV, HEAD_DIM, BLOCK), lambda i: (bid(i), 0, 0, 0)),
            pl.BlockSpec((1, N_KV, V_LANES, BLOCK), lambda i: (bid(i), 0, 0, 0)),
            _const_spec((3, N_KV, 4 * BLOCK, GROUP * BLOCK)),
            pl.BlockSpec(memory_space=pl.ANY),
        ],
        out_specs=pl.BlockSpec((t, Q_COLS), lambda i: (i, 0)),
        out_shape=jax.ShapeDtypeStruct(tail.shape, _BF),
        input_output_aliases={11: 0},
        compiler_params=_params(("parallel",)),
        name="window_attn",
    )(sink, qat, kat, kat, kat, vat, vat, vat, kmeta, vmetat, bias, tail)


def _window_meta_body(nb, sink_ref, q_ref, kfirst_ref, vfirst_ref, kmeta_ref, vmeta_ref, o_ref):
    b = pl.program_id(0)

    @pl.when(b < nb)
    def _():
        q_all = q_ref[...]
        for j in range(N_KV):
            qs = _stack_heads(q_all, j)
            sink = _sink_rows(sink_ref, j, N_META, 1)
            sm = _dot(qs, kmeta_ref[0, j])
            sf = _dot(qs, kfirst_ref[j, 0])
            m = jnp.maximum(jnp.maximum(sm.max(axis=1, keepdims=True), sf.max(axis=1, keepdims=True)), sink)
            pm = jnp.exp2(sm - m).astype(_BF)
            pf = jnp.exp2(sf - m).astype(_BF)
            acc = _dot(pm, vmeta_ref[0, j]) + _dot_nt(pf, vfirst_ref[j, 0])
            l = acc[:, HEAD_DIM:HEAD_DIM + 1] + jnp.exp2(sink - m)
            o = (acc[:, :HEAD_DIM] / l).astype(_BF)
            for g in range(GROUP):
                c0 = (j * GROUP + g) * HEAD_DIM
                o_ref[:, c0:c0 + HEAD_DIM] = o[g * N_META:(g + 1) * N_META]

    @pl.when(b >= nb)
    def _():
        o_ref[...] = jnp.zeros(o_ref.shape, o_ref.dtype)


def _window_meta_attention(sink, q_rows, kat, vat, kmeta, vmeta, layout, out_rows):
    n_real, b1, n1, b2, n2 = layout
    nb = b1 + b2
    t = TOKEN_TILE

    def start_chunk(b):
        bc = jnp.minimum(b, nb - 1)
        return jnp.where(bc < b1, bc * (n1 // t), b1 * (n1 // t) + (bc - b1) * (n2 // t))

    return pl.pallas_call(
        functools.partial(_window_meta_body, nb),
        grid=((out_rows - n_real) // N_META,),
        in_specs=[
            pl.BlockSpec(memory_space=pltpu.SMEM),
            pl.BlockSpec((N_META, Q_COLS), lambda b: (b, 0)),
            pl.BlockSpec((N_KV, 1, HEAD_DIM, BLOCK), lambda b: (0, start_chunk(b), 0, 0)),
            pl.BlockSpec((N_KV, 1, V_LANES, BLOCK), lambda b: (0, start_chunk(b), 0, 0)),
            pl.BlockSpec((1, N_KV, HEAD_DIM, N_META), lambda b: (jnp.minimum(b, nb - 1), 0, 0, 0)),
            pl.BlockSpec((1, N_KV, N_META, V_LANES), lambda b: (jnp.minimum(b, nb - 1), 0, 0, 0)),
        ],
        out_specs=pl.BlockSpec((N_META, Q_COLS), lambda b: (n_real // N_META + b, 0)),
        out_shape=jax.ShapeDtypeStruct((out_rows, Q_COLS), _BF),
        compiler_params=_params(("arbitrary",)),
        name="window_meta_attn",
    )(sink, q_rows, kat, vat, kmeta, vmeta)


_KEY_UNROLL = 4
_QUERY_BLOCK = 256
_SCORE_LOOKAHEAD = 3


def _global_body(nvalid, nchunks, tq, q_ref, kt_ref, vt_ref, kmeta_ref, vmetat_ref, _prev_ref, o_ref,
                 qt_ref, s0_ref, s1_ref, m_ref, acc_ref):
    b = pl.program_id(0)

    cols = qt_ref.shape[1]
    blocks = [slice(c0, min(c0 + _QUERY_BLOCK, cols)) for c0 in range(0, cols, _QUERY_BLOCK)]

    def scores(c, s_ref):
        for sl in blocks:
            s_ref[:, sl] = _dot_tn(kt_ref[0, c], qt_ref[:, sl])

    def softmax_pv(c, s_ref):
        for sl in blocks:
            s = s_ref[:, sl]
            m_prev = m_ref[:, sl]
            m_new = jnp.maximum(m_prev, jnp.max(s, axis=0, keepdims=True))
            p = jnp.exp2(s - m_new).astype(_BF)
            acc_ref[:, sl] = jnp.exp2(m_prev - m_new) * acc_ref[:, sl] + _dot(vt_ref[0, c], p)
            m_ref[:, sl] = m_new

    @pl.when(b < nvalid)
    def _():
        _stack_queries(q_ref, qt_ref)
        sm = _dot(kmeta_ref[0, 0], qt_ref[...])
        m0 = jnp.max(sm, axis=0, keepdims=True)
        m_ref[...] = m0
        acc_ref[...] = _dot(vmetat_ref[0, 0], jnp.exp2(sm - m0).astype(_BF))
        bufs = (s0_ref, s1_ref)
        scores(0, s0_ref)

        unroll = _KEY_UNROLL if nchunks > _KEY_UNROLL else 2

        def group(i, carry):
            c0 = unroll * i
            for u in range(unroll):
                scores(c0 + u + 1, bufs[(u + 1) % 2])
                softmax_pv(c0 + u, bufs[u % 2])
            return carry

        full = (nchunks - 1) // unroll
        lax.fori_loop(0, full, group, 0)
        for c in range(full * unroll, nchunks):
            if c + 1 < nchunks:
                scores(c + 1, bufs[(c + 1) % 2])
            softmax_pv(c, bufs[c % 2])
        _write_output(acc_ref, o_ref, tq)

    @pl.when(b >= nvalid)
    def _():
        o_ref[...] = jnp.zeros(o_ref.shape, o_ref.dtype)


def _global_fast_body(nvalid, nchunks, tq, q_ref, kt_ref, vt_ref, kmeta_ref, vmetat_ref, _prev_ref, o_ref,
                      qt_ref, acc_ref):
    b = pl.program_id(0)
    cols = qt_ref.shape[1]
    blocks = [slice(c0, min(c0 + _QUERY_BLOCK, cols)) for c0 in range(0, cols, _QUERY_BLOCK)]

    def chunks(cs):
        pending = []
        for c in cs:
            for sl in blocks:
                pending.append((c, sl, _dot_tn(kt_ref[0, c], qt_ref[:, sl])))
                if len(pending) > _SCORE_LOOKAHEAD:
                    values(*pending.pop(0))
        for item in pending:
            values(*item)

    def values(c, sl, s):
        acc_ref[:, sl] += _dot(vt_ref[0, c], jnp.exp2(s).astype(_BF))

    @pl.when(b < nvalid)
    def _():
        _stack_queries(q_ref, qt_ref)
        acc_ref[...] = _dot(vmetat_ref[0, 0], jnp.exp2(_dot(kmeta_ref[0, 0], qt_ref[...])).astype(_BF))

        def group(i, carry):
            chunks([_KEY_UNROLL * i + u for u in range(_KEY_UNROLL)])
            return carry

        full = nchunks // _KEY_UNROLL
        lax.fori_loop(0, full, group, 0)
        if full * _KEY_UNROLL < nchunks:
            chunks(range(full * _KEY_UNROLL, nchunks))
        _write_output(acc_ref, o_ref, tq)

    @pl.when(b >= nvalid)
    def _():
        o_ref[...] = jnp.zeros(o_ref.shape, o_ref.dtype)


def _stack_queries(q_ref, qt_ref):
    if q_ref.shape[0] == GROUP:
        qt_ref[...] = jnp.concatenate([q_ref[g, cc] for g in range(GROUP) for cc in range(q_ref.shape[1])], axis=1)
    else:
        qt_ref[...] = q_ref[0, 0]


def _write_output(acc_ref, o_ref, tq):
    acc = acc_ref[...]
    o = (acc / acc[HEAD_DIM:HEAD_DIM + 1]).T.astype(_BF)
    for g in range(GROUP):
        o_ref[:, g * HEAD_DIM:(g + 1) * HEAD_DIM] = o[g * tq:(g + 1) * tq, :HEAD_DIM]


def _global_attention(q, kbt, vbt, kmeta, vmetat, prev, out_rows, *, tq, q_row0, q_rows_per_batch, grid_batches,
                      valid_batches, kv_batch0, kv_row0, n, offset_is_bound=False):
    t = TOKEN_TILE
    nchunks = n // t
    qt = q_rows_per_batch // tq
    assert q_row0 % tq == 0 and kv_row0 % n == 0 and q_rows_per_batch % tq == 0
    kvb = lambda b: jnp.minimum(b, valid_batches - 1)
    if tq == N_META:
        cols = V_LANES
        q_spec = pl.BlockSpec((1, 1, QK_ROWS, cols), lambda b, j, i: (kv_batch0 + kvb(b), j, 0, 0))
    else:
        cols = GROUP * tq
        assert tq % t == 0
        q_spec = pl.BlockSpec((GROUP, tq // t, QK_ROWS, t),
                              lambda b, j, i: (j, (q_row0 + b * q_rows_per_batch) // tq + i, 0, 0))
    in_specs = [
        q_spec,
        pl.BlockSpec((1, nchunks, QK_ROWS, t), lambda b, j, i: (j, kv_row0 // n + kvb(b), 0, 0)),
        pl.BlockSpec((1, nchunks, V_LANES, t), lambda b, j, i: (j, kv_row0 // n + kvb(b), 0, 0)),
        pl.BlockSpec((1, 1, N_META, QK_ROWS), lambda b, j, i: (kv_batch0 + kvb(b), j, 0, 0)),
        pl.BlockSpec((1, 1, V_LANES, N_META), lambda b, j, i: (kv_batch0 + kvb(b), j, 0, 0)),
    ]
    args = [q, kbt, vbt, kmeta, vmetat]
    aliases = {}
    body = functools.partial(_global_fast_body if offset_is_bound else _global_body, valid_batches, nchunks, tq)
    score_bufs = [] if offset_is_bound else [pltpu.VMEM((t, cols), _F32), pltpu.VMEM((t, cols), _F32),
                                             pltpu.VMEM((1, cols), _F32)]
    if prev is None:
        body = functools.partial(_global_body_noprev, body)
    else:
        in_specs.append(pl.BlockSpec(memory_space=pl.ANY))
        args.append(prev)
        aliases = {5: 0}
    return pl.pallas_call(
        body,
        grid=(grid_batches, N_KV, qt),
        in_specs=in_specs,
        out_specs=pl.BlockSpec((tq, GROUP * HEAD_DIM), lambda b, j, i: (q_row0 // tq + b * qt + i, j)),
        out_shape=jax.ShapeDtypeStruct((out_rows, Q_COLS), _BF),
        scratch_shapes=[pltpu.VMEM((QK_ROWS, cols), _BF)] + score_bufs + [pltpu.VMEM((V_LANES, cols), _F32)],
        input_output_aliases=aliases,
        compiler_params=_params(("parallel", "parallel", "arbitrary")),
        name=f"global_attn{'_fast' if offset_is_bound else ''}_tq{tq}_n{n}",
    )(*args)


def _global_body_noprev(body, q_ref, kt_ref, vt_ref, kmeta_ref, vmetat_ref, o_ref, *scratch):
    body(q_ref, kt_ref, vt_ref, kmeta_ref, vmetat_ref, None, o_ref, *scratch)


def _mix_body(h_ref, oa_ref, ob_ref, ga_ref, gb_ref, wa_ref, wb_ref, wo_ref, g_ref, o_ref):
    mix = (ga_ref[...].astype(_F32) * _dot(oa_ref[...], wa_ref[...])
           + gb_ref[...].astype(_F32) * _dot(ob_ref[...], wb_ref[...]))
    u = _dot(mix.astype(_BF), wo_ref[...])
    ms = jnp.mean(u * u, axis=-1, keepdims=True)
    o_ref[...] = h_ref[...] + u * lax.rsqrt(ms + EPS) * g_ref[...]


def _mix(h, oa, ob, ga, gb, wa, wb, wo, g):
    ntok = h.shape[0]
    t = DENSE_TILE
    row = lambda i: (i, 0)
    return pl.pallas_call(
        _mix_body,
        grid=(ntok // t,),
        in_specs=[
            pl.BlockSpec((t, D_MODEL), row),
            pl.BlockSpec((t, Q_COLS), row), pl.BlockSpec((t, Q_COLS), row),
            pl.BlockSpec((t, D_MODEL), row), pl.BlockSpec((t, D_MODEL), row),
            _const_spec((Q_COLS, D_MODEL)), _const_spec((Q_COLS, D_MODEL)),
            _const_spec((D_MODEL, D_MODEL)), _const_spec((1, D_MODEL)),
        ],
        out_specs=pl.BlockSpec((t, D_MODEL), row),
        out_shape=jax.ShapeDtypeStruct(h.shape, _F32),
        input_output_aliases={0: 0},
        compiler_params=_params(("parallel",)),
        name="branch_mix",
    )(h, oa, ob, ga, gb, wa, wb, wo, g)


_FF_CHUNK = 256


def _ffn_body(h_ref, gpre_ref, wup_ref, wdown_ref, gpost_ref, o_ref, act_ref):
    h = h_ref[...]
    ms = jnp.mean(h * h, axis=-1, keepdims=True)
    xn = (h * lax.rsqrt(ms + EPS) * gpre_ref[...]).astype(_BF)
    for c in range(0, D_FF, _FF_CHUNK):
        a = _dot(xn, wup_ref[:, c:c + _FF_CHUNK])
        b = _dot(xn, wup_ref[:, D_FF + c:D_FF + c + _FF_CHUNK])
        act_ref[:, c:c + _FF_CHUNK] = (a * jax.nn.sigmoid(a) * b).astype(_BF)
    u = _dot(act_ref[...], wdown_ref[...])
    ms = jnp.mean(u * u, axis=-1, keepdims=True)
    o_ref[...] = h + u * lax.rsqrt(ms + EPS) * gpost_ref[...]


def _ffn_last_body(blocks, h_ref, gpre_ref, wup_ref, wdown_ref, gpost_ref, yp_ref, ys_ref, act_ref):
    p, sm = blocks
    i = pl.program_id(0)

    @pl.when(i < p)
    def _():
        _ffn_body(h_ref, gpre_ref, wup_ref, wdown_ref, gpost_ref, yp_ref, act_ref)

    @pl.when(jnp.logical_and(i >= p, i < p + sm))
    def _():
        _ffn_body(h_ref, gpre_ref, wup_ref, wdown_ref, gpost_ref, ys_ref, act_ref)


def _ffn(h, gpre, wup, wdown, gpost, split_rows=None):
    ntok = h.shape[0]
    t = DENSE_TILE
    row = lambda i: (i, 0)
    if split_rows is None:
        body, aliases = _ffn_body, {0: 0}
        out_specs = pl.BlockSpec((t, D_MODEL), row)
        out_shape = jax.ShapeDtypeStruct(h.shape, _F32)
    else:
        p, sm = (r // t for r in split_rows)
        assert p * t == split_rows[0] and sm * t == split_rows[1]
        body, aliases = functools.partial(_ffn_last_body, (p, sm)), {}
        out_specs = (pl.BlockSpec((t, D_MODEL), lambda i: (jnp.minimum(i, p - 1), 0)),
                     pl.BlockSpec((t, D_MODEL), lambda i: (jnp.clip(i - p, 0, sm - 1), 0)))
        out_shape = (jax.ShapeDtypeStruct((split_rows[0], D_MODEL), _F32),
                     jax.ShapeDtypeStruct((split_rows[1], D_MODEL), _F32))
    return pl.pallas_call(
        body,
        grid=(ntok // t,),
        in_specs=[
            pl.BlockSpec((t, D_MODEL), row),
            _const_spec((1, D_MODEL)),
            _const_spec((D_MODEL, 2 * D_FF)),
            _const_spec((D_FF, D_MODEL)),
            _const_spec((1, D_MODEL)),
        ],
        out_specs=out_specs,
        out_shape=out_shape,
        scratch_shapes=[pltpu.VMEM((t, D_FF), _BF)],
        input_output_aliases=aliases,
        compiler_params=_params(("arbitrary",)),
        name="swiglu_ffn" if split_rows is None else "swiglu_ffn_last",
    )(h, gpre, wup, wdown, gpost)


def _rope_tables(layout, ntok):
    n_real, b1, n1, b2, n2 = layout
    idx = np.zeros((ntok,), np.int64)
    idx[:b1 * n1] = np.arange(b1 * n1) % n1
    idx[b1 * n1:n_real] = np.arange(b2 * n2) % n2
    rows = (idx // GRID_W).astype(np.float32)
    cols = (idx % GRID_W).astype(np.float32)
    rows[n_real:] = 0.0
    cols[n_real:] = 0.0
    freqs = ROPE_BASE ** (-jnp.arange(ROPE_FREQS, dtype=_F32) / ROPE_FREQS)
    ang_r = jnp.asarray(rows)[:, None] * freqs[None, :]
    ang_c = jnp.asarray(cols)[:, None] * freqs[None, :]
    cr, sr, cc, sc = jnp.cos(ang_r), jnp.sin(ang_r), jnp.cos(ang_c), jnp.sin(ang_c)
    cos64 = jnp.concatenate([cr, cr, cc, cc], axis=1)
    sin64 = jnp.concatenate([-sr, sr, -sc, sc], axis=1)
    return cos64.T, sin64.T


def _window_bias():
    slopes = 2.0 ** (-8.0 * np.arange(1, N_HEADS + 1, dtype=np.float64) / N_HEADS)
    rel = np.arange(BLOCK)[:, None] - (np.arange(3 * BLOCK) - BLOCK)[None, :]
    dist = np.abs(rel)
    band = dist <= BLOCK
    local = np.where(band[None], -slopes[:, None, None] * dist[None].astype(np.float64) * LOG2E, NEG_INF)
    extra = np.full((N_HEADS, BLOCK, BLOCK), NEG_INF)
    extra[:, :, :N_META] = 0.0
    base = np.concatenate([local, extra], axis=2)
    no_prev = base.copy()
    no_prev[:, :, :BLOCK] = NEG_INF
    no_next = base.copy()
    no_next[:, :, 2 * BLOCK:3 * BLOCK] = NEG_INF
    out = np.stack([base, no_prev, no_next]).reshape(3, N_KV, GROUP * BLOCK, 4 * BLOCK)
    return jnp.asarray(np.swapaxes(out, -1, -2), _F32)


def _rearranged_w_in(w_in):
    o = np.cumsum([0, Q_COLS, KV_COLS, KV_COLS, Q_COLS, KV_COLS, KV_COLS, D_MODEL, D_MODEL])
    qa, ka, va, qb, kb, vb, ga, gb = [w_in[..., o[i]:o[i + 1]] for i in range(8)]

    def pad_v(v):
        z = jnp.zeros(v.shape[:-1] + (V_LANES - HEAD_DIM,), v.dtype)
        return jnp.concatenate([v[..., :HEAD_DIM], z, v[..., HEAD_DIM:], z], axis=-1)

    w_rows = jnp.concatenate([ga, gb], axis=-1).astype(_BF)
    w_t = jnp.swapaxes(jnp.concatenate([ka, kb, pad_v(vb), qb, pad_v(va), qa], axis=-1), -1, -2).astype(_BF)
    return w_rows, w_t


def _tail_features(xt, n_real, nb):
    tail = xt[:, n_real // TOKEN_TILE:]
    tail = jnp.moveaxis(tail, 2, 1).reshape(xt.shape[0], xt.shape[2], -1)[:, :, :nb * N_META]
    return tail.reshape(xt.shape[0], xt.shape[2], nb, N_META)


def kernel(x_prompt, x_sample, meta_tokens, g_mix_pre, g_mix_post, g_ffn_pre, g_ffn_post, w_in, q_norm_b,
           k_norm_b, sink_a, w_branch_a, w_branch_b, w_out, w_ffn_up, w_ffn_down):
    b1, n1, _ = x_prompt.shape
    b2, n2, _ = x_sample.shape
    depth = w_in.shape[0]
    t = TOKEN_TILE
    assert n1 % t == 0 and n2 % t == 0 and (b1 * n1) % n2 == 0 and n1 % GRID_W == 0 and n2 % GRID_W == 0
    assert (b1 * n1) % DENSE_TILE == 0 and (b2 * n2) % DENSE_TILE == 0
    nb = b1 + b2
    n_real = b1 * n1 + b2 * n2
    ntok = -(-(n_real + nb * N_META) // DENSE_TILE) * DENSE_TILE
    tail = ntok - n_real
    assert tail % t == 0
    layout = (n_real, b1, n1, b2, n2)

    h = (x_prompt.reshape(b1 * n1, D_MODEL).astype(_F32), x_sample.reshape(b2 * n2, D_MODEL).astype(_F32),
         jnp.concatenate([jnp.tile(meta_tokens.astype(_F32), (nb, 1)),
                          jnp.zeros((tail - nb * N_META, D_MODEL), _F32)], axis=0))

    cost, sint = _rope_tables(layout, ntok)
    bias = _window_bias()
    w_rows, w_t = _rearranged_w_in(w_in)
    wa, wb, wo = w_branch_a.astype(_BF), w_branch_b.astype(_BF), w_out.astype(_BF)
    wup, wdown = w_ffn_up.astype(_BF), w_ffn_down.astype(_BF)
    row = lambda g: g.reshape(1, -1).astype(_F32)
    col = lambda g: g.reshape(-1, 1).astype(_F32)

    for l in range(depth):
        bound = (HEAD_DIM * SCALE * LOG2E * 1.02) * jnp.max(jnp.abs(q_norm_b[l])) * jnp.max(jnp.abs(k_norm_b[l]))
        unit = (jnp.arange(QK_ROWS - HEAD_DIM) == 0).astype(_F32).reshape(-1, 1)
        proj = _in_proj(h, row(g_mix_pre[l]), w_rows[l], w_t[l], cost, sint, col(q_norm_b[l]), col(k_norm_b[l]),
                        -bound.astype(_F32) * unit, unit)
        if l == 0:
            *proj, h = proj
        qat, qbt, kat, kbt, vat, vbt, ga, gb = proj
        sink = sink_a[l].astype(_F32) * LOG2E
        ka_tail, va_tail = _tail_features(kat, n_real, nb), _tail_features(vat, n_real, nb)
        kameta = jnp.transpose(ka_tail, (2, 0, 1, 3))
        vameta = jnp.transpose(va_tail, (2, 0, 3, 1))
        lane_pad = ((0, 0), (0, 0), (0, 0), (0, BLOCK - N_META))
        kameta_pad = jnp.pad(kameta, lane_pad)
        vametat_pad = jnp.pad(jnp.transpose(va_tail, (2, 0, 1, 3)), lane_pad)
        qa_meta = jnp.transpose(_tail_features(qat, n_real, nb), (2, 3, 0, 1)).reshape(nb * N_META, Q_COLS)
        qa_meta = jnp.pad(qa_meta, ((0, tail - nb * N_META), (0, 0)))

        oa = _window_meta_attention(sink, qa_meta, kat, vat, kameta, vameta, layout, ntok)
        oa = _window_attention(sink, qat, kat, vat, kameta_pad, vametat_pad, bias, oa, layout)

        kbmeta = jnp.transpose(_tail_features(kbt, n_real, nb), (2, 0, 3, 1))
        vbmetat = jnp.transpose(_tail_features(vbt, n_real, nb), (2, 0, 1, 3))
        qmeta = _tail_features(qbt, n_real, nb).reshape(N_KV, GROUP, QK_ROWS, nb, N_META)
        qmeta = jnp.transpose(qmeta, (3, 0, 2, 1, 4)).reshape(nb, N_KV, QK_ROWS, GROUP * N_META)
        qmeta = jnp.pad(qmeta, ((0, 0), (0, 0), (0, 0), (0, V_LANES - GROUP * N_META)))

        glob = functools.partial(_global_attention, kbt=kbt, vbt=vbt, kmeta=kbmeta, vmetat=vbmetat, out_rows=ntok)
        ob = glob(qmeta, prev=None, tq=N_META, q_row0=n_real, q_rows_per_batch=N_META, grid_batches=b1,
                  valid_batches=b1, kv_batch0=0, kv_row0=0, n=n1)
        ob = glob(qmeta, prev=ob, tq=N_META, q_row0=n_real + b1 * N_META, q_rows_per_batch=N_META,
                  grid_batches=tail // N_META - b1, valid_batches=b2, kv_batch0=b1, kv_row0=b1 * n1, n=n2)

        def real_queries(prev, offset_is_bound):
            tq_real = 1024 if offset_is_bound else 512
            out = glob(qbt, prev=prev, tq=tq_real, q_row0=0, q_rows_per_batch=n1, grid_batches=b1, valid_batches=b1,
                       kv_batch0=0, kv_row0=0, n=n1, offset_is_bound=offset_is_bound)
            return glob(qbt, prev=out, tq=tq_real, q_row0=b1 * n1, q_rows_per_batch=n2, grid_batches=b2,
                        valid_batches=b2, kv_batch0=b1, kv_row0=b1 * n1, n=n2, offset_is_bound=offset_is_bound)

        ob = lax.cond(bound <= SAFE_OFFSET_MAX, functools.partial(real_queries, offset_is_bound=True),
                      functools.partial(real_queries, offset_is_bound=False), ob)

        h = _mix(h, oa, ob, ga, gb, wa[l], wb[l], wo[l], row(g_mix_post[l]))
        h = _ffn(h, row(g_ffn_pre[l]), wup[l], wdown[l], row(g_ffn_post[l]),
                 split_rows=(b1 * n1, b2 * n2) if l == depth - 1 else None)

    y_prompt, y_sample = h
    return (y_prompt.reshape(b1, n1, D_MODEL).astype(x_prompt.dtype),
            y_sample.reshape(b2, n2, D_MODEL).astype(x_sample.dtype))
```

```python
import functools
import math

import jax
import jax.numpy as jnp
import numpy as np
from jax import lax
from jax.experimental import pallas as pl
from jax.experimental.pallas import tpu as pltpu

D_MODEL = 1024
HEAD_DIM = 64
N_HEADS = 8
N_KV = 2
GROUP = N_HEADS // N_KV
Q_COLS = N_HEADS * HEAD_DIM
KV_COLS = N_KV * HEAD_DIM
N_META = 16
BLOCK = 128
GRID_W = 64
ROPE_BASE = 10000.0
ROPE_FREQS = HEAD_DIM // 4
D_FF = 2816
EPS = 1e-6
NEG_INF = -1e30
SCALE = HEAD_DIM ** -0.5
LOG2E = math.log2(math.e)

QK_ROWS = HEAD_DIM + 16
SAFE_OFFSET_MAX = 40.0
V_LANES = 128
TOKEN_TILE = 512
DENSE_TILE = 1024
WINDOW_TILE = 1024
VMEM_LIMIT = 56 * 1024 * 1024

_C_GA = 0
_C_GB = _C_GA + D_MODEL
_C_END = _C_GB + D_MODEL
_R_KA = 0
_R_KB = _R_KA + KV_COLS
_R_VB = _R_KB + KV_COLS
_R_QB = _R_VB + N_KV * V_LANES
_R_VA = _R_QB + Q_COLS
_R_QA = _R_VA + N_KV * V_LANES
_R_END = _R_QA + Q_COLS

_BF = jnp.bfloat16
_F32 = jnp.float32


def _dot(a, b):
    return jnp.dot(a, b, preferred_element_type=_F32)


def _dot_nt(a, b):
    return lax.dot_general(a, b, (((1,), (1,)), ((), ())), preferred_element_type=_F32)


def _dot_tn(a, b):
    return lax.dot_general(a, b, (((0,), (0,)), ((), ())), preferred_element_type=_F32)


def _params(sem, vmem=VMEM_LIMIT):
    return pltpu.CompilerParams(dimension_semantics=sem, vmem_limit_bytes=vmem)


def _const_spec(shape):
    nd = len(shape)
    return pl.BlockSpec(shape, lambda *_: (0,) * nd, pipeline_mode=pl.Buffered(1))


def _in_proj_body(h_ref, g_ref, w_ref, wt_ref, cost_ref, sint_ref, qg_ref, kg_ref, qx_ref, kx_ref,
                  qat_ref, qbt_ref, kat_ref, kbt_ref, vat_ref, vbt_ref, ga_ref, gb_ref):
    f = ROPE_FREQS
    ones_row = (lax.broadcasted_iota(jnp.int32, (V_LANES, 1), 0) == HEAD_DIM).astype(_F32)
    q_extra = jnp.broadcast_to(qx_ref[...], (QK_ROWS - HEAD_DIM, TOKEN_TILE))
    k_extra = jnp.broadcast_to(kx_ref[...], (QK_ROWS - HEAD_DIM, TOKEN_TILE))

    for cc in range(h_ref.shape[0] // TOKEN_TILE):
        rows = slice(cc * TOKEN_TILE, (cc + 1) * TOKEN_TILE)
        h = h_ref[rows, :]
        ms = jnp.mean(h * h, axis=-1, keepdims=True)
        xn = (h * lax.rsqrt(ms + EPS) * g_ref[...]).astype(_BF)

        ga_ref[rows, :] = jax.nn.sigmoid(_dot(xn, w_ref[:, _C_GA:_C_GA + D_MODEL])).astype(_BF)
        gb_ref[rows, :] = jax.nn.sigmoid(_dot(xn, w_ref[:, _C_GB:_C_GB + D_MODEL])).astype(_BF)

        tt = _dot_nt(wt_ref[...], xn)
        cost = cost_ref[:, rows]
        sint = sint_ref[:, rows]

        def norm_rope(x, gain):
            x = x * lax.rsqrt(jnp.mean(x * x, axis=0, keepdims=True) + EPS) * gain
            partner = jnp.concatenate([x[f:2 * f], x[0:f], x[3 * f:4 * f], x[2 * f:3 * f]], axis=0)
            return x * cost + partner * sint

        for j in range(N_KV):
            kat_ref[j, cc] = tt[_R_KA + j * HEAD_DIM:_R_KA + (j + 1) * HEAD_DIM].astype(_BF)
            k = norm_rope(tt[_R_KB + j * HEAD_DIM:_R_KB + (j + 1) * HEAD_DIM], kg_ref[...])
            kbt_ref[j, cc] = jnp.concatenate([k, k_extra], axis=0).astype(_BF)
            vat_ref[j, cc] = (tt[_R_VA + j * V_LANES:_R_VA + (j + 1) * V_LANES] + ones_row).astype(_BF)
            vbt_ref[j, cc] = (tt[_R_VB + j * V_LANES:_R_VB + (j + 1) * V_LANES] + ones_row).astype(_BF)
        for hd in range(N_HEADS):
            qa = tt[_R_QA + hd * HEAD_DIM:_R_QA + (hd + 1) * HEAD_DIM] * (SCALE * LOG2E)
            qat_ref[hd, cc] = qa.astype(_BF)
            q = norm_rope(tt[_R_QB + hd * HEAD_DIM:_R_QB + (hd + 1) * HEAD_DIM], qg_ref[...])
            qbt_ref[hd, cc] = jnp.concatenate([q * (SCALE * LOG2E), q_extra], axis=0).astype(_BF)


def _in_proj_first_body(blocks, xp_ref, xs_ref, tail_ref, *refs):
    p, sm = blocks
    i = pl.program_id(0)
    *rest, h_ref = refs
    h_ref[...] = jnp.where(i < p, xp_ref[...], jnp.where(i < p + sm, xs_ref[...], tail_ref[...]))
    _in_proj_body(h_ref, *rest)


def _in_proj(h, g, w, wt, cost, sint, qg, kg, qx, kx):
    first = isinstance(h, tuple)
    t = TOKEN_TILE if first else DENSE_TILE
    ntok = sum(x.shape[0] for x in h) if first else h.shape[0]
    per = t // TOKEN_TILE
    nch = ntok // TOKEN_TILE
    row = lambda i: (i, 0)
    chunk = lambda i: (0, i, 0, 0)
    feat = lambda heads, rows: (jax.ShapeDtypeStruct((heads, nch, rows, TOKEN_TILE), _BF),
                                pl.BlockSpec((heads, per, rows, TOKEN_TILE), chunk))
    outs = [
        feat(N_HEADS, HEAD_DIM),
        feat(N_HEADS, QK_ROWS),
        feat(N_KV, HEAD_DIM),
        feat(N_KV, QK_ROWS),
        feat(N_KV, V_LANES),
        feat(N_KV, V_LANES),
        (jax.ShapeDtypeStruct((ntok, D_MODEL), _BF), pl.BlockSpec((t, D_MODEL), row)),
        (jax.ShapeDtypeStruct((ntok, D_MODEL), _BF), pl.BlockSpec((t, D_MODEL), row)),
    ]
    if first:
        p, sm, tl = (x.shape[0] // t for x in h)
        body = functools.partial(_in_proj_first_body, (p, sm))
        h_specs = [pl.BlockSpec((t, D_MODEL), lambda i: (jnp.minimum(i, p - 1), 0)),
                   pl.BlockSpec((t, D_MODEL), lambda i: (jnp.clip(i - p, 0, sm - 1), 0)),
                   pl.BlockSpec((t, D_MODEL), lambda i: (jnp.clip(i - p - sm, 0, tl - 1), 0))]
        outs.append((jax.ShapeDtypeStruct((ntok, D_MODEL), _F32), pl.BlockSpec((t, D_MODEL), row)))
        h_args = list(h)
    else:
        body, h_specs, h_args = _in_proj_body, [pl.BlockSpec((t, D_MODEL), row)], [h]
    return pl.pallas_call(
        body,
        grid=(ntok // t,),
        in_specs=h_specs + [
            _const_spec((1, D_MODEL)),
            _const_spec((D_MODEL, _C_END)),
            _const_spec((_R_END, D_MODEL)),
            pl.BlockSpec((HEAD_DIM, t), lambda i: (0, i)),
            pl.BlockSpec((HEAD_DIM, t), lambda i: (0, i)),
            _const_spec((HEAD_DIM, 1)),
            _const_spec((HEAD_DIM, 1)),
            _const_spec((QK_ROWS - HEAD_DIM, 1)),
            _const_spec((QK_ROWS - HEAD_DIM, 1)),
        ],
        out_specs=tuple(o[1] for o in outs),
        out_shape=tuple(o[0] for o in outs),
        compiler_params=_params(("parallel",)),
        name="in_proj_first" if first else "in_proj",
    )(*h_args, g, w, wt, cost, sint, qg, kg, qx, kx)


_WINDOW_LOOKAHEAD = 3


def _stack_heads(q, j):
    base = j * GROUP * HEAD_DIM
    return jnp.concatenate([q[:, base + g * HEAD_DIM: base + (g + 1) * HEAD_DIM] for g in range(GROUP)], axis=0)


def _sink_rows(sink_ref, j, rows, lanes):
    return jnp.concatenate([jnp.full((rows, lanes), sink_ref[j * GROUP + g], _F32) for g in range(GROUP)], axis=0)


def _window_body(geom, sink_ref, q_ref, kmain_ref, kprev_ref, knext_ref, vmain_ref, vprev_ref, vnext_ref,
                 kmeta_ref, vmeta_ref, bias_ref, _tail_ref, o_ref):
    t1, tpb1, tpb2 = geom
    t = pl.program_id(0)
    in_prompt = t < t1
    is_first = jnp.where(in_prompt, t % tpb1 == 0, (t - t1) % tpb2 == 0)
    is_last = jnp.where(in_prompt, t % tpb1 == tpb1 - 1, (t - t1) % tpb2 == tpb2 - 1)
    nblk = WINDOW_TILE // BLOCK
    per_chunk = TOKEN_TILE // BLOCK

    def softmax_values(j, r, s, sm, vwin, vmeta, sink):
        m = jnp.maximum(jnp.maximum(jnp.max(s, axis=0, keepdims=True), jnp.max(sm, axis=0, keepdims=True)), sink)
        acc = (_dot(vwin, jnp.exp2(s - m).astype(_BF))
               + _dot(vmeta, jnp.exp2(sm - m).astype(_BF)))
        l = acc[HEAD_DIM:HEAD_DIM + 1] + jnp.exp2(sink - m)
        o = (acc / l).T.astype(_BF)
        for g in range(GROUP):
            c0 = (j * GROUP + g) * HEAD_DIM
            o_ref[r * BLOCK:(r + 1) * BLOCK, c0:c0 + HEAD_DIM] = o[g * BLOCK:(g + 1) * BLOCK, :HEAD_DIM]

    pending = []
    for j in range(N_KV):
        chunks = range(kmain_ref.shape[1])
        kcat = jnp.concatenate([kprev_ref[j, 0]] + [kmain_ref[j, cc] for cc in chunks] + [knext_ref[j, 0]], axis=1)
        vcat = jnp.concatenate([vprev_ref[j, 0]] + [vmain_ref[j, cc] for cc in chunks] + [vnext_ref[j, 0]], axis=1)
        kmeta = kmeta_ref[0, j]
        vmeta = vmeta_ref[0, j]
        sink = jnp.concatenate([jnp.full((1, BLOCK), sink_ref[j * GROUP + g], _F32) for g in range(GROUP)], axis=1)
        for r in range(nblk):
            variant = 0
            if r == 0:
                variant = jnp.where(is_first, 1, 0)
            if r == nblk - 1:
                variant = jnp.where(is_last, 2, variant)
            c0 = (r % per_chunk) * BLOCK
            qt = jnp.concatenate([q_ref[j * GROUP + g, r // per_chunk][:, c0:c0 + BLOCK] for g in range(GROUP)],
                                 axis=1)
            s = _dot_tn(kcat[:, r * BLOCK:(r + 3) * BLOCK], qt) + bias_ref[variant, j]
            sm = _dot_tn(kmeta, qt)
            pending.append((j, r, s, sm, vcat[:, r * BLOCK:(r + 3) * BLOCK], vmeta, sink))
            if len(pending) > _WINDOW_LOOKAHEAD:
                softmax_values(*pending.pop(0))
    for item in pending:
        softmax_values(*item)


def _window_attention(sink, qat, kat, vat, kmeta, vmetat, bias, tail, layout):
    n_real, b1, n1, b2, n2 = layout
    t = WINDOW_TILE
    per = t // TOKEN_TILE
    nch = kat.shape[1]
    assert n1 % t == 0 and n2 % t == 0
    t1 = b1 * n1 // t
    tpb1, tpb2 = n1 // t, n2 // t
    sub = TOKEN_TILE // BLOCK

    def bid(i):
        return jnp.where(i < t1, i // tpb1, b1 + (i - t1) // tpb2)

    main = lambda i: (0, i, 0, 0)
    prev = lambda i: (0, jnp.maximum(i * per - 1, 0), 0, sub - 1)
    nxt = lambda i: (0, jnp.minimum((i + 1) * per, nch - 1), 0, 0)
    return pl.pallas_call(
        functools.partial(_window_body, (t1, tpb1, tpb2)),
        grid=(n_real // t,),
        in_specs=[
            pl.BlockSpec(memory_space=pltpu.SMEM),
            pl.BlockSpec((N_HEADS, per, HEAD_DIM, TOKEN_TILE), main),
            pl.BlockSpec((N_KV, per, HEAD_DIM, TOKEN_TILE), main),
            pl.BlockSpec((N_KV, 1, HEAD_DIM, BLOCK), prev),
            pl.BlockSpec((N_KV, 1, HEAD_DIM, BLOCK), nxt),
            pl.BlockSpec((N_KV, per, V_LANES, TOKEN_TILE), main),
            pl.BlockSpec((N_KV, 1, V_LANES, BLOCK), prev),
            pl.BlockSpec((N_KV, 1, V_LANES, BLOCK), nxt),
            pl.BlockSpec((1, N_KV, HEAD_DIM, N_META), lambda i: (bid(i), 0, 0, 0)),
            pl.BlockSpec((1, N_KV, V_LANES, N_META), lambda i: (bid(i), 0, 0, 0)),
            _const_spec((3, N_KV, 3 * BLOCK, GROUP * BLOCK)),
            pl.BlockSpec(memory_space=pl.ANY),
        ],
        out_specs=pl.BlockSpec((t, Q_COLS), lambda i: (i, 0)),
        out_shape=jax.ShapeDtypeStruct(tail.shape, _BF),
        input_output_aliases={11: 0},
        compiler_params=_params(("parallel",)),
        name="window_attn",
    )(sink, qat, kat, kat, kat, vat, vat, vat, kmeta, vmetat, bias, tail)


def _window_meta_body(nb, sink_ref, q_ref, kfirst_ref, vfirst_ref, kmeta_ref, vmeta_ref, o_ref):
    b = pl.program_id(0)

    @pl.when(b < nb)
    def _():
        q_all = q_ref[...]
        for j in range(N_KV):
            qs = _stack_heads(q_all, j)
            sink = _sink_rows(sink_ref, j, N_META, 1)
            sm = _dot(qs, kmeta_ref[0, j])
            sf = _dot(qs, kfirst_ref[j, 0])
            m = jnp.maximum(jnp.maximum(sm.max(axis=1, keepdims=True), sf.max(axis=1, keepdims=True)), sink)
            pm = jnp.exp2(sm - m).astype(_BF)
            pf = jnp.exp2(sf - m).astype(_BF)
            acc = _dot(pm, vmeta_ref[0, j]) + _dot_nt(pf, vfirst_ref[j, 0])
            l = acc[:, HEAD_DIM:HEAD_DIM + 1] + jnp.exp2(sink - m)
            o = (acc[:, :HEAD_DIM] / l).astype(_BF)
            for g in range(GROUP):
                c0 = (j * GROUP + g) * HEAD_DIM
                o_ref[:, c0:c0 + HEAD_DIM] = o[g * N_META:(g + 1) * N_META]

    @pl.when(b >= nb)
    def _():
        o_ref[...] = jnp.zeros(o_ref.shape, o_ref.dtype)


def _window_meta_attention(sink, q_rows, kat, vat, kmeta, vmeta, layout, out_rows):
    n_real, b1, n1, b2, n2 = layout
    nb = b1 + b2
    t = TOKEN_TILE

    def start_chunk(b):
        bc = jnp.minimum(b, nb - 1)
        return jnp.where(bc < b1, bc * (n1 // t), b1 * (n1 // t) + (bc - b1) * (n2 // t))

    return pl.pallas_call(
        functools.partial(_window_meta_body, nb),
        grid=((out_rows - n_real) // N_META,),
        in_specs=[
            pl.BlockSpec(memory_space=pltpu.SMEM),
            pl.BlockSpec((N_META, Q_COLS), lambda b: (b, 0)),
            pl.BlockSpec((N_KV, 1, HEAD_DIM, BLOCK), lambda b: (0, start_chunk(b), 0, 0)),
            pl.BlockSpec((N_KV, 1, V_LANES, BLOCK), lambda b: (0, start_chunk(b), 0, 0)),
            pl.BlockSpec((1, N_KV, HEAD_DIM, N_META), lambda b: (jnp.minimum(b, nb - 1), 0, 0, 0)),
            pl.BlockSpec((1, N_KV, N_META, V_LANES), lambda b: (jnp.minimum(b, nb - 1), 0, 0, 0)),
        ],
        out_specs=pl.BlockSpec((N_META, Q_COLS), lambda b: (n_real // N_META + b, 0)),
        out_shape=jax.ShapeDtypeStruct((out_rows, Q_COLS), _BF),
        compiler_params=_params(("arbitrary",)),
        name="window_meta_attn",
    )(sink, q_rows, kat, vat, kmeta, vmeta)


_KEY_UNROLL = 4
_QUERY_BLOCK = 256
_SCORE_LOOKAHEAD = 3


def _global_body(nvalid, nchunks, tq, q_ref, kt_ref, vt_ref, kmeta_ref, vmetat_ref, _prev_ref, o_ref,
                 qt_ref, s0_ref, s1_ref, m_ref, acc_ref):
    b = pl.program_id(0)

    cols = qt_ref.shape[1]
    blocks = [slice(c0, min(c0 + _QUERY_BLOCK, cols)) for c0 in range(0, cols, _QUERY_BLOCK)]

    def scores(c, s_ref):
        for sl in blocks:
            s_ref[:, sl] = _dot_tn(kt_ref[0, c], qt_ref[:, sl])

    def softmax_pv(c, s_ref):
        for sl in blocks:
            s = s_ref[:, sl]
            m_prev = m_ref[:, sl]
            m_new = jnp.maximum(m_prev, jnp.max(s, axis=0, keepdims=True))
            p = jnp.exp2(s - m_new).astype(_BF)
            acc_ref[:, sl] = jnp.exp2(m_prev - m_new) * acc_ref[:, sl] + _dot(vt_ref[0, c], p)
            m_ref[:, sl] = m_new

    @pl.when(b < nvalid)
    def _():
        _stack_queries(q_ref, qt_ref)
        sm = _dot(kmeta_ref[0, 0], qt_ref[...])
        m0 = jnp.max(sm, axis=0, keepdims=True)
        m_ref[...] = m0
        acc_ref[...] = _dot(vmetat_ref[0, 0], jnp.exp2(sm - m0).astype(_BF))
        bufs = (s0_ref, s1_ref)
        scores(0, s0_ref)

        unroll = _KEY_UNROLL if nchunks > _KEY_UNROLL else 2

        def group(i, carry):
            c0 = unroll * i
            for u in range(unroll):
                scores(c0 + u + 1, bufs[(u + 1) % 2])
                softmax_pv(c0 + u, bufs[u % 2])
            return carry

        full = (nchunks - 1) // unroll
        lax.fori_loop(0, full, group, 0)
        for c in range(full * unroll, nchunks):
            if c + 1 < nchunks:
                scores(c + 1, bufs[(c + 1) % 2])
            softmax_pv(c, bufs[c % 2])
        _write_output(acc_ref, o_ref, tq)

    @pl.when(b >= nvalid)
    def _():
        o_ref[...] = jnp.zeros(o_ref.shape, o_ref.dtype)


def _global_fast_body(nvalid, nchunks, tq, q_ref, kt_ref, vt_ref, kmeta_ref, vmetat_ref, _prev_ref, o_ref,
                      qt_ref, acc_ref):
    b = pl.program_id(0)
    cols = qt_ref.shape[1]
    blocks = [slice(c0, min(c0 + _QUERY_BLOCK, cols)) for c0 in range(0, cols, _QUERY_BLOCK)]

    def chunks(cs):
        pending = []
        for c in cs:
            for sl in blocks:
                pending.append((c, sl, _dot_tn(kt_ref[0, c], qt_ref[:, sl])))
                if len(pending) > _SCORE_LOOKAHEAD:
                    values(*pending.pop(0))
        for item in pending:
            values(*item)

    def values(c, sl, s):
        acc_ref[:, sl] += _dot(vt_ref[0, c], jnp.exp2(s).astype(_BF))

    @pl.when(b < nvalid)
    def _():
        _stack_queries(q_ref, qt_ref)
        acc_ref[...] = _dot(vmetat_ref[0, 0], jnp.exp2(_dot(kmeta_ref[0, 0], qt_ref[...])).astype(_BF))

        def group(i, carry):
            chunks([_KEY_UNROLL * i + u for u in range(_KEY_UNROLL)])
            return carry

        full = nchunks // _KEY_UNROLL
        lax.fori_loop(0, full, group, 0)
        if full * _KEY_UNROLL < nchunks:
            chunks(range(full * _KEY_UNROLL, nchunks))
        _write_output(acc_ref, o_ref, tq)

    @pl.when(b >= nvalid)
    def _():
        o_ref[...] = jnp.zeros(o_ref.shape, o_ref.dtype)


def _stack_queries(q_ref, qt_ref):
    if q_ref.shape[0] == GROUP:
        qt_ref[...] = jnp.concatenate([q_ref[g, cc] for g in range(GROUP) for cc in range(q_ref.shape[1])], axis=1)
    else:
        qt_ref[...] = q_ref[0, 0]


def _write_output(acc_ref, o_ref, tq):
    acc = acc_ref[...]
    o = (acc / acc[HEAD_DIM:HEAD_DIM + 1]).T.astype(_BF)
    for g in range(GROUP):
        o_ref[:, g * HEAD_DIM:(g + 1) * HEAD_DIM] = o[g * tq:(g + 1) * tq, :HEAD_DIM]


def _global_attention(q, kbt, vbt, kmeta, vmetat, prev, out_rows, *, tq, q_row0, q_rows_per_batch, grid_batches,
                      valid_batches, kv_batch0, kv_row0, n, offset_is_bound=False):
    t = TOKEN_TILE
    nchunks = n // t
    qt = q_rows_per_batch // tq
    assert q_row0 % tq == 0 and kv_row0 % n == 0 and q_rows_per_batch % tq == 0
    kvb = lambda b: jnp.minimum(b, valid_batches - 1)
    if tq == N_META:
        cols = V_LANES
        q_spec = pl.BlockSpec((1, 1, QK_ROWS, cols), lambda b, j, i: (kv_batch0 + kvb(b), j, 0, 0))
    else:
        cols = GROUP * tq
        assert tq % t == 0
        q_spec = pl.BlockSpec((GROUP, tq // t, QK_ROWS, t),
                              lambda b, j, i: (j, (q_row0 + b * q_rows_per_batch) // tq + i, 0, 0))
    in_specs = [
        q_spec,
        pl.BlockSpec((1, nchunks, QK_ROWS, t), lambda b, j, i: (j, kv_row0 // n + kvb(b), 0, 0)),
        pl.BlockSpec((1, nchunks, V_LANES, t), lambda b, j, i: (j, kv_row0 // n + kvb(b), 0, 0)),
        pl.BlockSpec((1, 1, N_META, QK_ROWS), lambda b, j, i: (kv_batch0 + kvb(b), j, 0, 0)),
        pl.BlockSpec((1, 1, V_LANES, N_META), lambda b, j, i: (kv_batch0 + kvb(b), j, 0, 0)),
    ]
    args = [q, kbt, vbt, kmeta, vmetat]
    aliases = {}
    body = functools.partial(_global_fast_body if offset_is_bound else _global_body, valid_batches, nchunks, tq)
    score_bufs = [] if offset_is_bound else [pltpu.VMEM((t, cols), _F32), pltpu.VMEM((t, cols), _F32),
                                             pltpu.VMEM((1, cols), _F32)]
    if prev is None:
        body = functools.partial(_global_body_noprev, body)
    else:
        in_specs.append(pl.BlockSpec(memory_space=pl.ANY))
        args.append(prev)
        aliases = {5: 0}
    return pl.pallas_call(
        body,
        grid=(grid_batches, N_KV, qt),
        in_specs=in_specs,
        out_specs=pl.BlockSpec((tq, GROUP * HEAD_DIM), lambda b, j, i: (q_row0 // tq + b * qt + i, j)),
        out_shape=jax.ShapeDtypeStruct((out_rows, Q_COLS), _BF),
        scratch_shapes=[pltpu.VMEM((QK_ROWS, cols), _BF)] + score_bufs + [pltpu.VMEM((V_LANES, cols), _F32)],
        input_output_aliases=aliases,
        compiler_params=_params(("parallel", "parallel", "arbitrary")),
        name=f"global_attn{'_fast' if offset_is_bound else ''}_tq{tq}_n{n}",
    )(*args)


def _global_body_noprev(body, q_ref, kt_ref, vt_ref, kmeta_ref, vmetat_ref, o_ref, *scratch):
    body(q_ref, kt_ref, vt_ref, kmeta_ref, vmetat_ref, None, o_ref, *scratch)


def _mix_body(h_ref, oa_ref, ob_ref, ga_ref, gb_ref, wa_ref, wb_ref, wo_ref, g_ref, o_ref):
    mix = (ga_ref[...].astype(_F32) * _dot(oa_ref[...], wa_ref[...])
           + gb_ref[...].astype(_F32) * _dot(ob_ref[...], wb_ref[...]))
    u = _dot(mix.astype(_BF), wo_ref[...])
    ms = jnp.mean(u * u, axis=-1, keepdims=True)
    o_ref[...] = h_ref[...] + u * lax.rsqrt(ms + EPS) * g_ref[...]


def _mix(h, oa, ob, ga, gb, wa, wb, wo, g):
    ntok = h.shape[0]
    t = DENSE_TILE
    row = lambda i: (i, 0)
    return pl.pallas_call(
        _mix_body,
        grid=(ntok // t,),
        in_specs=[
            pl.BlockSpec((t, D_MODEL), row),
            pl.BlockSpec((t, Q_COLS), row), pl.BlockSpec((t, Q_COLS), row),
            pl.BlockSpec((t, D_MODEL), row), pl.BlockSpec((t, D_MODEL), row),
            _const_spec((Q_COLS, D_MODEL)), _const_spec((Q_COLS, D_MODEL)),
            _const_spec((D_MODEL, D_MODEL)), _const_spec((1, D_MODEL)),
        ],
        out_specs=pl.BlockSpec((t, D_MODEL), row),
        out_shape=jax.ShapeDtypeStruct(h.shape, _F32),
        input_output_aliases={0: 0},
        compiler_params=_params(("parallel",)),
        name="branch_mix",
    )(h, oa, ob, ga, gb, wa, wb, wo, g)


_FF_CHUNK = 256


def _ffn_body(h_ref, gpre_ref, wup_ref, wdown_ref, gpost_ref, o_ref, act_ref):
    h = h_ref[...]
    ms = jnp.mean(h * h, axis=-1, keepdims=True)
    xn = (h * lax.rsqrt(ms + EPS) * gpre_ref[...]).astype(_BF)
    for c in range(0, D_FF, _FF_CHUNK):
        a = _dot(xn, wup_ref[:, c:c + _FF_CHUNK])
        b = _dot(xn, wup_ref[:, D_FF + c:D_FF + c + _FF_CHUNK])
        act_ref[:, c:c + _FF_CHUNK] = (a * jax.nn.sigmoid(a) * b).astype(_BF)
    u = _dot(act_ref[...], wdown_ref[...])
    ms = jnp.mean(u * u, axis=-1, keepdims=True)
    o_ref[...] = h + u * lax.rsqrt(ms + EPS) * gpost_ref[...]


def _ffn_last_body(blocks, h_ref, gpre_ref, wup_ref, wdown_ref, gpost_ref, yp_ref, ys_ref, act_ref):
    p, sm = blocks
    i = pl.program_id(0)

    @pl.when(i < p)
    def _():
        _ffn_body(h_ref, gpre_ref, wup_ref, wdown_ref, gpost_ref, yp_ref, act_ref)

    @pl.when(jnp.logical_and(i >= p, i < p + sm))
    def _():
        _ffn_body(h_ref, gpre_ref, wup_ref, wdown_ref, gpost_ref, ys_ref, act_ref)


def _ffn(h, gpre, wup, wdown, gpost, split_rows=None):
    ntok = h.shape[0]
    t = DENSE_TILE
    row = lambda i: (i, 0)
    if split_rows is None:
        body, aliases = _ffn_body, {0: 0}
        out_specs = pl.BlockSpec((t, D_MODEL), row)
        out_shape = jax.ShapeDtypeStruct(h.shape, _F32)
    else:
        p, sm = (r // t for r in split_rows)
        assert p * t == split_rows[0] and sm * t == split_rows[1]
        body, aliases = functools.partial(_ffn_last_body, (p, sm)), {}
        out_specs = (pl.BlockSpec((t, D_MODEL), lambda i: (jnp.minimum(i, p - 1), 0)),
                     pl.BlockSpec((t, D_MODEL), lambda i: (jnp.clip(i - p, 0, sm - 1), 0)))
        out_shape = (jax.ShapeDtypeStruct((split_rows[0], D_MODEL), _F32),
                     jax.ShapeDtypeStruct((split_rows[1], D_MODEL), _F32))
    return pl.pallas_call(
        body,
        grid=(ntok // t,),
        in_specs=[
            pl.BlockSpec((t, D_MODEL), row),
            _const_spec((1, D_MODEL)),
            _const_spec((D_MODEL, 2 * D_FF)),
            _const_spec((D_FF, D_MODEL)),
            _const_spec((1, D_MODEL)),
        ],
        out_specs=out_specs,
        out_shape=out_shape,
        scratch_shapes=[pltpu.VMEM((t, D_FF), _BF)],
        input_output_aliases=aliases,
        compiler_params=_params(("arbitrary",)),
        name="swiglu_ffn" if split_rows is None else "swiglu_ffn_last",
    )(h, gpre, wup, wdown, gpost)


def _rope_tables(layout, ntok):
    n_real, b1, n1, b2, n2 = layout
    idx = np.zeros((ntok,), np.int64)
    idx[:b1 * n1] = np.arange(b1 * n1) % n1
    idx[b1 * n1:n_real] = np.arange(b2 * n2) % n2
    rows = (idx // GRID_W).astype(np.float32)
    cols = (idx % GRID_W).astype(np.float32)
    rows[n_real:] = 0.0
    cols[n_real:] = 0.0
    freqs = ROPE_BASE ** (-jnp.arange(ROPE_FREQS, dtype=_F32) / ROPE_FREQS)
    ang_r = jnp.asarray(rows)[:, None] * freqs[None, :]
    ang_c = jnp.asarray(cols)[:, None] * freqs[None, :]
    cr, sr, cc, sc = jnp.cos(ang_r), jnp.sin(ang_r), jnp.cos(ang_c), jnp.sin(ang_c)
    cos64 = jnp.concatenate([cr, cr, cc, cc], axis=1)
    sin64 = jnp.concatenate([-sr, sr, -sc, sc], axis=1)
    return cos64.T, sin64.T


def _window_bias():
    slopes = 2.0 ** (-8.0 * np.arange(1, N_HEADS + 1, dtype=np.float64) / N_HEADS)
    rel = np.arange(BLOCK)[:, None] - (np.arange(3 * BLOCK) - BLOCK)[None, :]
    dist = np.abs(rel)
    band = dist <= BLOCK
    base = np.where(band[None], -slopes[:, None, None] * dist[None].astype(np.float64) * LOG2E, NEG_INF)
    no_prev = base.copy()
    no_prev[:, :, :BLOCK] = NEG_INF
    no_next = base.copy()
    no_next[:, :, 2 * BLOCK:3 * BLOCK] = NEG_INF
    out = np.stack([base, no_prev, no_next]).reshape(3, N_KV, GROUP * BLOCK, 3 * BLOCK)
    return jnp.asarray(np.swapaxes(out, -1, -2), _F32)


def _rearranged_w_in(w_in):
    o = np.cumsum([0, Q_COLS, KV_COLS, KV_COLS, Q_COLS, KV_COLS, KV_COLS, D_MODEL, D_MODEL])
    qa, ka, va, qb, kb, vb, ga, gb = [w_in[..., o[i]:o[i + 1]] for i in range(8)]

    def pad_v(v):
        z = jnp.zeros(v.shape[:-1] + (V_LANES - HEAD_DIM,), v.dtype)
        return jnp.concatenate([v[..., :HEAD_DIM], z, v[..., HEAD_DIM:], z], axis=-1)

    w_rows = jnp.concatenate([ga, gb], axis=-1).astype(_BF)
    w_t = jnp.swapaxes(jnp.concatenate([ka, kb, pad_v(vb), qb, pad_v(va), qa], axis=-1), -1, -2).astype(_BF)
    return w_rows, w_t


def _tail_features(xt, n_real, nb):
    tail = xt[:, n_real // TOKEN_TILE:]
    tail = jnp.moveaxis(tail, 2, 1).reshape(xt.shape[0], xt.shape[2], -1)[:, :, :nb * N_META]
    return tail.reshape(xt.shape[0], xt.shape[2], nb, N_META)


def kernel(x_prompt, x_sample, meta_tokens, g_mix_pre, g_mix_post, g_ffn_pre, g_ffn_post, w_in, q_norm_b,
           k_norm_b, sink_a, w_branch_a, w_branch_b, w_out, w_ffn_up, w_ffn_down):
    b1, n1, _ = x_prompt.shape
    b2, n2, _ = x_sample.shape
    depth = w_in.shape[0]
    t = TOKEN_TILE
    assert n1 % t == 0 and n2 % t == 0 and (b1 * n1) % n2 == 0 and n1 % GRID_W == 0 and n2 % GRID_W == 0
    assert (b1 * n1) % DENSE_TILE == 0 and (b2 * n2) % DENSE_TILE == 0
    nb = b1 + b2
    n_real = b1 * n1 + b2 * n2
    ntok = -(-(n_real + nb * N_META) // DENSE_TILE) * DENSE_TILE
    tail = ntok - n_real
    assert tail % t == 0
    layout = (n_real, b1, n1, b2, n2)

    h = (x_prompt.reshape(b1 * n1, D_MODEL).astype(_F32), x_sample.reshape(b2 * n2, D_MODEL).astype(_F32),
         jnp.concatenate([jnp.tile(meta_tokens.astype(_F32), (nb, 1)),
                          jnp.zeros((tail - nb * N_META, D_MODEL), _F32)], axis=0))

    cost, sint = _rope_tables(layout, ntok)
    bias = _window_bias()
    w_rows, w_t = _rearranged_w_in(w_in)
    wa, wb, wo = w_branch_a.astype(_BF), w_branch_b.astype(_BF), w_out.astype(_BF)
    wup, wdown = w_ffn_up.astype(_BF), w_ffn_down.astype(_BF)
    row = lambda g: g.reshape(1, -1).astype(_F32)
    col = lambda g: g.reshape(-1, 1).astype(_F32)

    for l in range(depth):
        bound = (HEAD_DIM * SCALE * LOG2E * 1.02) * jnp.max(jnp.abs(q_norm_b[l])) * jnp.max(jnp.abs(k_norm_b[l]))
        unit = (jnp.arange(QK_ROWS - HEAD_DIM) == 0).astype(_F32).reshape(-1, 1)
        proj = _in_proj(h, row(g_mix_pre[l]), w_rows[l], w_t[l], cost, sint, col(q_norm_b[l]), col(k_norm_b[l]),
                        -bound.astype(_F32) * unit, unit)
        if l == 0:
            *proj, h = proj
        qat, qbt, kat, kbt, vat, vbt, ga, gb = proj
        sink = sink_a[l].astype(_F32) * LOG2E
        ka_tail, va_tail = _tail_features(kat, n_real, nb), _tail_features(vat, n_real, nb)
        kameta = jnp.transpose(ka_tail, (2, 0, 1, 3))
        vameta = jnp.transpose(va_tail, (2, 0, 3, 1))
        vametat = jnp.transpose(va_tail, (2, 0, 1, 3))
        qa_meta = jnp.transpose(_tail_features(qat, n_real, nb), (2, 3, 0, 1)).reshape(nb * N_META, Q_COLS)
        qa_meta = jnp.pad(qa_meta, ((0, tail - nb * N_META), (0, 0)))

        oa = _window_meta_attention(sink, qa_meta, kat, vat, kameta, vameta, layout, ntok)
        oa = _window_attention(sink, qat, kat, vat, kameta, vametat, bias, oa, layout)

        kbmeta = jnp.transpose(_tail_features(kbt, n_real, nb), (2, 0, 3, 1))
        vbmetat = jnp.transpose(_tail_features(vbt, n_real, nb), (2, 0, 1, 3))
        qmeta = _tail_features(qbt, n_real, nb).reshape(N_KV, GROUP, QK_ROWS, nb, N_META)
        qmeta = jnp.transpose(qmeta, (3, 0, 2, 1, 4)).reshape(nb, N_KV, QK_ROWS, GROUP * N_META)
        qmeta = jnp.pad(qmeta, ((0, 0), (0, 0), (0, 0), (0, V_LANES - GROUP * N_META)))

        glob = functools.partial(_global_attention, kbt=kbt, vbt=vbt, kmeta=kbmeta, vmetat=vbmetat, out_rows=ntok)
        def mixer_b(offset_is_bound):
            tq_real = 1024 if offset_is_bound else 512
            out = glob(qmeta, prev=None, tq=N_META, q_row0=n_real, q_rows_per_batch=N_META, grid_batches=b1,
                       valid_batches=b1, kv_batch0=0, kv_row0=0, n=n1, offset_is_bound=offset_is_bound)
            out = glob(qmeta, prev=out, tq=N_META, q_row0=n_real + b1 * N_META, q_rows_per_batch=N_META,
                       grid_batches=tail // N_META - b1, valid_batches=b2, kv_batch0=b1, kv_row0=b1 * n1, n=n2,
                       offset_is_bound=offset_is_bound)
            out = glob(qbt, prev=out, tq=tq_real, q_row0=0, q_rows_per_batch=n1, grid_batches=b1, valid_batches=b1,
                       kv_batch0=0, kv_row0=0, n=n1, offset_is_bound=offset_is_bound)
            return glob(qbt, prev=out, tq=tq_real, q_row0=b1 * n1, q_rows_per_batch=n2, grid_batches=b2,
                        valid_batches=b2, kv_batch0=b1, kv_row0=b1 * n1, n=n2, offset_is_bound=offset_is_bound)

        ob = lax.cond(bound <= SAFE_OFFSET_MAX, functools.partial(mixer_b, True), functools.partial(mixer_b, False))

        h = _mix(h, oa, ob, ga, gb, wa[l], wb[l], wo[l], row(g_mix_post[l]))
        h = _ffn(h, row(g_ffn_pre[l]), wup[l], wdown[l], row(g_ffn_post[l]),
                 split_rows=(b1 * n1, b2 * n2) if l == depth - 1 else None)

    y_prompt, y_sample = h
    return (y_prompt.reshape(b1, n1, D_MODEL).astype(x_prompt.dtype),
            y_sample.reshape(b2, n2, D_MODEL).astype(x_sample.dtype))
```

```python
import functools
import math

import jax
import jax.numpy as jnp
import numpy as np
from jax import lax
from jax.experimental import pallas as pl
from jax.experimental.pallas import tpu as pltpu

D_MODEL = 1024
HEAD_DIM = 64
N_HEADS = 8
N_KV = 2
GROUP = N_HEADS // N_KV
Q_COLS = N_HEADS * HEAD_DIM
KV_COLS = N_KV * HEAD_DIM
N_META = 16
BLOCK = 128
GRID_W = 64
ROPE_BASE = 10000.0
ROPE_FREQS = HEAD_DIM // 4
D_FF = 2816
EPS = 1e-6
NEG_INF = -1e30
SCALE = HEAD_DIM ** -0.5
LOG2E = math.log2(math.e)

QK_ROWS = HEAD_DIM + 16
SAFE_OFFSET_MAX = 40.0
V_LANES = 128
TOKEN_TILE = 512
DENSE_TILE = 1024
WINDOW_TILE = 1024
VMEM_LIMIT = 56 * 1024 * 1024

_C_GA = 0
_C_GB = _C_GA + D_MODEL
_C_END = _C_GB + D_MODEL
_R_KA = 0
_R_KB = _R_KA + KV_COLS
_R_VB = _R_KB + KV_COLS
_R_QB = _R_VB + N_KV * V_LANES
_R_VA = _R_QB + Q_COLS
_R_QA = _R_VA + N_KV * V_LANES
_R_END = _R_QA + Q_COLS

_BF = jnp.bfloat16
_F32 = jnp.float32


def _dot(a, b):
    return jnp.dot(a, b, preferred_element_type=_F32)


def _dot_nt(a, b):
    return lax.dot_general(a, b, (((1,), (1,)), ((), ())), preferred_element_type=_F32)


def _dot_tn(a, b):
    return lax.dot_general(a, b, (((0,), (0,)), ((), ())), preferred_element_type=_F32)


def _params(sem, vmem=VMEM_LIMIT):
    return pltpu.CompilerParams(dimension_semantics=sem, vmem_limit_bytes=vmem)


def _const_spec(shape):
    nd = len(shape)
    return pl.BlockSpec(shape, lambda *_: (0,) * nd, pipeline_mode=pl.Buffered(1))


def _in_proj_body(h_ref, g_ref, w_ref, wt_ref, cost_ref, sint_ref, qg_ref, kg_ref, qx_ref, kx_ref,
                  qat_ref, qbt_ref, kat_ref, kbt_ref, vat_ref, vbt_ref, ga_ref, gb_ref):
    f = ROPE_FREQS
    ones_row = (lax.broadcasted_iota(jnp.int32, (V_LANES, 1), 0) == HEAD_DIM).astype(_F32)
    q_extra = jnp.broadcast_to(qx_ref[...], (QK_ROWS - HEAD_DIM, TOKEN_TILE)).astype(_BF)
    k_extra = jnp.broadcast_to(kx_ref[...], (QK_ROWS - HEAD_DIM, TOKEN_TILE)).astype(_BF)

    for cc in range(h_ref.shape[0] // TOKEN_TILE):
        rows = slice(cc * TOKEN_TILE, (cc + 1) * TOKEN_TILE)
        h = h_ref[rows, :]
        ms = jnp.mean(h * h, axis=-1, keepdims=True)
        xn = (h * lax.rsqrt(ms + EPS) * g_ref[...]).astype(_BF)

        ga_ref[rows, :] = jax.nn.sigmoid(_dot(xn, w_ref[:, _C_GA:_C_GA + D_MODEL])).astype(_BF)
        gb_ref[rows, :] = jax.nn.sigmoid(_dot(xn, w_ref[:, _C_GB:_C_GB + D_MODEL])).astype(_BF)

        tt = _dot_nt(wt_ref[...], xn)
        cost = cost_ref[:, rows]
        sint = sint_ref[:, rows]

        def norm_rope(x, gain):
            x = x * lax.rsqrt(jnp.mean(x * x, axis=0, keepdims=True) + EPS) * gain
            partner = jnp.concatenate([x[f:2 * f], x[0:f], x[3 * f:4 * f], x[2 * f:3 * f]], axis=0)
            return x * cost + partner * sint

        for j in range(N_KV):
            kat_ref[j, cc] = tt[_R_KA + j * HEAD_DIM:_R_KA + (j + 1) * HEAD_DIM].astype(_BF)
            k = norm_rope(tt[_R_KB + j * HEAD_DIM:_R_KB + (j + 1) * HEAD_DIM], kg_ref[...])
            kbt_ref[j, cc, :HEAD_DIM, :] = k.astype(_BF)
            kbt_ref[j, cc, HEAD_DIM:, :] = k_extra
            vat_ref[j, cc] = (tt[_R_VA + j * V_LANES:_R_VA + (j + 1) * V_LANES] + ones_row).astype(_BF)
            vbt_ref[j, cc] = (tt[_R_VB + j * V_LANES:_R_VB + (j + 1) * V_LANES] + ones_row).astype(_BF)
        for hd in range(N_HEADS):
            qa = tt[_R_QA + hd * HEAD_DIM:_R_QA + (hd + 1) * HEAD_DIM] * (SCALE * LOG2E)
            qat_ref[hd, cc] = qa.astype(_BF)
            q = norm_rope(tt[_R_QB + hd * HEAD_DIM:_R_QB + (hd + 1) * HEAD_DIM], qg_ref[...])
            qbt_ref[hd, cc, :HEAD_DIM, :] = (q * (SCALE * LOG2E)).astype(_BF)
            qbt_ref[hd, cc, HEAD_DIM:, :] = q_extra


def _in_proj_first_body(blocks, xp_ref, xs_ref, tail_ref, *refs):
    p, sm = blocks
    i = pl.program_id(0)
    *rest, h_ref = refs
    h_ref[...] = jnp.where(i < p, xp_ref[...], jnp.where(i < p + sm, xs_ref[...], tail_ref[...]))
    _in_proj_body(h_ref, *rest)


def _in_proj(h, g, w, wt, cost, sint, qg, kg, qx, kx):
    first = isinstance(h, tuple)
    t = TOKEN_TILE if first else DENSE_TILE
    ntok = sum(x.shape[0] for x in h) if first else h.shape[0]
    per = t // TOKEN_TILE
    nch = ntok // TOKEN_TILE
    row = lambda i: (i, 0)
    chunk = lambda i: (0, i, 0, 0)
    feat = lambda heads, rows: (jax.ShapeDtypeStruct((heads, nch, rows, TOKEN_TILE), _BF),
                                pl.BlockSpec((heads, per, rows, TOKEN_TILE), chunk))
    outs = [
        feat(N_HEADS, HEAD_DIM),
        feat(N_HEADS, QK_ROWS),
        feat(N_KV, HEAD_DIM),
        feat(N_KV, QK_ROWS),
        feat(N_KV, V_LANES),
        feat(N_KV, V_LANES),
        (jax.ShapeDtypeStruct((ntok, D_MODEL), _BF), pl.BlockSpec((t, D_MODEL), row)),
        (jax.ShapeDtypeStruct((ntok, D_MODEL), _BF), pl.BlockSpec((t, D_MODEL), row)),
    ]
    if first:
        p, sm, tl = (x.shape[0] // t for x in h)
        body = functools.partial(_in_proj_first_body, (p, sm))
        h_specs = [pl.BlockSpec((t, D_MODEL), lambda i: (jnp.minimum(i, p - 1), 0)),
                   pl.BlockSpec((t, D_MODEL), lambda i: (jnp.clip(i - p, 0, sm - 1), 0)),
                   pl.BlockSpec((t, D_MODEL), lambda i: (jnp.clip(i - p - sm, 0, tl - 1), 0))]
        outs.append((jax.ShapeDtypeStruct((ntok, D_MODEL), _F32), pl.BlockSpec((t, D_MODEL), row)))
        h_args = list(h)
    else:
        body, h_specs, h_args = _in_proj_body, [pl.BlockSpec((t, D_MODEL), row)], [h]
    return pl.pallas_call(
        body,
        grid=(ntok // t,),
        in_specs=h_specs + [
            _const_spec((1, D_MODEL)),
            _const_spec((D_MODEL, _C_END)),
            _const_spec((_R_END, D_MODEL)),
            pl.BlockSpec((HEAD_DIM, t), lambda i: (0, i)),
            pl.BlockSpec((HEAD_DIM, t), lambda i: (0, i)),
            _const_spec((HEAD_DIM, 1)),
            _const_spec((HEAD_DIM, 1)),
            _const_spec((QK_ROWS - HEAD_DIM, 1)),
            _const_spec((QK_ROWS - HEAD_DIM, 1)),
        ],
        out_specs=tuple(o[1] for o in outs),
        out_shape=tuple(o[0] for o in outs),
        compiler_params=_params(("parallel",)),
        name="in_proj_first" if first else "in_proj",
    )(*h_args, g, w, wt, cost, sint, qg, kg, qx, kx)


_WINDOW_LOOKAHEAD = 3


def _stack_heads(q, j):
    base = j * GROUP * HEAD_DIM
    return jnp.concatenate([q[:, base + g * HEAD_DIM: base + (g + 1) * HEAD_DIM] for g in range(GROUP)], axis=0)


def _sink_rows(sink_ref, j, rows, lanes):
    return jnp.concatenate([jnp.full((rows, lanes), sink_ref[j * GROUP + g], _F32) for g in range(GROUP)], axis=0)


def _window_body(geom, sink_ref, q_ref, kmain_ref, kprev_ref, knext_ref, vmain_ref, vprev_ref, vnext_ref,
                 kmeta_ref, vmeta_ref, bias_ref, _tail_ref, o_ref):
    t1, tpb1, tpb2 = geom
    t = pl.program_id(0)
    in_prompt = t < t1
    is_first = jnp.where(in_prompt, t % tpb1 == 0, (t - t1) % tpb2 == 0)
    is_last = jnp.where(in_prompt, t % tpb1 == tpb1 - 1, (t - t1) % tpb2 == tpb2 - 1)
    nblk = WINDOW_TILE // BLOCK
    per_chunk = TOKEN_TILE // BLOCK

    def softmax_values(j, r, s, sm, vwin, vmeta, sink):
        m = jnp.maximum(jnp.maximum(jnp.max(s, axis=0, keepdims=True), jnp.max(sm, axis=0, keepdims=True)), sink)
        acc = (_dot(vwin, jnp.exp2(s - m).astype(_BF))
               + _dot(vmeta, jnp.exp2(sm - m).astype(_BF)))
        l = acc[HEAD_DIM:HEAD_DIM + 1] + jnp.exp2(sink - m)
        o = (acc / l).T.astype(_BF)
        for g in range(GROUP):
            c0 = (j * GROUP + g) * HEAD_DIM
            o_ref[r * BLOCK:(r + 1) * BLOCK, c0:c0 + HEAD_DIM] = o[g * BLOCK:(g + 1) * BLOCK, :HEAD_DIM]

    pending = []
    for j in range(N_KV):
        chunks = range(kmain_ref.shape[1])
        kcat = jnp.concatenate([kprev_ref[j, 0]] + [kmain_ref[j, cc] for cc in chunks] + [knext_ref[j, 0]], axis=1)
        vcat = jnp.concatenate([vprev_ref[j, 0]] + [vmain_ref[j, cc] for cc in chunks] + [vnext_ref[j, 0]], axis=1)
        kmeta = kmeta_ref[0, j]
        vmeta = vmeta_ref[0, j]
        sink = jnp.concatenate([jnp.full((1, BLOCK), sink_ref[j * GROUP + g], _F32) for g in range(GROUP)], axis=1)
        for r in range(nblk):
            variant = 0
            if r == 0:
                variant = jnp.where(is_first, 1, 0)
            if r == nblk - 1:
                variant = jnp.where(is_last, 2, variant)
            c0 = (r % per_chunk) * BLOCK
            qt = jnp.concatenate([q_ref[j * GROUP + g, r // per_chunk][:, c0:c0 + BLOCK] for g in range(GROUP)],
                                 axis=1)
            s = _dot_tn(kcat[:, r * BLOCK:(r + 3) * BLOCK], qt) + bias_ref[variant, j]
            sm = _dot_tn(kmeta, qt)
            pending.append((j, r, s, sm, vcat[:, r * BLOCK:(r + 3) * BLOCK], vmeta, sink))
            if len(pending) > _WINDOW_LOOKAHEAD:
                softmax_values(*pending.pop(0))
    for item in pending:
        softmax_values(*item)


def _window_attention(sink, qat, kat, vat, kmeta, vmetat, bias, tail, layout):
    n_real, b1, n1, b2, n2 = layout
    t = WINDOW_TILE
    per = t // TOKEN_TILE
    nch = kat.shape[1]
    assert n1 % t == 0 and n2 % t == 0
    t1 = b1 * n1 // t
    tpb1, tpb2 = n1 // t, n2 // t
    sub = TOKEN_TILE // BLOCK

    def bid(i):
        return jnp.where(i < t1, i // tpb1, b1 + (i - t1) // tpb2)

    main = lambda i: (0, i, 0, 0)
    prev = lambda i: (0, jnp.maximum(i * per - 1, 0), 0, sub - 1)
    nxt = lambda i: (0, jnp.minimum((i + 1) * per, nch - 1), 0, 0)
    return pl.pallas_call(
        functools.partial(_window_body, (t1, tpb1, tpb2)),
        grid=(n_real // t,),
        in_specs=[
            pl.BlockSpec(memory_space=pltpu.SMEM),
            pl.BlockSpec((N_HEADS, per, HEAD_DIM, TOKEN_TILE), main),
            pl.BlockSpec((N_KV, per, HEAD_DIM, TOKEN_TILE), main),
            pl.BlockSpec((N_KV, 1, HEAD_DIM, BLOCK), prev),
            pl.BlockSpec((N_KV, 1, HEAD_DIM, BLOCK), nxt),
            pl.BlockSpec((N_KV, per, V_LANES, TOKEN_TILE), main),
            pl.BlockSpec((N_KV, 1, V_LANES, BLOCK), prev),
            pl.BlockSpec((N_KV, 1, V_LANES, BLOCK), nxt),
            pl.BlockSpec((1, N_KV, HEAD_DIM, N_META), lambda i: (bid(i), 0, 0, 0)),
            pl.BlockSpec((1, N_KV, V_LANES, N_META), lambda i: (bid(i), 0, 0, 0)),
            _const_spec((3, N_KV, 3 * BLOCK, GROUP * BLOCK)),
            pl.BlockSpec(memory_space=pl.ANY),
        ],
        out_specs=pl.BlockSpec((t, Q_COLS), lambda i: (i, 0)),
        out_shape=jax.ShapeDtypeStruct(tail.shape, _BF),
        input_output_aliases={11: 0},
        compiler_params=_params(("parallel",)),
        name="window_attn",
    )(sink, qat, kat, kat, kat, vat, vat, vat, kmeta, vmetat, bias, tail)


def _window_meta_body(nb, sink_ref, q_ref, kfirst_ref, vfirst_ref, kmeta_ref, vmeta_ref, o_ref):
    b = pl.program_id(0)

    @pl.when(b < nb)
    def _():
        q_all = q_ref[...]
        for j in range(N_KV):
            qs = _stack_heads(q_all, j)
            sink = _sink_rows(sink_ref, j, N_META, 1)
            sm = _dot(qs, kmeta_ref[0, j])
            sf = _dot(qs, kfirst_ref[j, 0])
            m = jnp.maximum(jnp.maximum(sm.max(axis=1, keepdims=True), sf.max(axis=1, keepdims=True)), sink)
            pm = jnp.exp2(sm - m).astype(_BF)
            pf = jnp.exp2(sf - m).astype(_BF)
            acc = _dot(pm, vmeta_ref[0, j]) + _dot_nt(pf, vfirst_ref[j, 0])
            l = acc[:, HEAD_DIM:HEAD_DIM + 1] + jnp.exp2(sink - m)
            o = (acc[:, :HEAD_DIM] / l).astype(_BF)
            for g in range(GROUP):
                c0 = (j * GROUP + g) * HEAD_DIM
                o_ref[:, c0:c0 + HEAD_DIM] = o[g * N_META:(g + 1) * N_META]

    @pl.when(b >= nb)
    def _():
        o_ref[...] = jnp.zeros(o_ref.shape, o_ref.dtype)


def _window_meta_attention(sink, q_rows, kat, vat, kmeta, vmeta, layout, out_rows):
    n_real, b1, n1, b2, n2 = layout
    nb = b1 + b2
    t = TOKEN_TILE

    def start_chunk(b):
        bc = jnp.minimum(b, nb - 1)
        return jnp.where(bc < b1, bc * (n1 // t), b1 * (n1 // t) + (bc - b1) * (n2 // t))

    return pl.pallas_call(
        functools.partial(_window_meta_body, nb),
        grid=((out_rows - n_real) // N_META,),
        in_specs=[
            pl.BlockSpec(memory_space=pltpu.SMEM),
            pl.BlockSpec((N_META, Q_COLS), lambda b: (b, 0)),
            pl.BlockSpec((N_KV, 1, HEAD_DIM, BLOCK), lambda b: (0, start_chunk(b), 0, 0)),
            pl.BlockSpec((N_KV, 1, V_LANES, BLOCK), lambda b: (0, start_chunk(b), 0, 0)),
            pl.BlockSpec((1, N_KV, HEAD_DIM, N_META), lambda b: (jnp.minimum(b, nb - 1), 0, 0, 0)),
            pl.BlockSpec((1, N_KV, N_META, V_LANES), lambda b: (jnp.minimum(b, nb - 1), 0, 0, 0)),
        ],
        out_specs=pl.BlockSpec((N_META, Q_COLS), lambda b: (n_real // N_META + b, 0)),
        out_shape=jax.ShapeDtypeStruct((out_rows, Q_COLS), _BF),
        compiler_params=_params(("arbitrary",)),
        name="window_meta_attn",
    )(sink, q_rows, kat, vat, kmeta, vmeta)


_KEY_UNROLL = 4
_QUERY_BLOCK = 256
_SCORE_LOOKAHEAD = 3


def _global_body(nvalid, nchunks, tq, q_ref, kt_ref, vt_ref, kmeta_ref, vmetat_ref, _prev_ref, o_ref,
                 qt_ref, s0_ref, s1_ref, m_ref, acc_ref):
    b = pl.program_id(0)

    cols = qt_ref.shape[1]
    blocks = [slice(c0, min(c0 + _QUERY_BLOCK, cols)) for c0 in range(0, cols, _QUERY_BLOCK)]

    def scores(c, s_ref):
        for sl in blocks:
            s_ref[:, sl] = _dot_tn(kt_ref[0, c], qt_ref[:, sl])

    def softmax_pv(c, s_ref):
        for sl in blocks:
            s = s_ref[:, sl]
            m_prev = m_ref[:, sl]
            m_new = jnp.maximum(m_prev, jnp.max(s, axis=0, keepdims=True))
            p = jnp.exp2(s - m_new).astype(_BF)
            acc_ref[:, sl] = jnp.exp2(m_prev - m_new) * acc_ref[:, sl] + _dot(vt_ref[0, c], p)
            m_ref[:, sl] = m_new

    @pl.when(b < nvalid)
    def _():
        _stack_queries(q_ref, qt_ref)
        sm = _dot(kmeta_ref[0, 0], qt_ref[...])
        m0 = jnp.max(sm, axis=0, keepdims=True)
        m_ref[...] = m0
        acc_ref[...] = _dot(vmetat_ref[0, 0], jnp.exp2(sm - m0).astype(_BF))
        bufs = (s0_ref, s1_ref)
        scores(0, s0_ref)

        unroll = _KEY_UNROLL if nchunks > _KEY_UNROLL else 2

        def group(i, carry):
            c0 = unroll * i
            for u in range(unroll):
                scores(c0 + u + 1, bufs[(u + 1) % 2])
                softmax_pv(c0 + u, bufs[u % 2])
            return carry

        full = (nchunks - 1) // unroll
        lax.fori_loop(0, full, group, 0)
        for c in range(full * unroll, nchunks):
            if c + 1 < nchunks:
                scores(c + 1, bufs[(c + 1) % 2])
            softmax_pv(c, bufs[c % 2])
        _write_output(acc_ref, o_ref, tq)

    @pl.when(b >= nvalid)
    def _():
        o_ref[...] = jnp.zeros(o_ref.shape, o_ref.dtype)


def _maxfree_chunks(kt_ref, vt_ref, q_ref, acc_ref, cs):
    cols = q_ref.shape[1]
    blocks = [slice(c0, min(c0 + _QUERY_BLOCK, cols)) for c0 in range(0, cols, _QUERY_BLOCK)]

    def values(c, sl, s):
        acc_ref[:, sl] += _dot(vt_ref[0, c], jnp.exp2(s).astype(_BF))

    pending = []
    for c in cs:
        for sl in blocks:
            pending.append((c, sl, _dot_tn(kt_ref[0, c], q_ref[:, sl])))
            if len(pending) > _SCORE_LOOKAHEAD:
                values(*pending.pop(0))
    for item in pending:
        values(*item)


def _maxfree_attend(nchunks, kt_ref, vt_ref, kmeta_ref, vmetat_ref, q_ref, acc_ref):
    acc_ref[...] = _dot(vmetat_ref[0, 0], jnp.exp2(_dot(kmeta_ref[0, 0], q_ref[...])).astype(_BF))

    def group(i, carry):
        _maxfree_chunks(kt_ref, vt_ref, q_ref, acc_ref, [_KEY_UNROLL * i + u for u in range(_KEY_UNROLL)])
        return carry

    full = nchunks // _KEY_UNROLL
    lax.fori_loop(0, full, group, 0)
    if full * _KEY_UNROLL < nchunks:
        _maxfree_chunks(kt_ref, vt_ref, q_ref, acc_ref, range(full * _KEY_UNROLL, nchunks))


def _global_fast_body(nvalid, nchunks, tq, q_ref, kt_ref, vt_ref, kmeta_ref, vmetat_ref, _prev_ref, o_ref,
                      qt_ref, acc_ref):
    b = pl.program_id(0)

    @pl.when(b < nvalid)
    def _():
        _stack_queries(q_ref, qt_ref)
        _maxfree_attend(nchunks, kt_ref, vt_ref, kmeta_ref, vmetat_ref, qt_ref, acc_ref)
        _write_output(acc_ref, o_ref, tq)

    @pl.when(b >= nvalid)
    def _():
        o_ref[...] = jnp.zeros(o_ref.shape, o_ref.dtype)


def _global_fused_body(nchunks, tq, has_prev, q_ref, kt_ref, vt_ref, kmeta_ref, vmetat_ref, qm_ref, *refs):
    o_ref, otail_ref, qt_ref, acc_ref, accm_ref = refs[2 if has_prev else 1:]
    _stack_queries(q_ref, qt_ref)
    _maxfree_attend(nchunks, kt_ref, vt_ref, kmeta_ref, vmetat_ref, qt_ref, acc_ref)
    _write_output(acc_ref, o_ref, tq)

    @pl.when(pl.program_id(2) == 0)
    def _():
        _maxfree_attend(nchunks, kt_ref, vt_ref, kmeta_ref, vmetat_ref, qm_ref.at[0, 0], accm_ref)
        _write_output(accm_ref, otail_ref, N_META)


def _global_fused(qbt, kbt, vbt, kmeta, vmetat, qmeta, prev, prev_tail, out_rows, *, tq, q_row0, batches,
                  kv_batch0, n):
    t = TOKEN_TILE
    nchunks = n // t
    qt = n // tq
    cols = GROUP * tq
    assert tq % t == 0 and q_row0 % tq == 0 and n % tq == 0 and q_row0 % n == 0
    kv = lambda b, j, i: (j, q_row0 // n + b, 0, 0)
    meta = lambda b, j, i: (kv_batch0 + b, j, 0, 0)
    in_specs = [
        pl.BlockSpec((GROUP, tq // t, QK_ROWS, t), lambda b, j, i: (j, (q_row0 + b * n) // tq + i, 0, 0)),
        pl.BlockSpec((1, nchunks, QK_ROWS, t), kv),
        pl.BlockSpec((1, nchunks, V_LANES, t), kv),
        pl.BlockSpec((1, 1, N_META, QK_ROWS), meta),
        pl.BlockSpec((1, 1, V_LANES, N_META), meta),
        pl.BlockSpec((1, 1, QK_ROWS, V_LANES), meta),
        pl.BlockSpec(memory_space=pl.ANY),
    ]
    args = [qbt, kbt, vbt, kmeta, vmetat, qmeta, prev_tail]
    aliases = {6: 1}
    if prev is not None:
        in_specs.insert(6, pl.BlockSpec(memory_space=pl.ANY))
        args.insert(6, prev)
        aliases = {6: 0, 7: 1}
    return pl.pallas_call(
        functools.partial(_global_fused_body, nchunks, tq, prev is not None),
        grid=(batches, N_KV, qt),
        in_specs=in_specs,
        out_specs=(pl.BlockSpec((tq, GROUP * HEAD_DIM), lambda b, j, i: (q_row0 // tq + b * qt + i, j)),
                   pl.BlockSpec((N_META, GROUP * HEAD_DIM), lambda b, j, i: (kv_batch0 + b, j))),
        out_shape=(jax.ShapeDtypeStruct((out_rows, Q_COLS), _BF),
                   jax.ShapeDtypeStruct(prev_tail.shape, _BF)),
        scratch_shapes=[pltpu.VMEM((QK_ROWS, cols), _BF), pltpu.VMEM((V_LANES, cols), _F32),
                        pltpu.VMEM((V_LANES, V_LANES), _F32)],
        input_output_aliases=aliases,
        compiler_params=_params(("parallel", "parallel", "arbitrary")),
        name=f"global_attn_fused_tq{tq}_n{n}",
    )(*args)


def _stack_queries(q_ref, qt_ref):
    if q_ref.shape[0] == GROUP:
        qt_ref[...] = jnp.concatenate([q_ref[g, cc] for g in range(GROUP) for cc in range(q_ref.shape[1])], axis=1)
    else:
        qt_ref[...] = q_ref[0, 0]


def _write_output(acc_ref, o_ref, tq):
    acc = acc_ref[...]
    o = (acc / acc[HEAD_DIM:HEAD_DIM + 1]).T.astype(_BF)
    for g in range(GROUP):
        o_ref[:, g * HEAD_DIM:(g + 1) * HEAD_DIM] = o[g * tq:(g + 1) * tq, :HEAD_DIM]


def _global_attention(q, kbt, vbt, kmeta, vmetat, prev, out_rows, *, tq, q_row0, q_rows_per_batch, grid_batches,
                      valid_batches, kv_batch0, kv_row0, n, offset_is_bound=False):
    t = TOKEN_TILE
    nchunks = n // t
    qt = q_rows_per_batch // tq
    assert q_row0 % tq == 0 and kv_row0 % n == 0 and q_rows_per_batch % tq == 0
    kvb = lambda b: jnp.minimum(b, valid_batches - 1)
    if tq == N_META:
        cols = V_LANES
        q_spec = pl.BlockSpec((1, 1, QK_ROWS, cols), lambda b, j, i: (kv_batch0 + kvb(b), j, 0, 0))
    else:
        cols = GROUP * tq
        assert tq % t == 0
        q_spec = pl.BlockSpec((GROUP, tq // t, QK_ROWS, t),
                              lambda b, j, i: (j, (q_row0 + b * q_rows_per_batch) // tq + i, 0, 0))
    in_specs = [
        q_spec,
        pl.BlockSpec((1, nchunks, QK_ROWS, t), lambda b, j, i: (j, kv_row0 // n + kvb(b), 0, 0)),
        pl.BlockSpec((1, nchunks, V_LANES, t), lambda b, j, i: (j, kv_row0 // n + kvb(b), 0, 0)),
        pl.BlockSpec((1, 1, N_META, QK_ROWS), lambda b, j, i: (kv_batch0 + kvb(b), j, 0, 0)),
        pl.BlockSpec((1, 1, V_LANES, N_META), lambda b, j, i: (kv_batch0 + kvb(b), j, 0, 0)),
    ]
    args = [q, kbt, vbt, kmeta, vmetat]
    aliases = {}
    body = functools.partial(_global_fast_body if offset_is_bound else _global_body, valid_batches, nchunks, tq)
    score_bufs = [] if offset_is_bound else [pltpu.VMEM((t, cols), _F32), pltpu.VMEM((t, cols), _F32),
                                             pltpu.VMEM((1, cols), _F32)]
    if prev is None:
        body = functools.partial(_global_body_noprev, body)
    else:
        in_specs.append(pl.BlockSpec(memory_space=pl.ANY))
        args.append(prev)
        aliases = {5: 0}
    return pl.pallas_call(
        body,
        grid=(grid_batches, N_KV, qt),
        in_specs=in_specs,
        out_specs=pl.BlockSpec((tq, GROUP * HEAD_DIM), lambda b, j, i: (q_row0 // tq + b * qt + i, j)),
        out_shape=jax.ShapeDtypeStruct((out_rows, Q_COLS), _BF),
        scratch_shapes=[pltpu.VMEM((QK_ROWS, cols), _BF)] + score_bufs + [pltpu.VMEM((V_LANES, cols), _F32)],
        input_output_aliases=aliases,
        compiler_params=_params(("parallel", "parallel", "arbitrary")),
        name=f"global_attn{'_fast' if offset_is_bound else ''}_tq{tq}_n{n}",
    )(*args)


def _global_body_noprev(body, q_ref, kt_ref, vt_ref, kmeta_ref, vmetat_ref, o_ref, *scratch):
    body(q_ref, kt_ref, vt_ref, kmeta_ref, vmetat_ref, None, o_ref, *scratch)


def _mix_body(real_tiles, h_ref, oa_ref, ob_ref, obtail_ref, ga_ref, gb_ref, wa_ref, wb_ref, wo_ref, g_ref, o_ref):
    ob = jnp.where(pl.program_id(0) >= real_tiles, obtail_ref[...], ob_ref[...])
    mix = (ga_ref[...].astype(_F32) * _dot(oa_ref[...], wa_ref[...])
           + gb_ref[...].astype(_F32) * _dot(ob, wb_ref[...]))
    u = _dot(mix.astype(_BF), wo_ref[...])
    ms = jnp.mean(u * u, axis=-1, keepdims=True)
    o_ref[...] = h_ref[...] + u * lax.rsqrt(ms + EPS) * g_ref[...]


def _mix(h, oa, ob, ob_tail, ga, gb, wa, wb, wo, g):
    ntok = h.shape[0]
    t = DENSE_TILE
    real_tiles = (ntok - ob_tail.shape[0]) // t
    assert real_tiles * t + ob_tail.shape[0] == ntok and ob_tail.shape[0] % t == 0
    row = lambda i: (i, 0)
    return pl.pallas_call(
        functools.partial(_mix_body, real_tiles),
        grid=(ntok // t,),
        in_specs=[
            pl.BlockSpec((t, D_MODEL), row),
            pl.BlockSpec((t, Q_COLS), row),
            pl.BlockSpec((t, Q_COLS), lambda i: (jnp.minimum(i, real_tiles - 1), 0)),
            pl.BlockSpec((t, Q_COLS), lambda i: (jnp.maximum(i - real_tiles, 0), 0)),
            pl.BlockSpec((t, D_MODEL), row), pl.BlockSpec((t, D_MODEL), row),
            _const_spec((Q_COLS, D_MODEL)), _const_spec((Q_COLS, D_MODEL)),
            _const_spec((D_MODEL, D_MODEL)), _const_spec((1, D_MODEL)),
        ],
        out_specs=pl.BlockSpec((t, D_MODEL), row),
        out_shape=jax.ShapeDtypeStruct(h.shape, _F32),
        input_output_aliases={0: 0},
        compiler_params=_params(("parallel",)),
        name="branch_mix",
    )(h, oa, ob, ob_tail, ga, gb, wa, wb, wo, g)


_FF_CHUNK = 256


def _ffn_body(h_ref, gpre_ref, wup_ref, wdown_ref, gpost_ref, o_ref, act_ref):
    h = h_ref[...]
    ms = jnp.mean(h * h, axis=-1, keepdims=True)
    xn = (h * lax.rsqrt(ms + EPS) * gpre_ref[...]).astype(_BF)
    for c in range(0, D_FF, _FF_CHUNK):
        a = _dot(xn, wup_ref[:, c:c + _FF_CHUNK])
        b = _dot(xn, wup_ref[:, D_FF + c:D_FF + c + _FF_CHUNK])
        act_ref[:, c:c + _FF_CHUNK] = (a * jax.nn.sigmoid(a) * b).astype(_BF)
    u = _dot(act_ref[...], wdown_ref[...])
    ms = jnp.mean(u * u, axis=-1, keepdims=True)
    o_ref[...] = h + u * lax.rsqrt(ms + EPS) * gpost_ref[...]


def _ffn_last_body(blocks, h_ref, gpre_ref, wup_ref, wdown_ref, gpost_ref, yp_ref, ys_ref, act_ref):
    p, sm = blocks
    i = pl.program_id(0)

    @pl.when(i < p)
    def _():
        _ffn_body(h_ref, gpre_ref, wup_ref, wdown_ref, gpost_ref, yp_ref, act_ref)

    @pl.when(jnp.logical_and(i >= p, i < p + sm))
    def _():
        _ffn_body(h_ref, gpre_ref, wup_ref, wdown_ref, gpost_ref, ys_ref, act_ref)


def _ffn(h, gpre, wup, wdown, gpost, split_rows=None):
    ntok = h.shape[0]
    t = DENSE_TILE
    row = lambda i: (i, 0)
    if split_rows is None:
        body, aliases = _ffn_body, {0: 0}
        out_specs = pl.BlockSpec((t, D_MODEL), row)
        out_shape = jax.ShapeDtypeStruct(h.shape, _F32)
    else:
        p, sm = (r // t for r in split_rows)
        assert p * t == split_rows[0] and sm * t == split_rows[1]
        body, aliases = functools.partial(_ffn_last_body, (p, sm)), {}
        out_specs = (pl.BlockSpec((t, D_MODEL), lambda i: (jnp.minimum(i, p - 1), 0)),
                     pl.BlockSpec((t, D_MODEL), lambda i: (jnp.clip(i - p, 0, sm - 1), 0)))
        out_shape = (jax.ShapeDtypeStruct((split_rows[0], D_MODEL), _F32),
                     jax.ShapeDtypeStruct((split_rows[1], D_MODEL), _F32))
    return pl.pallas_call(
        body,
        grid=(ntok // t,),
        in_specs=[
            pl.BlockSpec((t, D_MODEL), row),
            _const_spec((1, D_MODEL)),
            _const_spec((D_MODEL, 2 * D_FF)),
            _const_spec((D_FF, D_MODEL)),
            _const_spec((1, D_MODEL)),
        ],
        out_specs=out_specs,
        out_shape=out_shape,
        scratch_shapes=[pltpu.VMEM((t, D_FF), _BF)],
        input_output_aliases=aliases,
        compiler_params=_params(("arbitrary",)),
        name="swiglu_ffn" if split_rows is None else "swiglu_ffn_last",
    )(h, gpre, wup, wdown, gpost)


def _rope_tables(layout, ntok):
    n_real, b1, n1, b2, n2 = layout
    idx = np.zeros((ntok,), np.int64)
    idx[:b1 * n1] = np.arange(b1 * n1) % n1
    idx[b1 * n1:n_real] = np.arange(b2 * n2) % n2
    rows = (idx // GRID_W).astype(np.float32)
    cols = (idx % GRID_W).astype(np.float32)
    rows[n_real:] = 0.0
    cols[n_real:] = 0.0
    freqs = ROPE_BASE ** (-jnp.arange(ROPE_FREQS, dtype=_F32) / ROPE_FREQS)
    ang_r = jnp.asarray(rows)[:, None] * freqs[None, :]
    ang_c = jnp.asarray(cols)[:, None] * freqs[None, :]
    cr, sr, cc, sc = jnp.cos(ang_r), jnp.sin(ang_r), jnp.cos(ang_c), jnp.sin(ang_c)
    cos64 = jnp.concatenate([cr, cr, cc, cc], axis=1)
    sin64 = jnp.concatenate([-sr, sr, -sc, sc], axis=1)
    return cos64.T, sin64.T


def _window_bias():
    slopes = 2.0 ** (-8.0 * np.arange(1, N_HEADS + 1, dtype=np.float64) / N_HEADS)
    rel = np.arange(BLOCK)[:, None] - (np.arange(3 * BLOCK) - BLOCK)[None, :]
    dist = np.abs(rel)
    band = dist <= BLOCK
    base = np.where(band[None], -slopes[:, None, None] * dist[None].astype(np.float64) * LOG2E, NEG_INF)
    no_prev = base.copy()
    no_prev[:, :, :BLOCK] = NEG_INF
    no_next = base.copy()
    no_next[:, :, 2 * BLOCK:3 * BLOCK] = NEG_INF
    out = np.stack([base, no_prev, no_next]).reshape(3, N_KV, GROUP * BLOCK, 3 * BLOCK)
    return jnp.asarray(np.swapaxes(out, -1, -2), _F32)


def _rearranged_w_in(w_in):
    o = np.cumsum([0, Q_COLS, KV_COLS, KV_COLS, Q_COLS, KV_COLS, KV_COLS, D_MODEL, D_MODEL])
    qa, ka, va, qb, kb, vb, ga, gb = [w_in[..., o[i]:o[i + 1]] for i in range(8)]

    def pad_v(v):
        z = jnp.zeros(v.shape[:-1] + (V_LANES - HEAD_DIM,), v.dtype)
        return jnp.concatenate([v[..., :HEAD_DIM], z, v[..., HEAD_DIM:], z], axis=-1)

    w_rows = jnp.concatenate([ga, gb], axis=-1).astype(_BF)
    w_t = jnp.swapaxes(jnp.concatenate([ka, kb, pad_v(vb), qb, pad_v(va), qa], axis=-1), -1, -2).astype(_BF)
    return w_rows, w_t


def _tail_features(xt, n_real, nb):
    tail = xt[:, n_real // TOKEN_TILE:]
    tail = jnp.moveaxis(tail, 2, 1).reshape(xt.shape[0], xt.shape[2], -1)[:, :, :nb * N_META]
    return tail.reshape(xt.shape[0], xt.shape[2], nb, N_META)


def kernel(x_prompt, x_sample, meta_tokens, g_mix_pre, g_mix_post, g_ffn_pre, g_ffn_post, w_in, q_norm_b,
           k_norm_b, sink_a, w_branch_a, w_branch_b, w_out, w_ffn_up, w_ffn_down):
    b1, n1, _ = x_prompt.shape
    b2, n2, _ = x_sample.shape
    depth = w_in.shape[0]
    t = TOKEN_TILE
    assert n1 % t == 0 and n2 % t == 0 and (b1 * n1) % n2 == 0 and n1 % GRID_W == 0 and n2 % GRID_W == 0
    assert (b1 * n1) % DENSE_TILE == 0 and (b2 * n2) % DENSE_TILE == 0
    nb = b1 + b2
    n_real = b1 * n1 + b2 * n2
    ntok = -(-(n_real + nb * N_META) // DENSE_TILE) * DENSE_TILE
    tail = ntok - n_real
    assert tail % t == 0
    layout = (n_real, b1, n1, b2, n2)

    h = (x_prompt.reshape(b1 * n1, D_MODEL).astype(_F32), x_sample.reshape(b2 * n2, D_MODEL).astype(_F32),
         jnp.concatenate([jnp.tile(meta_tokens.astype(_F32), (nb, 1)),
                          jnp.zeros((tail - nb * N_META, D_MODEL), _F32)], axis=0))

    cost, sint = _rope_tables(layout, ntok)
    bias = _window_bias()
    w_rows, w_t = _rearranged_w_in(w_in)
    wa, wb, wo = w_branch_a.astype(_BF), w_branch_b.astype(_BF), w_out.astype(_BF)
    wup, wdown = w_ffn_up.astype(_BF), w_ffn_down.astype(_BF)
    row = lambda g: g.reshape(1, -1).astype(_F32)
    col = lambda g: g.reshape(-1, 1).astype(_F32)

    for l in range(depth):
        bound = (HEAD_DIM * SCALE * LOG2E * 1.02) * jnp.max(jnp.abs(q_norm_b[l])) * jnp.max(jnp.abs(k_norm_b[l]))
        unit = (jnp.arange(QK_ROWS - HEAD_DIM) == 0).astype(_F32).reshape(-1, 1)
        proj = _in_proj(h, row(g_mix_pre[l]), w_rows[l], w_t[l], cost, sint, col(q_norm_b[l]), col(k_norm_b[l]),
                        -bound.astype(_F32) * unit, unit)
        if l == 0:
            *proj, h = proj
        qat, qbt, kat, kbt, vat, vbt, ga, gb = proj
        sink = sink_a[l].astype(_F32) * LOG2E
        ka_tail, va_tail = _tail_features(kat, n_real, nb), _tail_features(vat, n_real, nb)
        kameta = jnp.transpose(ka_tail, (2, 0, 1, 3))
        vameta = jnp.transpose(va_tail, (2, 0, 3, 1))
        vametat = jnp.transpose(va_tail, (2, 0, 1, 3))
        qa_meta = jnp.transpose(_tail_features(qat, n_real, nb), (2, 3, 0, 1)).reshape(nb * N_META, Q_COLS)
        qa_meta = jnp.pad(qa_meta, ((0, tail - nb * N_META), (0, 0)))

        oa = _window_meta_attention(sink, qa_meta, kat, vat, kameta, vameta, layout, ntok)
        oa = _window_attention(sink, qat, kat, vat, kameta, vametat, bias, oa, layout)

        kbmeta = jnp.transpose(_tail_features(kbt, n_real, nb), (2, 0, 3, 1))
        vbmetat = jnp.transpose(_tail_features(vbt, n_real, nb), (2, 0, 1, 3))
        qmeta = _tail_features(qbt, n_real, nb).reshape(N_KV, GROUP, QK_ROWS, nb, N_META)
        qmeta = jnp.transpose(qmeta, (3, 0, 2, 1, 4)).reshape(nb, N_KV, QK_ROWS, GROUP * N_META)
        qmeta = jnp.pad(qmeta, ((0, 0), (0, 0), (0, 0), (0, V_LANES - GROUP * N_META)))

        glob = functools.partial(_global_attention, kbt=kbt, vbt=vbt, kmeta=kbmeta, vmetat=vbmetat)
        def mixer_b_maxfree():
            out = tails = None
            tails = jnp.zeros((tail, Q_COLS), _BF)
            for row0, batches, batch0, n in ((0, b1, 0, n1), (b1 * n1, b2, b1, n2)):
                out, tails = _global_fused(qbt, kbt, vbt, kbmeta, vbmetat, qmeta, out, tails, ntok, tq=1024,
                                           q_row0=row0, batches=batches, kv_batch0=batch0, n=n)
            return out, tails

        def mixer_b_general():
            tails = glob(qmeta, prev=None, out_rows=tail, tq=N_META, q_row0=0, q_rows_per_batch=N_META,
                         grid_batches=b1, valid_batches=b1, kv_batch0=0, kv_row0=0, n=n1)
            tails = glob(qmeta, prev=tails, out_rows=tail, tq=N_META, q_row0=b1 * N_META, q_rows_per_batch=N_META,
                         grid_batches=tail // N_META - b1, valid_batches=b2, kv_batch0=b1, kv_row0=b1 * n1, n=n2)
            out = glob(qbt, prev=None, out_rows=ntok, tq=512, q_row0=0, q_rows_per_batch=n1, grid_batches=b1,
                       valid_batches=b1, kv_batch0=0, kv_row0=0, n=n1)
            out = glob(qbt, prev=out, out_rows=ntok, tq=512, q_row0=b1 * n1, q_rows_per_batch=n2, grid_batches=b2,
                       valid_batches=b2, kv_batch0=b1, kv_row0=b1 * n1, n=n2)
            return out, tails

        ob, ob_tail = lax.cond(bound <= SAFE_OFFSET_MAX, mixer_b_maxfree, mixer_b_general)

        h = _mix(h, oa, ob, ob_tail, ga, gb, wa[l], wb[l], wo[l], row(g_mix_post[l]))
        h = _ffn(h, row(g_ffn_pre[l]), wup[l], wdown[l], row(g_ffn_post[l]),
                 split_rows=(b1 * n1, b2 * n2) if l == depth - 1 else None)

    y_prompt, y_sample = h
    return (y_prompt.reshape(b1, n1, D_MODEL).astype(x_prompt.dtype),
            y_sample.reshape(b2, n2, D_MODEL).astype(x_sample.dtype))
```

```python
import functools
import math

import jax
import jax.numpy as jnp
import numpy as np
from jax import lax
from jax.experimental import pallas as pl
from jax.experimental.pallas import tpu as pltpu

D_MODEL = 1024
HEAD_DIM = 64
N_HEADS = 8
N_KV = 2
GROUP = N_HEADS // N_KV
Q_COLS = N_HEADS * HEAD_DIM
KV_COLS = N_KV * HEAD_DIM
N_META = 16
BLOCK = 128
GRID_W = 64
ROPE_BASE = 10000.0
ROPE_FREQS = HEAD_DIM // 4
D_FF = 2816
EPS = 1e-6
NEG_INF = -1e30
SCALE = HEAD_DIM ** -0.5
LOG2E = math.log2(math.e)

QK_ROWS = HEAD_DIM + 16
SAFE_OFFSET_MAX = 40.0
V_LANES = 128
TOKEN_TILE = 512
DENSE_TILE = 1024
WINDOW_TILE = 1024
VMEM_LIMIT = 56 * 1024 * 1024

_C_GA = 0
_C_GB = _C_GA + D_MODEL
_C_END = _C_GB + D_MODEL
_R_KA = 0
_R_KB = _R_KA + KV_COLS
_R_VB = _R_KB + KV_COLS
_R_QB = _R_VB + N_KV * V_LANES
_R_VA = _R_QB + Q_COLS
_R_QA = _R_VA + N_KV * V_LANES
_R_END = _R_QA + Q_COLS

_BF = jnp.bfloat16
_F32 = jnp.float32


def _dot(a, b):
    return jnp.dot(a, b, preferred_element_type=_F32)


def _dot_nt(a, b):
    return lax.dot_general(a, b, (((1,), (1,)), ((), ())), preferred_element_type=_F32)


def _dot_tn(a, b):
    return lax.dot_general(a, b, (((0,), (0,)), ((), ())), preferred_element_type=_F32)


def _params(sem, vmem=VMEM_LIMIT):
    return pltpu.CompilerParams(dimension_semantics=sem, vmem_limit_bytes=vmem)


def _const_spec(shape):
    nd = len(shape)
    return pl.BlockSpec(shape, lambda *_: (0,) * nd, pipeline_mode=pl.Buffered(1))


def _in_proj_body(h_ref, g_ref, w_ref, wt_ref, cost_ref, sint_ref, qg_ref, kg_ref, qx_ref, kx_ref,
                  qat_ref, qbt_ref, kat_ref, kbt_ref, vat_ref, vbt_ref, ga_ref, gb_ref):
    f = ROPE_FREQS
    ones_row = (lax.broadcasted_iota(jnp.int32, (V_LANES, 1), 0) == HEAD_DIM).astype(_F32)
    q_extra = jnp.broadcast_to(qx_ref[...], (QK_ROWS - HEAD_DIM, TOKEN_TILE)).astype(_BF)
    k_extra = jnp.broadcast_to(kx_ref[...], (QK_ROWS - HEAD_DIM, TOKEN_TILE)).astype(_BF)

    for cc in range(h_ref.shape[0] // TOKEN_TILE):
        rows = slice(cc * TOKEN_TILE, (cc + 1) * TOKEN_TILE)
        h = h_ref[rows, :]
        ms = jnp.mean(h * h, axis=-1, keepdims=True)
        xn = (h * lax.rsqrt(ms + EPS) * g_ref[...]).astype(_BF)

        ga_ref[rows, :] = jax.nn.sigmoid(_dot(xn, w_ref[:, _C_GA:_C_GA + D_MODEL])).astype(_BF)
        gb_ref[rows, :] = jax.nn.sigmoid(_dot(xn, w_ref[:, _C_GB:_C_GB + D_MODEL])).astype(_BF)

        tt = _dot_nt(wt_ref[...], xn)
        cost = cost_ref[:, rows]
        sint = sint_ref[:, rows]

        def norm_rope(x, gain):
            x = x * lax.rsqrt(jnp.mean(x * x, axis=0, keepdims=True) + EPS) * gain
            partner = jnp.concatenate([x[f:2 * f], x[0:f], x[3 * f:4 * f], x[2 * f:3 * f]], axis=0)
            return x * cost + partner * sint

        for j in range(N_KV):
            kat_ref[j, cc] = tt[_R_KA + j * HEAD_DIM:_R_KA + (j + 1) * HEAD_DIM].astype(_BF)
            k = norm_rope(tt[_R_KB + j * HEAD_DIM:_R_KB + (j + 1) * HEAD_DIM], kg_ref[...])
            kbt_ref[j, cc, :HEAD_DIM, :] = k.astype(_BF)
            kbt_ref[j, cc, HEAD_DIM:, :] = k_extra
            vat_ref[j, cc] = (tt[_R_VA + j * V_LANES:_R_VA + (j + 1) * V_LANES] + ones_row).astype(_BF)
            vbt_ref[j, cc] = (tt[_R_VB + j * V_LANES:_R_VB + (j + 1) * V_LANES] + ones_row).astype(_BF)
        for hd in range(N_HEADS):
            qa = tt[_R_QA + hd * HEAD_DIM:_R_QA + (hd + 1) * HEAD_DIM] * (SCALE * LOG2E)
            qat_ref[hd, cc] = qa.astype(_BF)
            q = norm_rope(tt[_R_QB + hd * HEAD_DIM:_R_QB + (hd + 1) * HEAD_DIM], qg_ref[...])
            qbt_ref[hd, cc, :HEAD_DIM, :] = (q * (SCALE * LOG2E)).astype(_BF)
            qbt_ref[hd, cc, HEAD_DIM:, :] = q_extra


def _in_proj_first_body(blocks, xp_ref, xs_ref, tail_ref, *refs):
    p, sm = blocks
    i = pl.program_id(0)
    *rest, h_ref = refs
    h_ref[...] = jnp.where(i < p, xp_ref[...], jnp.where(i < p + sm, xs_ref[...], tail_ref[...]))
    _in_proj_body(h_ref, *rest)


def _in_proj(h, g, w, wt, cost, sint, qg, kg, qx, kx):
    first = isinstance(h, tuple)
    t = TOKEN_TILE if first else DENSE_TILE
    ntok = sum(x.shape[0] for x in h) if first else h.shape[0]
    per = t // TOKEN_TILE
    nch = ntok // TOKEN_TILE
    row = lambda i: (i, 0)
    chunk = lambda i: (0, i, 0, 0)
    feat = lambda heads, rows: (jax.ShapeDtypeStruct((heads, nch, rows, TOKEN_TILE), _BF),
                                pl.BlockSpec((heads, per, rows, TOKEN_TILE), chunk))
    outs = [
        feat(N_HEADS, HEAD_DIM),
        feat(N_HEADS, QK_ROWS),
        feat(N_KV, HEAD_DIM),
        feat(N_KV, QK_ROWS),
        feat(N_KV, V_LANES),
        feat(N_KV, V_LANES),
        (jax.ShapeDtypeStruct((ntok, D_MODEL), _BF), pl.BlockSpec((t, D_MODEL), row)),
        (jax.ShapeDtypeStruct((ntok, D_MODEL), _BF), pl.BlockSpec((t, D_MODEL), row)),
    ]
    if first:
        p, sm, tl = (x.shape[0] // t for x in h)
        body = functools.partial(_in_proj_first_body, (p, sm))
        h_specs = [pl.BlockSpec((t, D_MODEL), lambda i: (jnp.minimum(i, p - 1), 0)),
                   pl.BlockSpec((t, D_MODEL), lambda i: (jnp.clip(i - p, 0, sm - 1), 0)),
                   pl.BlockSpec((t, D_MODEL), lambda i: (jnp.clip(i - p - sm, 0, tl - 1), 0))]
        outs.append((jax.ShapeDtypeStruct((ntok, D_MODEL), _F32), pl.BlockSpec((t, D_MODEL), row)))
        h_args = list(h)
    else:
        body, h_specs, h_args = _in_proj_body, [pl.BlockSpec((t, D_MODEL), row)], [h]
    return pl.pallas_call(
        body,
        grid=(ntok // t,),
        in_specs=h_specs + [
            _const_spec((1, D_MODEL)),
            _const_spec((D_MODEL, _C_END)),
            _const_spec((_R_END, D_MODEL)),
            pl.BlockSpec((HEAD_DIM, t), lambda i: (0, i)),
            pl.BlockSpec((HEAD_DIM, t), lambda i: (0, i)),
            _const_spec((HEAD_DIM, 1)),
            _const_spec((HEAD_DIM, 1)),
            _const_spec((QK_ROWS - HEAD_DIM, 1)),
            _const_spec((QK_ROWS - HEAD_DIM, 1)),
        ],
        out_specs=tuple(o[1] for o in outs),
        out_shape=tuple(o[0] for o in outs),
        compiler_params=_params(("parallel",)),
        name="in_proj_first" if first else "in_proj",
    )(*h_args, g, w, wt, cost, sint, qg, kg, qx, kx)


_WINDOW_LOOKAHEAD = 3


def _stack_heads(q, j):
    base = j * GROUP * HEAD_DIM
    return jnp.concatenate([q[:, base + g * HEAD_DIM: base + (g + 1) * HEAD_DIM] for g in range(GROUP)], axis=0)


def _sink_rows(sink_ref, j, rows, lanes):
    return jnp.concatenate([jnp.full((rows, lanes), sink_ref[j * GROUP + g], _F32) for g in range(GROUP)], axis=0)


def _window_body(geom, sink_ref, q_ref, kmain_ref, kprev_ref, knext_ref, vmain_ref, vprev_ref, vnext_ref,
                 kmeta_ref, vmeta_ref, bias_ref, _tail_ref, o_ref):
    t1, tpb1, tpb2 = geom
    t = pl.program_id(0)
    in_prompt = t < t1
    is_first = jnp.where(in_prompt, t % tpb1 == 0, (t - t1) % tpb2 == 0)
    is_last = jnp.where(in_prompt, t % tpb1 == tpb1 - 1, (t - t1) % tpb2 == tpb2 - 1)
    nblk = WINDOW_TILE // BLOCK
    per_chunk = TOKEN_TILE // BLOCK

    def softmax_values(j, r, s, sm, vwin, vmeta, sink):
        m = jnp.maximum(jnp.maximum(jnp.max(s, axis=0, keepdims=True), jnp.max(sm, axis=0, keepdims=True)), sink)
        acc = (_dot(vwin, jnp.exp2(s - m).astype(_BF))
               + _dot(vmeta, jnp.exp2(sm - m).astype(_BF)))
        l = acc[HEAD_DIM:HEAD_DIM + 1] + jnp.exp2(sink - m)
        o = (acc / l).T.astype(_BF)
        for g in range(GROUP):
            c0 = (j * GROUP + g) * HEAD_DIM
            o_ref[r * BLOCK:(r + 1) * BLOCK, c0:c0 + HEAD_DIM] = o[g * BLOCK:(g + 1) * BLOCK, :HEAD_DIM]

    pending = []
    for j in range(N_KV):
        chunks = range(kmain_ref.shape[1])
        kcat = jnp.concatenate([kprev_ref[j, 0]] + [kmain_ref[j, cc] for cc in chunks] + [knext_ref[j, 0]], axis=1)
        vcat = jnp.concatenate([vprev_ref[j, 0]] + [vmain_ref[j, cc] for cc in chunks] + [vnext_ref[j, 0]], axis=1)
        kmeta = kmeta_ref[0, j]
        vmeta = vmeta_ref[0, j]
        sink = jnp.concatenate([jnp.full((1, BLOCK), sink_ref[j * GROUP + g], _F32) for g in range(GROUP)], axis=1)
        for r in range(nblk):
            variant = 0
            if r == 0:
                variant = jnp.where(is_first, 1, 0)
            if r == nblk - 1:
                variant = jnp.where(is_last, 2, variant)
            c0 = (r % per_chunk) * BLOCK
            qt = jnp.concatenate([q_ref[j * GROUP + g, r // per_chunk][:, c0:c0 + BLOCK] for g in range(GROUP)],
                                 axis=1)
            s = _dot_tn(kcat[:, r * BLOCK:(r + 3) * BLOCK], qt) + bias_ref[variant, j]
            sm = _dot_tn(kmeta, qt)
            pending.append((j, r, s, sm, vcat[:, r * BLOCK:(r + 3) * BLOCK], vmeta, sink))
            if len(pending) > _WINDOW_LOOKAHEAD:
                softmax_values(*pending.pop(0))
    for item in pending:
        softmax_values(*item)


def _window_attention(sink, qat, kat, vat, kmeta, vmetat, bias, tail, layout):
    n_real, b1, n1, b2, n2 = layout
    t = WINDOW_TILE
    per = t // TOKEN_TILE
    nch = kat.shape[1]
    assert n1 % t == 0 and n2 % t == 0
    t1 = b1 * n1 // t
    tpb1, tpb2 = n1 // t, n2 // t
    sub = TOKEN_TILE // BLOCK

    def bid(i):
        return jnp.where(i < t1, i // tpb1, b1 + (i - t1) // tpb2)

    main = lambda i: (0, i, 0, 0)
    prev = lambda i: (0, jnp.maximum(i * per - 1, 0), 0, sub - 1)
    nxt = lambda i: (0, jnp.minimum((i + 1) * per, nch - 1), 0, 0)
    return pl.pallas_call(
        functools.partial(_window_body, (t1, tpb1, tpb2)),
        grid=(n_real // t,),
        in_specs=[
            pl.BlockSpec(memory_space=pltpu.SMEM),
            pl.BlockSpec((N_HEADS, per, HEAD_DIM, TOKEN_TILE), main),
            pl.BlockSpec((N_KV, per, HEAD_DIM, TOKEN_TILE), main),
            pl.BlockSpec((N_KV, 1, HEAD_DIM, BLOCK), prev),
            pl.BlockSpec((N_KV, 1, HEAD_DIM, BLOCK), nxt),
            pl.BlockSpec((N_KV, per, V_LANES, TOKEN_TILE), main),
            pl.BlockSpec((N_KV, 1, V_LANES, BLOCK), prev),
            pl.BlockSpec((N_KV, 1, V_LANES, BLOCK), nxt),
            pl.BlockSpec((1, N_KV, HEAD_DIM, N_META), lambda i: (bid(i), 0, 0, 0)),
            pl.BlockSpec((1, N_KV, V_LANES, N_META), lambda i: (bid(i), 0, 0, 0)),
            _const_spec((3, N_KV, 3 * BLOCK, GROUP * BLOCK)),
            pl.BlockSpec(memory_space=pl.ANY),
        ],
        out_specs=pl.BlockSpec((t, Q_COLS), lambda i: (i, 0)),
        out_shape=jax.ShapeDtypeStruct(tail.shape, _BF),
        input_output_aliases={11: 0},
        compiler_params=_params(("parallel",)),
        name="window_attn",
    )(sink, qat, kat, kat, kat, vat, vat, vat, kmeta, vmetat, bias, tail)


def _window_meta_body(nb, sink_ref, q_ref, kfirst_ref, vfirst_ref, kmeta_ref, vmeta_ref, o_ref):
    b = pl.program_id(0)

    @pl.when(b < nb)
    def _():
        q_all = q_ref[...]
        for j in range(N_KV):
            qs = _stack_heads(q_all, j)
            sink = _sink_rows(sink_ref, j, N_META, 1)
            sm = _dot(qs, kmeta_ref[0, j])
            sf = _dot(qs, kfirst_ref[j, 0])
            m = jnp.maximum(jnp.maximum(sm.max(axis=1, keepdims=True), sf.max(axis=1, keepdims=True)), sink)
            pm = jnp.exp2(sm - m).astype(_BF)
            pf = jnp.exp2(sf - m).astype(_BF)
            acc = _dot(pm, vmeta_ref[0, j]) + _dot_nt(pf, vfirst_ref[j, 0])
            l = acc[:, HEAD_DIM:HEAD_DIM + 1] + jnp.exp2(sink - m)
            o = (acc[:, :HEAD_DIM] / l).astype(_BF)
            for g in range(GROUP):
                c0 = (j * GROUP + g) * HEAD_DIM
                o_ref[:, c0:c0 + HEAD_DIM] = o[g * N_META:(g + 1) * N_META]

    @pl.when(b >= nb)
    def _():
        o_ref[...] = jnp.zeros(o_ref.shape, o_ref.dtype)


def _window_meta_attention(sink, q_rows, kat, vat, kmeta, vmeta, layout, out_rows):
    n_real, b1, n1, b2, n2 = layout
    nb = b1 + b2
    t = TOKEN_TILE

    def start_chunk(b):
        bc = jnp.minimum(b, nb - 1)
        return jnp.where(bc < b1, bc * (n1 // t), b1 * (n1 // t) + (bc - b1) * (n2 // t))

    return pl.pallas_call(
        functools.partial(_window_meta_body, nb),
        grid=((out_rows - n_real) // N_META,),
        in_specs=[
            pl.BlockSpec(memory_space=pltpu.SMEM),
            pl.BlockSpec((N_META, Q_COLS), lambda b: (b, 0)),
            pl.BlockSpec((N_KV, 1, HEAD_DIM, BLOCK), lambda b: (0, start_chunk(b), 0, 0)),
            pl.BlockSpec((N_KV, 1, V_LANES, BLOCK), lambda b: (0, start_chunk(b), 0, 0)),
            pl.BlockSpec((1, N_KV, HEAD_DIM, N_META), lambda b: (jnp.minimum(b, nb - 1), 0, 0, 0)),
            pl.BlockSpec((1, N_KV, N_META, V_LANES), lambda b: (jnp.minimum(b, nb - 1), 0, 0, 0)),
        ],
        out_specs=pl.BlockSpec((N_META, Q_COLS), lambda b: (n_real // N_META + b, 0)),
        out_shape=jax.ShapeDtypeStruct((out_rows, Q_COLS), _BF),
        compiler_params=_params(("arbitrary",)),
        name="window_meta_attn",
    )(sink, q_rows, kat, vat, kmeta, vmeta)


_KEY_UNROLL = 4
_QUERY_BLOCK = 256
_SCORE_LOOKAHEAD = 3


def _global_body(nvalid, nchunks, tq, q_ref, kt_ref, vt_ref, kmeta_ref, vmetat_ref, _prev_ref, o_ref,
                 qt_ref, s0_ref, s1_ref, m_ref, acc_ref):
    b = pl.program_id(0)

    cols = qt_ref.shape[1]
    blocks = [slice(c0, min(c0 + _QUERY_BLOCK, cols)) for c0 in range(0, cols, _QUERY_BLOCK)]

    def scores(c, s_ref):
        for sl in blocks:
            s_ref[:, sl] = _dot_tn(kt_ref[0, c], qt_ref[:, sl])

    def softmax_pv(c, s_ref):
        for sl in blocks:
            s = s_ref[:, sl]
            m_prev = m_ref[:, sl]
            m_new = jnp.maximum(m_prev, jnp.max(s, axis=0, keepdims=True))
            p = jnp.exp2(s - m_new).astype(_BF)
            acc_ref[:, sl] = jnp.exp2(m_prev - m_new) * acc_ref[:, sl] + _dot(vt_ref[0, c], p)
            m_ref[:, sl] = m_new

    @pl.when(b < nvalid)
    def _():
        _stack_queries(q_ref, qt_ref)
        sm = _dot(kmeta_ref[0, 0], qt_ref[...])
        m0 = jnp.max(sm, axis=0, keepdims=True)
        m_ref[...] = m0
        acc_ref[...] = _dot(vmetat_ref[0, 0], jnp.exp2(sm - m0).astype(_BF))
        bufs = (s0_ref, s1_ref)
        scores(0, s0_ref)

        unroll = _KEY_UNROLL if nchunks > _KEY_UNROLL else 2

        def group(i, carry):
            c0 = unroll * i
            for u in range(unroll):
                scores(c0 + u + 1, bufs[(u + 1) % 2])
                softmax_pv(c0 + u, bufs[u % 2])
            return carry

        full = (nchunks - 1) // unroll
        lax.fori_loop(0, full, group, 0)
        for c in range(full * unroll, nchunks):
            if c + 1 < nchunks:
                scores(c + 1, bufs[(c + 1) % 2])
            softmax_pv(c, bufs[c % 2])
        _write_output(acc_ref, o_ref, tq)

    @pl.when(b >= nvalid)
    def _():
        o_ref[...] = jnp.zeros(o_ref.shape, o_ref.dtype)


def _maxfree_chunks(kt_ref, vt_ref, q_ref, acc_ref, cs):
    cols = q_ref.shape[1]
    blocks = [slice(c0, min(c0 + _QUERY_BLOCK, cols)) for c0 in range(0, cols, _QUERY_BLOCK)]

    def values(c, sl, s):
        acc_ref[:, sl] += _dot(vt_ref[0, c], jnp.exp2(s).astype(_BF))

    pending = []
    for c in cs:
        for sl in blocks:
            pending.append((c, sl, _dot_tn(kt_ref[0, c], q_ref[:, sl])))
            if len(pending) > _SCORE_LOOKAHEAD:
                values(*pending.pop(0))
    for item in pending:
        values(*item)


def _maxfree_attend(nchunks, kt_ref, vt_ref, kmeta_ref, vmetat_ref, q_ref, acc_ref):
    acc_ref[...] = _dot(vmetat_ref[0, 0], jnp.exp2(_dot(kmeta_ref[0, 0], q_ref[...])).astype(_BF))

    def group(i, carry):
        _maxfree_chunks(kt_ref, vt_ref, q_ref, acc_ref, [_KEY_UNROLL * i + u for u in range(_KEY_UNROLL)])
        return carry

    full = nchunks // _KEY_UNROLL
    lax.fori_loop(0, full, group, 0)
    if full * _KEY_UNROLL < nchunks:
        _maxfree_chunks(kt_ref, vt_ref, q_ref, acc_ref, range(full * _KEY_UNROLL, nchunks))


def _global_fused_body(nchunks, tq, has_prev, q_ref, kt_ref, vt_ref, kmeta_ref, vmetat_ref, qm_ref, *refs):
    o_ref, otail_ref, qt_ref, acc_ref, accm_ref = refs[2 if has_prev else 1:]
    _stack_queries(q_ref, qt_ref)
    _maxfree_attend(nchunks, kt_ref, vt_ref, kmeta_ref, vmetat_ref, qt_ref, acc_ref)
    _write_output(acc_ref, o_ref, tq)

    @pl.when(pl.program_id(2) == 0)
    def _():
        _maxfree_attend(nchunks, kt_ref, vt_ref, kmeta_ref, vmetat_ref, qm_ref.at[0, 0], accm_ref)
        _write_output(accm_ref, otail_ref, N_META)


def _global_fused(qbt, kbt, vbt, kmeta, vmetat, qmeta, prev, prev_tail, out_rows, *, tq, q_row0, batches,
                  kv_batch0, n):
    t = TOKEN_TILE
    nchunks = n // t
    qt = n // tq
    cols = GROUP * tq
    assert tq % t == 0 and q_row0 % tq == 0 and n % tq == 0 and q_row0 % n == 0
    kv = lambda b, j, i: (j, q_row0 // n + b, 0, 0)
    meta = lambda b, j, i: (kv_batch0 + b, j, 0, 0)
    in_specs = [
        pl.BlockSpec((GROUP, tq // t, QK_ROWS, t), lambda b, j, i: (j, (q_row0 + b * n) // tq + i, 0, 0)),
        pl.BlockSpec((1, nchunks, QK_ROWS, t), kv),
        pl.BlockSpec((1, nchunks, V_LANES, t), kv),
        pl.BlockSpec((1, 1, N_META, QK_ROWS), meta),
        pl.BlockSpec((1, 1, V_LANES, N_META), meta),
        pl.BlockSpec((1, 1, QK_ROWS, V_LANES), meta),
        pl.BlockSpec(memory_space=pl.ANY),
    ]
    args = [qbt, kbt, vbt, kmeta, vmetat, qmeta, prev_tail]
    aliases = {6: 1}
    if prev is not None:
        in_specs.insert(6, pl.BlockSpec(memory_space=pl.ANY))
        args.insert(6, prev)
        aliases = {6: 0, 7: 1}
    return pl.pallas_call(
        functools.partial(_global_fused_body, nchunks, tq, prev is not None),
        grid=(batches, N_KV, qt),
        in_specs=in_specs,
        out_specs=(pl.BlockSpec((tq, GROUP * HEAD_DIM), lambda b, j, i: (q_row0 // tq + b * qt + i, j)),
                   pl.BlockSpec((N_META, GROUP * HEAD_DIM), lambda b, j, i: (kv_batch0 + b, j))),
        out_shape=(jax.ShapeDtypeStruct((out_rows, Q_COLS), _BF),
                   jax.ShapeDtypeStruct(prev_tail.shape, _BF)),
        scratch_shapes=[pltpu.VMEM((QK_ROWS, cols), _BF), pltpu.VMEM((V_LANES, cols), _F32),
                        pltpu.VMEM((V_LANES, V_LANES), _F32)],
        input_output_aliases=aliases,
        compiler_params=_params(("parallel", "parallel", "arbitrary")),
        name=f"global_attn_fused_tq{tq}_n{n}",
    )(*args)


def _stack_queries(q_ref, qt_ref):
    if q_ref.shape[0] == GROUP:
        qt_ref[...] = jnp.concatenate([q_ref[g, cc] for g in range(GROUP) for cc in range(q_ref.shape[1])], axis=1)
    else:
        qt_ref[...] = q_ref[0, 0]


def _write_output(acc_ref, o_ref, tq):
    acc = acc_ref[...]
    o = (acc / acc[HEAD_DIM:HEAD_DIM + 1]).T.astype(_BF)
    for g in range(GROUP):
        o_ref[:, g * HEAD_DIM:(g + 1) * HEAD_DIM] = o[g * tq:(g + 1) * tq, :HEAD_DIM]


def _global_attention(q, kbt, vbt, kmeta, vmetat, prev, out_rows, *, tq, q_row0, q_rows_per_batch, grid_batches,
                      valid_batches, kv_batch0, kv_row0, n):
    t = TOKEN_TILE
    nchunks = n // t
    qt = q_rows_per_batch // tq
    assert q_row0 % tq == 0 and kv_row0 % n == 0 and q_rows_per_batch % tq == 0
    kvb = lambda b: jnp.minimum(b, valid_batches - 1)
    if tq == N_META:
        cols = V_LANES
        q_spec = pl.BlockSpec((1, 1, QK_ROWS, cols), lambda b, j, i: (kv_batch0 + kvb(b), j, 0, 0))
    else:
        cols = GROUP * tq
        assert tq % t == 0
        q_spec = pl.BlockSpec((GROUP, tq // t, QK_ROWS, t),
                              lambda b, j, i: (j, (q_row0 + b * q_rows_per_batch) // tq + i, 0, 0))
    in_specs = [
        q_spec,
        pl.BlockSpec((1, nchunks, QK_ROWS, t), lambda b, j, i: (j, kv_row0 // n + kvb(b), 0, 0)),
        pl.BlockSpec((1, nchunks, V_LANES, t), lambda b, j, i: (j, kv_row0 // n + kvb(b), 0, 0)),
        pl.BlockSpec((1, 1, N_META, QK_ROWS), lambda b, j, i: (kv_batch0 + kvb(b), j, 0, 0)),
        pl.BlockSpec((1, 1, V_LANES, N_META), lambda b, j, i: (kv_batch0 + kvb(b), j, 0, 0)),
    ]
    args = [q, kbt, vbt, kmeta, vmetat]
    aliases = {}
    body = functools.partial(_global_body, valid_batches, nchunks, tq)
    if prev is None:
        body = functools.partial(_global_body_noprev, body)
    else:
        in_specs.append(pl.BlockSpec(memory_space=pl.ANY))
        args.append(prev)
        aliases = {5: 0}
    return pl.pallas_call(
        body,
        grid=(grid_batches, N_KV, qt),
        in_specs=in_specs,
        out_specs=pl.BlockSpec((tq, GROUP * HEAD_DIM), lambda b, j, i: (q_row0 // tq + b * qt + i, j)),
        out_shape=jax.ShapeDtypeStruct((out_rows, Q_COLS), _BF),
        scratch_shapes=[pltpu.VMEM((QK_ROWS, cols), _BF), pltpu.VMEM((t, cols), _F32), pltpu.VMEM((t, cols), _F32),
                        pltpu.VMEM((1, cols), _F32), pltpu.VMEM((V_LANES, cols), _F32)],
        input_output_aliases=aliases,
        compiler_params=_params(("parallel", "parallel", "arbitrary")),
        name=f"global_attn_tq{tq}_n{n}",
    )(*args)


def _global_body_noprev(body, q_ref, kt_ref, vt_ref, kmeta_ref, vmetat_ref, o_ref, *scratch):
    body(q_ref, kt_ref, vt_ref, kmeta_ref, vmetat_ref, None, o_ref, *scratch)


def _mix_body(real_tiles, h_ref, oa_ref, ob_ref, obtail_ref, ga_ref, gb_ref, wa_ref, wb_ref, wo_ref, g_ref, o_ref):
    ob = jnp.where(pl.program_id(0) >= real_tiles, obtail_ref[...], ob_ref[...])
    mix = (ga_ref[...].astype(_F32) * _dot(oa_ref[...], wa_ref[...])
           + gb_ref[...].astype(_F32) * _dot(ob, wb_ref[...]))
    u = _dot(mix.astype(_BF), wo_ref[...])
    ms = jnp.mean(u * u, axis=-1, keepdims=True)
    o_ref[...] = h_ref[...] + u * lax.rsqrt(ms + EPS) * g_ref[...]


def _mix(h, oa, ob, ob_tail, ga, gb, wa, wb, wo, g):
    ntok = h.shape[0]
    t = DENSE_TILE
    real_tiles = (ntok - ob_tail.shape[0]) // t
    assert real_tiles * t + ob_tail.shape[0] == ntok and ob_tail.shape[0] % t == 0
    row = lambda i: (i, 0)
    return pl.pallas_call(
        functools.partial(_mix_body, real_tiles),
        grid=(ntok // t,),
        in_specs=[
            pl.BlockSpec((t, D_MODEL), row),
            pl.BlockSpec((t, Q_COLS), row),
            pl.BlockSpec((t, Q_COLS), lambda i: (jnp.minimum(i, real_tiles - 1), 0)),
            pl.BlockSpec((t, Q_COLS), lambda i: (jnp.maximum(i - real_tiles, 0), 0)),
            pl.BlockSpec((t, D_MODEL), row), pl.BlockSpec((t, D_MODEL), row),
            _const_spec((Q_COLS, D_MODEL)), _const_spec((Q_COLS, D_MODEL)),
            _const_spec((D_MODEL, D_MODEL)), _const_spec((1, D_MODEL)),
        ],
        out_specs=pl.BlockSpec((t, D_MODEL), row),
        out_shape=jax.ShapeDtypeStruct(h.shape, _F32),
        input_output_aliases={0: 0},
        compiler_params=_params(("parallel",)),
        name="branch_mix",
    )(h, oa, ob, ob_tail, ga, gb, wa, wb, wo, g)


_FF_CHUNK = 256


def _ffn_body(h_ref, gpre_ref, wup_ref, wdown_ref, gpost_ref, o_ref, act_ref):
    h = h_ref[...]
    ms = jnp.mean(h * h, axis=-1, keepdims=True)
    xn = (h * lax.rsqrt(ms + EPS) * gpre_ref[...]).astype(_BF)
    for c in range(0, D_FF, _FF_CHUNK):
        a = _dot(xn, wup_ref[:, c:c + _FF_CHUNK])
        b = _dot(xn, wup_ref[:, D_FF + c:D_FF + c + _FF_CHUNK])
        act_ref[:, c:c + _FF_CHUNK] = (a * jax.nn.sigmoid(a) * b).astype(_BF)
    u = _dot(act_ref[...], wdown_ref[...])
    ms = jnp.mean(u * u, axis=-1, keepdims=True)
    o_ref[...] = h + u * lax.rsqrt(ms + EPS) * gpost_ref[...]


def _ffn_last_body(blocks, h_ref, gpre_ref, wup_ref, wdown_ref, gpost_ref, yp_ref, ys_ref, act_ref):
    p, sm = blocks
    i = pl.program_id(0)

    @pl.when(i < p)
    def _():
        _ffn_body(h_ref, gpre_ref, wup_ref, wdown_ref, gpost_ref, yp_ref, act_ref)

    @pl.when(jnp.logical_and(i >= p, i < p + sm))
    def _():
        _ffn_body(h_ref, gpre_ref, wup_ref, wdown_ref, gpost_ref, ys_ref, act_ref)


def _ffn(h, gpre, wup, wdown, gpost, split_rows=None):
    ntok = h.shape[0]
    t = DENSE_TILE
    row = lambda i: (i, 0)
    if split_rows is None:
        body, aliases = _ffn_body, {0: 0}
        out_specs = pl.BlockSpec((t, D_MODEL), row)
        out_shape = jax.ShapeDtypeStruct(h.shape, _F32)
    else:
        p, sm = (r // t for r in split_rows)
        assert p * t == split_rows[0] and sm * t == split_rows[1]
        body, aliases = functools.partial(_ffn_last_body, (p, sm)), {}
        out_specs = (pl.BlockSpec((t, D_MODEL), lambda i: (jnp.minimum(i, p - 1), 0)),
                     pl.BlockSpec((t, D_MODEL), lambda i: (jnp.clip(i - p, 0, sm - 1), 0)))
        out_shape = (jax.ShapeDtypeStruct((split_rows[0], D_MODEL), _F32),
                     jax.ShapeDtypeStruct((split_rows[1], D_MODEL), _F32))
    return pl.pallas_call(
        body,
        grid=(ntok // t,),
        in_specs=[
            pl.BlockSpec((t, D_MODEL), row),
            _const_spec((1, D_MODEL)),
            _const_spec((D_MODEL, 2 * D_FF)),
            _const_spec((D_FF, D_MODEL)),
            _const_spec((1, D_MODEL)),
        ],
        out_specs=out_specs,
        out_shape=out_shape,
        scratch_shapes=[pltpu.VMEM((t, D_FF), _BF)],
        input_output_aliases=aliases,
        compiler_params=_params(("arbitrary",)),
        name="swiglu_ffn" if split_rows is None else "swiglu_ffn_last",
    )(h, gpre, wup, wdown, gpost)


def _rope_tables(layout, ntok):
    n_real, b1, n1, b2, n2 = layout
    idx = np.zeros((ntok,), np.int64)
    idx[:b1 * n1] = np.arange(b1 * n1) % n1
    idx[b1 * n1:n_real] = np.arange(b2 * n2) % n2
    rows = (idx // GRID_W).astype(np.float32)
    cols = (idx % GRID_W).astype(np.float32)
    rows[n_real:] = 0.0
    cols[n_real:] = 0.0
    freqs = ROPE_BASE ** (-jnp.arange(ROPE_FREQS, dtype=_F32) / ROPE_FREQS)
    ang_r = jnp.asarray(rows)[:, None] * freqs[None, :]
    ang_c = jnp.asarray(cols)[:, None] * freqs[None, :]
    cr, sr, cc, sc = jnp.cos(ang_r), jnp.sin(ang_r), jnp.cos(ang_c), jnp.sin(ang_c)
    cos64 = jnp.concatenate([cr, cr, cc, cc], axis=1)
    sin64 = jnp.concatenate([-sr, sr, -sc, sc], axis=1)
    return cos64.T, sin64.T


def _window_bias():
    slopes = 2.0 ** (-8.0 * np.arange(1, N_HEADS + 1, dtype=np.float64) / N_HEADS)
    rel = np.arange(BLOCK)[:, None] - (np.arange(3 * BLOCK) - BLOCK)[None, :]
    dist = np.abs(rel)
    band = dist <= BLOCK
    base = np.where(band[None], -slopes[:, None, None] * dist[None].astype(np.float64) * LOG2E, NEG_INF)
    no_prev = base.copy()
    no_prev[:, :, :BLOCK] = NEG_INF
    no_next = base.copy()
    no_next[:, :, 2 * BLOCK:3 * BLOCK] = NEG_INF
    out = np.stack([base, no_prev, no_next]).reshape(3, N_KV, GROUP * BLOCK, 3 * BLOCK)
    return jnp.asarray(np.swapaxes(out, -1, -2), _F32)


def _rearranged_w_in(w_in):
    o = np.cumsum([0, Q_COLS, KV_COLS, KV_COLS, Q_COLS, KV_COLS, KV_COLS, D_MODEL, D_MODEL])
    qa, ka, va, qb, kb, vb, ga, gb = [w_in[..., o[i]:o[i + 1]] for i in range(8)]

    def pad_v(v):
        z = jnp.zeros(v.shape[:-1] + (V_LANES - HEAD_DIM,), v.dtype)
        return jnp.concatenate([v[..., :HEAD_DIM], z, v[..., HEAD_DIM:], z], axis=-1)

    w_rows = jnp.concatenate([ga, gb], axis=-1).astype(_BF)
    w_t = jnp.swapaxes(jnp.concatenate([ka, kb, pad_v(vb), qb, pad_v(va), qa], axis=-1), -1, -2).astype(_BF)
    return w_rows, w_t


def _tail_features(xt, n_real, nb):
    tail = xt[:, n_real // TOKEN_TILE:]
    tail = jnp.moveaxis(tail, 2, 1).reshape(xt.shape[0], xt.shape[2], -1)[:, :, :nb * N_META]
    return tail.reshape(xt.shape[0], xt.shape[2], nb, N_META)


def kernel(x_prompt, x_sample, meta_tokens, g_mix_pre, g_mix_post, g_ffn_pre, g_ffn_post, w_in, q_norm_b,
           k_norm_b, sink_a, w_branch_a, w_branch_b, w_out, w_ffn_up, w_ffn_down):
    b1, n1, _ = x_prompt.shape
    b2, n2, _ = x_sample.shape
    depth = w_in.shape[0]
    t = TOKEN_TILE
    assert n1 % t == 0 and n2 % t == 0 and (b1 * n1) % n2 == 0 and n1 % GRID_W == 0 and n2 % GRID_W == 0
    assert (b1 * n1) % DENSE_TILE == 0 and (b2 * n2) % DENSE_TILE == 0
    nb = b1 + b2
    n_real = b1 * n1 + b2 * n2
    ntok = -(-(n_real + nb * N_META) // DENSE_TILE) * DENSE_TILE
    tail = ntok - n_real
    assert tail % t == 0
    layout = (n_real, b1, n1, b2, n2)

    h = (x_prompt.reshape(b1 * n1, D_MODEL).astype(_F32), x_sample.reshape(b2 * n2, D_MODEL).astype(_F32),
         jnp.concatenate([jnp.tile(meta_tokens.astype(_F32), (nb, 1)),
                          jnp.zeros((tail - nb * N_META, D_MODEL), _F32)], axis=0))

    cost, sint = _rope_tables(layout, ntok)
    bias = _window_bias()
    w_rows, w_t = _rearranged_w_in(w_in)
    wa, wb, wo = w_branch_a.astype(_BF), w_branch_b.astype(_BF), w_out.astype(_BF)
    wup, wdown = w_ffn_up.astype(_BF), w_ffn_down.astype(_BF)
    row = lambda g: g.reshape(1, -1).astype(_F32)
    col = lambda g: g.reshape(-1, 1).astype(_F32)

    for l in range(depth):
        bound = (HEAD_DIM * SCALE * LOG2E * 1.02) * jnp.max(jnp.abs(q_norm_b[l])) * jnp.max(jnp.abs(k_norm_b[l]))
        unit = (jnp.arange(QK_ROWS - HEAD_DIM) == 0).astype(_F32).reshape(-1, 1)
        proj = _in_proj(h, row(g_mix_pre[l]), w_rows[l], w_t[l], cost, sint, col(q_norm_b[l]), col(k_norm_b[l]),
                        -bound.astype(_F32) * unit, unit)
        if l == 0:
            *proj, h = proj
        qat, qbt, kat, kbt, vat, vbt, ga, gb = proj
        sink = sink_a[l].astype(_F32) * LOG2E
        ka_tail, va_tail = _tail_features(kat, n_real, nb), _tail_features(vat, n_real, nb)
        kameta = jnp.transpose(ka_tail, (2, 0, 1, 3))
        vameta = jnp.transpose(va_tail, (2, 0, 3, 1))
        vametat = jnp.transpose(va_tail, (2, 0, 1, 3))
        qa_meta = jnp.transpose(_tail_features(qat, n_real, nb), (2, 3, 0, 1)).reshape(nb * N_META, Q_COLS)
        qa_meta = jnp.pad(qa_meta, ((0, tail - nb * N_META), (0, 0)))

        oa = _window_meta_attention(sink, qa_meta, kat, vat, kameta, vameta, layout, ntok)
        oa = _window_attention(sink, qat, kat, vat, kameta, vametat, bias, oa, layout)

        kbmeta = jnp.transpose(_tail_features(kbt, n_real, nb), (2, 0, 3, 1))
        vbmetat = jnp.transpose(_tail_features(vbt, n_real, nb), (2, 0, 1, 3))
        qmeta = _tail_features(qbt, n_real, nb).reshape(N_KV, GROUP, QK_ROWS, nb, N_META)
        qmeta = jnp.transpose(qmeta, (3, 0, 2, 1, 4)).reshape(nb, N_KV, QK_ROWS, GROUP * N_META)
        qmeta = jnp.pad(qmeta, ((0, 0), (0, 0), (0, 0), (0, V_LANES - GROUP * N_META)))

        glob = functools.partial(_global_attention, kbt=kbt, vbt=vbt, kmeta=kbmeta, vmetat=vbmetat)
        def mixer_b_maxfree():
            out = None
            tails = jnp.zeros((tail, Q_COLS), _BF)
            for row0, batches, batch0, n in ((0, b1, 0, n1), (b1 * n1, b2, b1, n2)):
                out, tails = _global_fused(qbt, kbt, vbt, kbmeta, vbmetat, qmeta, out, tails, ntok, tq=1024,
                                           q_row0=row0, batches=batches, kv_batch0=batch0, n=n)
            return out, tails

        def mixer_b_general():
            tails = glob(qmeta, prev=None, out_rows=tail, tq=N_META, q_row0=0, q_rows_per_batch=N_META,
                         grid_batches=b1, valid_batches=b1, kv_batch0=0, kv_row0=0, n=n1)
            tails = glob(qmeta, prev=tails, out_rows=tail, tq=N_META, q_row0=b1 * N_META, q_rows_per_batch=N_META,
                         grid_batches=tail // N_META - b1, valid_batches=b2, kv_batch0=b1, kv_row0=b1 * n1, n=n2)
            out = glob(qbt, prev=None, out_rows=ntok, tq=512, q_row0=0, q_rows_per_batch=n1, grid_batches=b1,
                       valid_batches=b1, kv_batch0=0, kv_row0=0, n=n1)
            out = glob(qbt, prev=out, out_rows=ntok, tq=512, q_row0=b1 * n1, q_rows_per_batch=n2, grid_batches=b2,
                       valid_batches=b2, kv_batch0=b1, kv_row0=b1 * n1, n=n2)
            return out, tails

        ob, ob_tail = lax.cond(bound <= SAFE_OFFSET_MAX, mixer_b_maxfree, mixer_b_general)

        h = _mix(h, oa, ob, ob_tail, ga, gb, wa[l], wb[l], wo[l], row(g_mix_post[l]))
        h = _ffn(h, row(g_ffn_pre[l]), wup[l], wdown[l], row(g_ffn_post[l]),
                 split_rows=(b1 * n1, b2 * n2) if l == depth - 1 else None)

    y_prompt, y_sample = h
    return (y_prompt.reshape(b1, n1, D_MODEL).astype(x_prompt.dtype),
            y_sample.reshape(b2, n2, D_MODEL).astype(x_sample.dtype))
```

```python
import functools
import math

import jax
import jax.numpy as jnp
import numpy as np
from jax import lax
from jax.experimental import pallas as pl
from jax.experimental.pallas import tpu as pltpu

D_MODEL = 1024
HEAD_DIM = 64
N_HEADS = 8
N_KV = 2
GROUP = N_HEADS // N_KV
Q_COLS = N_HEADS * HEAD_DIM
KV_COLS = N_KV * HEAD_DIM
N_META = 16
BLOCK = 128
GRID_W = 64
ROPE_BASE = 10000.0
ROPE_FREQS = HEAD_DIM // 4
D_FF = 2816
EPS = 1e-6
NEG_INF = -1e30
SCALE = HEAD_DIM ** -0.5
LOG2E = math.log2(math.e)

QK_ROWS = HEAD_DIM + 16
SAFE_OFFSET_MAX = 40.0
V_LANES = 128
TOKEN_TILE = 512
DENSE_TILE = 1024
WINDOW_TILE = 1024
VMEM_LIMIT = 56 * 1024 * 1024

_C_GA = 0
_C_GB = _C_GA + D_MODEL
_C_END = _C_GB + D_MODEL
_R_KA = 0
_R_KB = _R_KA + KV_COLS
_R_VB = _R_KB + KV_COLS
_R_QB = _R_VB + N_KV * V_LANES
_R_VA = _R_QB + Q_COLS
_R_QA = _R_VA + N_KV * V_LANES
_R_END = _R_QA + Q_COLS

_BF = jnp.bfloat16
_F32 = jnp.float32


def _dot(a, b):
    return jnp.dot(a, b, preferred_element_type=_F32)


def _dot_nt(a, b):
    return lax.dot_general(a, b, (((1,), (1,)), ((), ())), preferred_element_type=_F32)


def _dot_tn(a, b):
    return lax.dot_general(a, b, (((0,), (0,)), ((), ())), preferred_element_type=_F32)


def _params(sem, vmem=VMEM_LIMIT):
    return pltpu.CompilerParams(dimension_semantics=sem, vmem_limit_bytes=vmem)


def _const_spec(shape):
    nd = len(shape)
    return pl.BlockSpec(shape, lambda *_: (0,) * nd, pipeline_mode=pl.Buffered(1))


def _in_proj_body(h_ref, g_ref, w_ref, wt_ref, cost_ref, sint_ref, qg_ref, kg_ref, qx_ref, kx_ref,
                  qat_ref, qbt_ref, kat_ref, kbt_ref, vat_ref, vbt_ref, ga_ref, gb_ref):
    f = ROPE_FREQS
    ones_row = (lax.broadcasted_iota(jnp.int32, (V_LANES, 1), 0) == HEAD_DIM).astype(_F32)
    q_extra = jnp.broadcast_to(qx_ref[...], (QK_ROWS - HEAD_DIM, TOKEN_TILE)).astype(_BF)
    k_extra = jnp.broadcast_to(kx_ref[...], (QK_ROWS - HEAD_DIM, TOKEN_TILE)).astype(_BF)

    for cc in range(h_ref.shape[0] // TOKEN_TILE):
        rows = slice(cc * TOKEN_TILE, (cc + 1) * TOKEN_TILE)
        h = h_ref[rows, :]
        ms = jnp.mean(h * h, axis=-1, keepdims=True)
        xn = (h * lax.rsqrt(ms + EPS) * g_ref[...]).astype(_BF)

        ga_ref[rows, :] = jax.nn.sigmoid(_dot(xn, w_ref[:, _C_GA:_C_GA + D_MODEL])).astype(_BF)
        gb_ref[rows, :] = jax.nn.sigmoid(_dot(xn, w_ref[:, _C_GB:_C_GB + D_MODEL])).astype(_BF)

        tt = _dot_nt(wt_ref[...], xn)
        cost = cost_ref[:, rows]
        sint = sint_ref[:, rows]

        def norm_rope(x, gain):
            x = x * lax.rsqrt(jnp.mean(x * x, axis=0, keepdims=True) + EPS) * gain
            partner = jnp.concatenate([x[f:2 * f], x[0:f], x[3 * f:4 * f], x[2 * f:3 * f]], axis=0)
            return x * cost + partner * sint

        for j in range(N_KV):
            kat_ref[j, cc] = tt[_R_KA + j * HEAD_DIM:_R_KA + (j + 1) * HEAD_DIM].astype(_BF)
            k = norm_rope(tt[_R_KB + j * HEAD_DIM:_R_KB + (j + 1) * HEAD_DIM], kg_ref[...])
            kbt_ref[j, cc, :HEAD_DIM, :] = k.astype(_BF)
            kbt_ref[j, cc, HEAD_DIM:, :] = k_extra
            vat_ref[j, cc] = (tt[_R_VA + j * V_LANES:_R_VA + (j + 1) * V_LANES] + ones_row).astype(_BF)
            vbt_ref[j, cc] = (tt[_R_VB + j * V_LANES:_R_VB + (j + 1) * V_LANES] + ones_row).astype(_BF)
        for hd in range(N_HEADS):
            qa = tt[_R_QA + hd * HEAD_DIM:_R_QA + (hd + 1) * HEAD_DIM] * (SCALE * LOG2E)
            qat_ref[hd, cc] = qa.astype(_BF)
            q = norm_rope(tt[_R_QB + hd * HEAD_DIM:_R_QB + (hd + 1) * HEAD_DIM], qg_ref[...])
            qbt_ref[hd, cc, :HEAD_DIM, :] = (q * (SCALE * LOG2E)).astype(_BF)
            qbt_ref[hd, cc, HEAD_DIM:, :] = q_extra


def _in_proj_first_body(blocks, xp_ref, xs_ref, tail_ref, *refs):
    p, sm = blocks
    i = pl.program_id(0)
    *rest, h_ref = refs
    h_ref[...] = jnp.where(i < p, xp_ref[...], jnp.where(i < p + sm, xs_ref[...], tail_ref[...]))
    _in_proj_body(h_ref, *rest)


def _in_proj(h, g, w, wt, cost, sint, qg, kg, qx, kx):
    first = isinstance(h, tuple)
    t = TOKEN_TILE if first else DENSE_TILE
    ntok = sum(x.shape[0] for x in h) if first else h.shape[0]
    per = t // TOKEN_TILE
    nch = ntok // TOKEN_TILE
    row = lambda i: (i, 0)
    chunk = lambda i: (0, i, 0, 0)
    feat = lambda heads, rows: (jax.ShapeDtypeStruct((heads, nch, rows, TOKEN_TILE), _BF),
                                pl.BlockSpec((heads, per, rows, TOKEN_TILE), chunk))
    outs = [
        feat(N_HEADS, HEAD_DIM),
        feat(N_HEADS, QK_ROWS),
        feat(N_KV, HEAD_DIM),
        feat(N_KV, QK_ROWS),
        feat(N_KV, V_LANES),
        feat(N_KV, V_LANES),
        (jax.ShapeDtypeStruct((ntok, D_MODEL), _BF), pl.BlockSpec((t, D_MODEL), row)),
        (jax.ShapeDtypeStruct((ntok, D_MODEL), _BF), pl.BlockSpec((t, D_MODEL), row)),
    ]
    if first:
        p, sm, tl = (x.shape[0] // t for x in h)
        body = functools.partial(_in_proj_first_body, (p, sm))
        h_specs = [pl.BlockSpec((t, D_MODEL), lambda i: (jnp.minimum(i, p - 1), 0)),
                   pl.BlockSpec((t, D_MODEL), lambda i: (jnp.clip(i - p, 0, sm - 1), 0)),
                   pl.BlockSpec((t, D_MODEL), lambda i: (jnp.clip(i - p - sm, 0, tl - 1), 0))]
        outs.append((jax.ShapeDtypeStruct((ntok, D_MODEL), _F32), pl.BlockSpec((t, D_MODEL), row)))
        h_args = list(h)
    else:
        body, h_specs, h_args = _in_proj_body, [pl.BlockSpec((t, D_MODEL), row)], [h]
    return pl.pallas_call(
        body,
        grid=(ntok // t,),
        in_specs=h_specs + [
            _const_spec((1, D_MODEL)),
            _const_spec((D_MODEL, _C_END)),
            _const_spec((_R_END, D_MODEL)),
            pl.BlockSpec((HEAD_DIM, t), lambda i: (0, i)),
            pl.BlockSpec((HEAD_DIM, t), lambda i: (0, i)),
            _const_spec((HEAD_DIM, 1)),
            _const_spec((HEAD_DIM, 1)),
            _const_spec((QK_ROWS - HEAD_DIM, 1)),
            _const_spec((QK_ROWS - HEAD_DIM, 1)),
        ],
        out_specs=tuple(o[1] for o in outs),
        out_shape=tuple(o[0] for o in outs),
        compiler_params=_params(("parallel",)),
        name="in_proj_first" if first else "in_proj",
    )(*h_args, g, w, wt, cost, sint, qg, kg, qx, kx)


_WINDOW_LOOKAHEAD = 3


def _stack_heads(q, j):
    base = j * GROUP * HEAD_DIM
    return jnp.concatenate([q[:, base + g * HEAD_DIM: base + (g + 1) * HEAD_DIM] for g in range(GROUP)], axis=0)


def _sink_rows(sink_ref, j, rows, lanes):
    return jnp.concatenate([jnp.full((rows, lanes), sink_ref[j * GROUP + g], _F32) for g in range(GROUP)], axis=0)


def _window_body(geom, sink_ref, q_ref, kmain_ref, kprev_ref, knext_ref, vmain_ref, vprev_ref, vnext_ref,
                 kmeta_ref, vmeta_ref, bias_ref, qmeta_ref, _tail_ref, o_ref, otail_ref):
    t1, tpb1, tpb2 = geom
    t = pl.program_id(0)
    in_prompt = t < t1
    is_first = jnp.where(in_prompt, t % tpb1 == 0, (t - t1) % tpb2 == 0)
    is_last = jnp.where(in_prompt, t % tpb1 == tpb1 - 1, (t - t1) % tpb2 == tpb2 - 1)
    nblk = WINDOW_TILE // BLOCK
    per_chunk = TOKEN_TILE // BLOCK

    def softmax_values(j, r, s, sm, vwin, vmeta, sink):
        m = jnp.maximum(jnp.maximum(jnp.max(s, axis=0, keepdims=True), jnp.max(sm, axis=0, keepdims=True)), sink)
        acc = (_dot(vwin, jnp.exp2(s - m).astype(_BF))
               + _dot(vmeta, jnp.exp2(sm - m).astype(_BF)))
        l = acc[HEAD_DIM:HEAD_DIM + 1] + jnp.exp2(sink - m)
        o = (acc / l).T.astype(_BF)
        for g in range(GROUP):
            c0 = (j * GROUP + g) * HEAD_DIM
            o_ref[r * BLOCK:(r + 1) * BLOCK, c0:c0 + HEAD_DIM] = o[g * BLOCK:(g + 1) * BLOCK, :HEAD_DIM]

    pending = []
    for j in range(N_KV):
        chunks = range(kmain_ref.shape[1])
        kcat = jnp.concatenate([kprev_ref[j, 0]] + [kmain_ref[j, cc] for cc in chunks] + [knext_ref[j, 0]], axis=1)
        vcat = jnp.concatenate([vprev_ref[j, 0]] + [vmain_ref[j, cc] for cc in chunks] + [vnext_ref[j, 0]], axis=1)
        kmeta = kmeta_ref[0, j]
        vmeta = vmeta_ref[0, j]
        sink = jnp.concatenate([jnp.full((1, BLOCK), sink_ref[j * GROUP + g], _F32) for g in range(GROUP)], axis=1)
        for r in range(nblk):
            variant = 0
            if r == 0:
                variant = jnp.where(is_first, 1, 0)
            if r == nblk - 1:
                variant = jnp.where(is_last, 2, variant)
            c0 = (r % per_chunk) * BLOCK
            qt = jnp.concatenate([q_ref[j * GROUP + g, r // per_chunk][:, c0:c0 + BLOCK] for g in range(GROUP)],
                                 axis=1)
            s = _dot_tn(kcat[:, r * BLOCK:(r + 3) * BLOCK], qt) + bias_ref[variant, j]
            sm = _dot_tn(kmeta, qt)
            pending.append((j, r, s, sm, vcat[:, r * BLOCK:(r + 3) * BLOCK], vmeta, sink))
            if len(pending) > _WINDOW_LOOKAHEAD:
                softmax_values(*pending.pop(0))
    for item in pending:
        softmax_values(*item)

    @pl.when(is_first)
    def _():
        q_all = qmeta_ref[...]
        for j in range(N_KV):
            qs = _stack_heads(q_all, j)
            sink = _sink_rows(sink_ref, j, N_META, 1)
            sm = _dot(qs, kmeta_ref[0, j])
            sf = _dot(qs, kmain_ref[j, 0][:, :BLOCK])
            m = jnp.maximum(jnp.maximum(sm.max(axis=1, keepdims=True), sf.max(axis=1, keepdims=True)), sink)
            acc = (_dot_nt(jnp.exp2(sm - m).astype(_BF), vmeta_ref[0, j])
                   + _dot_nt(jnp.exp2(sf - m).astype(_BF), vmain_ref[j, 0][:, :BLOCK]))
            l = acc[:, HEAD_DIM:HEAD_DIM + 1] + jnp.exp2(sink - m)
            o = (acc[:, :HEAD_DIM] / l).astype(_BF)
            for g in range(GROUP):
                c0 = (j * GROUP + g) * HEAD_DIM
                otail_ref[:, c0:c0 + HEAD_DIM] = o[g * N_META:(g + 1) * N_META]


def _window_attention(sink, qat, kat, vat, kmeta, vmetat, bias, q_meta, tail, out_rows, layout):
    n_real, b1, n1, b2, n2 = layout
    t = WINDOW_TILE
    per = t // TOKEN_TILE
    nch = kat.shape[1]
    assert n1 % t == 0 and n2 % t == 0
    t1 = b1 * n1 // t
    tpb1, tpb2 = n1 // t, n2 // t
    sub = TOKEN_TILE // BLOCK

    def bid(i):
        return jnp.where(i < t1, i // tpb1, b1 + (i - t1) // tpb2)

    main = lambda i: (0, i, 0, 0)
    prev = lambda i: (0, jnp.maximum(i * per - 1, 0), 0, sub - 1)
    nxt = lambda i: (0, jnp.minimum((i + 1) * per, nch - 1), 0, 0)
    return pl.pallas_call(
        functools.partial(_window_body, (t1, tpb1, tpb2)),
        grid=(n_real // t,),
        in_specs=[
            pl.BlockSpec(memory_space=pltpu.SMEM),
            pl.BlockSpec((N_HEADS, per, HEAD_DIM, TOKEN_TILE), main),
            pl.BlockSpec((N_KV, per, HEAD_DIM, TOKEN_TILE), main),
            pl.BlockSpec((N_KV, 1, HEAD_DIM, BLOCK), prev),
            pl.BlockSpec((N_KV, 1, HEAD_DIM, BLOCK), nxt),
            pl.BlockSpec((N_KV, per, V_LANES, TOKEN_TILE), main),
            pl.BlockSpec((N_KV, 1, V_LANES, BLOCK), prev),
            pl.BlockSpec((N_KV, 1, V_LANES, BLOCK), nxt),
            pl.BlockSpec((1, N_KV, HEAD_DIM, N_META), lambda i: (bid(i), 0, 0, 0)),
            pl.BlockSpec((1, N_KV, V_LANES, N_META), lambda i: (bid(i), 0, 0, 0)),
            _const_spec((3, N_KV, 3 * BLOCK, GROUP * BLOCK)),
            pl.BlockSpec((N_META, Q_COLS), lambda i: (bid(i), 0)),
            pl.BlockSpec(memory_space=pl.ANY),
        ],
        out_specs=(pl.BlockSpec((t, Q_COLS), lambda i: (i, 0)),
                   pl.BlockSpec((N_META, Q_COLS), lambda i: (bid(i), 0))),
        out_shape=(jax.ShapeDtypeStruct((out_rows, Q_COLS), _BF), jax.ShapeDtypeStruct(tail.shape, _BF)),
        input_output_aliases={12: 1},
        compiler_params=_params(("arbitrary",)),
        name="window_attn",
    )(sink, qat, kat, kat, kat, vat, vat, vat, kmeta, vmetat, bias, q_meta, tail)


_KEY_UNROLL = 4
_QUERY_BLOCK = 256
_SCORE_LOOKAHEAD = 3


def _global_body(nvalid, nchunks, tq, q_ref, kt_ref, vt_ref, kmeta_ref, vmetat_ref, _prev_ref, o_ref,
                 qt_ref, s0_ref, s1_ref, m_ref, acc_ref):
    b = pl.program_id(0)

    cols = qt_ref.shape[1]
    blocks = [slice(c0, min(c0 + _QUERY_BLOCK, cols)) for c0 in range(0, cols, _QUERY_BLOCK)]

    def scores(c, s_ref):
        for sl in blocks:
            s_ref[:, sl] = _dot_tn(kt_ref[0, c], qt_ref[:, sl])

    def softmax_pv(c, s_ref):
        for sl in blocks:
            s = s_ref[:, sl]
            m_prev = m_ref[:, sl]
            m_new = jnp.maximum(m_prev, jnp.max(s, axis=0, keepdims=True))
            p = jnp.exp2(s - m_new).astype(_BF)
            acc_ref[:, sl] = jnp.exp2(m_prev - m_new) * acc_ref[:, sl] + _dot(vt_ref[0, c], p)
            m_ref[:, sl] = m_new

    @pl.when(b < nvalid)
    def _():
        _stack_queries(q_ref, qt_ref)
        sm = _dot(kmeta_ref[0, 0], qt_ref[...])
        m0 = jnp.max(sm, axis=0, keepdims=True)
        m_ref[...] = m0
        acc_ref[...] = _dot(vmetat_ref[0, 0], jnp.exp2(sm - m0).astype(_BF))
        bufs = (s0_ref, s1_ref)
        scores(0, s0_ref)

        unroll = _KEY_UNROLL if nchunks > _KEY_UNROLL else 2

        def group(i, carry):
            c0 = unroll * i
            for u in range(unroll):
                scores(c0 + u + 1, bufs[(u + 1) % 2])
                softmax_pv(c0 + u, bufs[u % 2])
            return carry

        full = (nchunks - 1) // unroll
        lax.fori_loop(0, full, group, 0)
        for c in range(full * unroll, nchunks):
            if c + 1 < nchunks:
                scores(c + 1, bufs[(c + 1) % 2])
            softmax_pv(c, bufs[c % 2])
        _write_output(acc_ref, o_ref, tq)

    @pl.when(b >= nvalid)
    def _():
        o_ref[...] = jnp.zeros(o_ref.shape, o_ref.dtype)


def _maxfree_chunks(kt_ref, vt_ref, q_ref, acc_ref, cs):
    cols = q_ref.shape[1]
    blocks = [slice(c0, min(c0 + _QUERY_BLOCK, cols)) for c0 in range(0, cols, _QUERY_BLOCK)]

    def values(c, sl, s):
        acc_ref[:, sl] += _dot(vt_ref[0, c], jnp.exp2(s).astype(_BF))

    pending = []
    for c in cs:
        for sl in blocks:
            pending.append((c, sl, _dot_tn(kt_ref[0, c], q_ref[:, sl])))
            if len(pending) > _SCORE_LOOKAHEAD:
                values(*pending.pop(0))
    for item in pending:
        values(*item)


def _maxfree_attend(nchunks, kt_ref, vt_ref, kmeta_ref, vmetat_ref, q_ref, acc_ref):
    acc_ref[...] = _dot(vmetat_ref[0, 0], jnp.exp2(_dot(kmeta_ref[0, 0], q_ref[...])).astype(_BF))

    def group(i, carry):
        _maxfree_chunks(kt_ref, vt_ref, q_ref, acc_ref, [_KEY_UNROLL * i + u for u in range(_KEY_UNROLL)])
        return carry

    full = nchunks // _KEY_UNROLL
    lax.fori_loop(0, full, group, 0)
    if full * _KEY_UNROLL < nchunks:
        _maxfree_chunks(kt_ref, vt_ref, q_ref, acc_ref, range(full * _KEY_UNROLL, nchunks))


def _global_fused_body(nchunks, tq, has_prev, q_ref, kt_ref, vt_ref, kmeta_ref, vmetat_ref, qm_ref, *refs):
    o_ref, otail_ref, qt_ref, acc_ref, accm_ref = refs[2 if has_prev else 1:]
    _stack_queries(q_ref, qt_ref)
    _maxfree_attend(nchunks, kt_ref, vt_ref, kmeta_ref, vmetat_ref, qt_ref, acc_ref)
    _write_output(acc_ref, o_ref, tq)

    @pl.when(pl.program_id(2) == 0)
    def _():
        _maxfree_attend(nchunks, kt_ref, vt_ref, kmeta_ref, vmetat_ref, qm_ref.at[0, 0], accm_ref)
        _write_output(accm_ref, otail_ref, N_META)


def _global_fused(qbt, kbt, vbt, kmeta, vmetat, qmeta, prev, prev_tail, out_rows, *, tq, q_row0, batches,
                  kv_batch0, n):
    t = TOKEN_TILE
    nchunks = n // t
    qt = n // tq
    cols = GROUP * tq
    assert tq % t == 0 and q_row0 % tq == 0 and n % tq == 0 and q_row0 % n == 0
    kv = lambda b, j, i: (j, q_row0 // n + b, 0, 0)
    meta = lambda b, j, i: (kv_batch0 + b, j, 0, 0)
    in_specs = [
        pl.BlockSpec((GROUP, tq // t, QK_ROWS, t), lambda b, j, i: (j, (q_row0 + b * n) // tq + i, 0, 0)),
        pl.BlockSpec((1, nchunks, QK_ROWS, t), kv),
        pl.BlockSpec((1, nchunks, V_LANES, t), kv),
        pl.BlockSpec((1, 1, N_META, QK_ROWS), meta),
        pl.BlockSpec((1, 1, V_LANES, N_META), meta),
        pl.BlockSpec((1, 1, QK_ROWS, V_LANES), meta),
        pl.BlockSpec(memory_space=pl.ANY),
    ]
    args = [qbt, kbt, vbt, kmeta, vmetat, qmeta, prev_tail]
    aliases = {6: 1}
    if prev is not None:
        in_specs.insert(6, pl.BlockSpec(memory_space=pl.ANY))
        args.insert(6, prev)
        aliases = {6: 0, 7: 1}
    return pl.pallas_call(
        functools.partial(_global_fused_body, nchunks, tq, prev is not None),
        grid=(batches, N_KV, qt),
        in_specs=in_specs,
        out_specs=(pl.BlockSpec((tq, GROUP * HEAD_DIM), lambda b, j, i: (q_row0 // tq + b * qt + i, j)),
                   pl.BlockSpec((N_META, GROUP * HEAD_DIM), lambda b, j, i: (kv_batch0 + b, j))),
        out_shape=(jax.ShapeDtypeStruct((out_rows, Q_COLS), _BF),
                   jax.ShapeDtypeStruct(prev_tail.shape, _BF)),
        scratch_shapes=[pltpu.VMEM((QK_ROWS, cols), _BF), pltpu.VMEM((V_LANES, cols), _F32),
                        pltpu.VMEM((V_LANES, V_LANES), _F32)],
        input_output_aliases=aliases,
        compiler_params=_params(("parallel", "parallel", "arbitrary")),
        name=f"global_attn_fused_tq{tq}_n{n}",
    )(*args)


def _stack_queries(q_ref, qt_ref):
    if q_ref.shape[0] == GROUP:
        qt_ref[...] = jnp.concatenate([q_ref[g, cc] for g in range(GROUP) for cc in range(q_ref.shape[1])], axis=1)
    else:
        qt_ref[...] = q_ref[0, 0]


def _write_output(acc_ref, o_ref, tq):
    acc = acc_ref[...]
    o = (acc / acc[HEAD_DIM:HEAD_DIM + 1]).T.astype(_BF)
    for g in range(GROUP):
        o_ref[:, g * HEAD_DIM:(g + 1) * HEAD_DIM] = o[g * tq:(g + 1) * tq, :HEAD_DIM]


def _global_attention(q, kbt, vbt, kmeta, vmetat, prev, out_rows, *, tq, q_row0, q_rows_per_batch, grid_batches,
                      valid_batches, kv_batch0, kv_row0, n):
    t = TOKEN_TILE
    nchunks = n // t
    qt = q_rows_per_batch // tq
    assert q_row0 % tq == 0 and kv_row0 % n == 0 and q_rows_per_batch % tq == 0
    kvb = lambda b: jnp.minimum(b, valid_batches - 1)
    if tq == N_META:
        cols = V_LANES
        q_spec = pl.BlockSpec((1, 1, QK_ROWS, cols), lambda b, j, i: (kv_batch0 + kvb(b), j, 0, 0))
    else:
        cols = GROUP * tq
        assert tq % t == 0
        q_spec = pl.BlockSpec((GROUP, tq // t, QK_ROWS, t),
                              lambda b, j, i: (j, (q_row0 + b * q_rows_per_batch) // tq + i, 0, 0))
    in_specs = [
        q_spec,
        pl.BlockSpec((1, nchunks, QK_ROWS, t), lambda b, j, i: (j, kv_row0 // n + kvb(b), 0, 0)),
        pl.BlockSpec((1, nchunks, V_LANES, t), lambda b, j, i: (j, kv_row0 // n + kvb(b), 0, 0)),
        pl.BlockSpec((1, 1, N_META, QK_ROWS), lambda b, j, i: (kv_batch0 + kvb(b), j, 0, 0)),
        pl.BlockSpec((1, 1, V_LANES, N_META), lambda b, j, i: (kv_batch0 + kvb(b), j, 0, 0)),
    ]
    args = [q, kbt, vbt, kmeta, vmetat]
    aliases = {}
    body = functools.partial(_global_body, valid_batches, nchunks, tq)
    if prev is None:
        body = functools.partial(_global_body_noprev, body)
    else:
        in_specs.append(pl.BlockSpec(memory_space=pl.ANY))
        args.append(prev)
        aliases = {5: 0}
    return pl.pallas_call(
        body,
        grid=(grid_batches, N_KV, qt),
        in_specs=in_specs,
        out_specs=pl.BlockSpec((tq, GROUP * HEAD_DIM), lambda b, j, i: (q_row0 // tq + b * qt + i, j)),
        out_shape=jax.ShapeDtypeStruct((out_rows, Q_COLS), _BF),
        scratch_shapes=[pltpu.VMEM((QK_ROWS, cols), _BF), pltpu.VMEM((t, cols), _F32), pltpu.VMEM((t, cols), _F32),
                        pltpu.VMEM((1, cols), _F32), pltpu.VMEM((V_LANES, cols), _F32)],
        input_output_aliases=aliases,
        compiler_params=_params(("parallel", "parallel", "arbitrary")),
        name=f"global_attn_tq{tq}_n{n}",
    )(*args)


def _global_body_noprev(body, q_ref, kt_ref, vt_ref, kmeta_ref, vmetat_ref, o_ref, *scratch):
    body(q_ref, kt_ref, vt_ref, kmeta_ref, vmetat_ref, None, o_ref, *scratch)


def _mix_body(real_tiles, h_ref, oa_ref, oatail_ref, ob_ref, obtail_ref, ga_ref, gb_ref, wa_ref, wb_ref, wo_ref,
              g_ref, o_ref):
    in_tail = pl.program_id(0) >= real_tiles
    oa = jnp.where(in_tail, oatail_ref[...], oa_ref[...])
    ob = jnp.where(in_tail, obtail_ref[...], ob_ref[...])
    mix = (ga_ref[...].astype(_F32) * _dot(oa, wa_ref[...])
           + gb_ref[...].astype(_F32) * _dot(ob, wb_ref[...]))
    u = _dot(mix.astype(_BF), wo_ref[...])
    ms = jnp.mean(u * u, axis=-1, keepdims=True)
    o_ref[...] = h_ref[...] + u * lax.rsqrt(ms + EPS) * g_ref[...]


def _mix(h, oa, oa_tail, ob, ob_tail, ga, gb, wa, wb, wo, g):
    ntok = h.shape[0]
    t = DENSE_TILE
    real_tiles = (ntok - ob_tail.shape[0]) // t
    assert real_tiles * t + ob_tail.shape[0] == ntok and ob_tail.shape[0] % t == 0
    row = lambda i: (i, 0)
    return pl.pallas_call(
        functools.partial(_mix_body, real_tiles),
        grid=(ntok // t,),
        in_specs=[
            pl.BlockSpec((t, D_MODEL), row),
            pl.BlockSpec((t, Q_COLS), lambda i: (jnp.minimum(i, real_tiles - 1), 0)),
            pl.BlockSpec((t, Q_COLS), lambda i: (jnp.maximum(i - real_tiles, 0), 0)),
            pl.BlockSpec((t, Q_COLS), lambda i: (jnp.minimum(i, real_tiles - 1), 0)),
            pl.BlockSpec((t, Q_COLS), lambda i: (jnp.maximum(i - real_tiles, 0), 0)),
            pl.BlockSpec((t, D_MODEL), row), pl.BlockSpec((t, D_MODEL), row),
            _const_spec((Q_COLS, D_MODEL)), _const_spec((Q_COLS, D_MODEL)),
            _const_spec((D_MODEL, D_MODEL)), _const_spec((1, D_MODEL)),
        ],
        out_specs=pl.BlockSpec((t, D_MODEL), row),
        out_shape=jax.ShapeDtypeStruct(h.shape, _F32),
        input_output_aliases={0: 0},
        compiler_params=_params(("parallel",)),
        name="branch_mix",
    )(h, oa, oa_tail, ob, ob_tail, ga, gb, wa, wb, wo, g)


_FF_CHUNK = 256


def _ffn_body(h_ref, gpre_ref, wup_ref, wdown_ref, gpost_ref, o_ref, act_ref):
    h = h_ref[...]
    ms = jnp.mean(h * h, axis=-1, keepdims=True)
    xn = (h * lax.rsqrt(ms + EPS) * gpre_ref[...]).astype(_BF)
    for c in range(0, D_FF, _FF_CHUNK):
        a = _dot(xn, wup_ref[:, c:c + _FF_CHUNK])
        b = _dot(xn, wup_ref[:, D_FF + c:D_FF + c + _FF_CHUNK])
        act_ref[:, c:c + _FF_CHUNK] = (a * jax.nn.sigmoid(a) * b).astype(_BF)
    u = _dot(act_ref[...], wdown_ref[...])
    ms = jnp.mean(u * u, axis=-1, keepdims=True)
    o_ref[...] = h + u * lax.rsqrt(ms + EPS) * gpost_ref[...]


def _ffn_last_body(blocks, h_ref, gpre_ref, wup_ref, wdown_ref, gpost_ref, yp_ref, ys_ref, act_ref):
    p, sm = blocks
    i = pl.program_id(0)

    @pl.when(i < p)
    def _():
        _ffn_body(h_ref, gpre_ref, wup_ref, wdown_ref, gpost_ref, yp_ref, act_ref)

    @pl.when(jnp.logical_and(i >= p, i < p + sm))
    def _():
        _ffn_body(h_ref, gpre_ref, wup_ref, wdown_ref, gpost_ref, ys_ref, act_ref)


def _ffn(h, gpre, wup, wdown, gpost, split_rows=None):
    ntok = h.shape[0]
    t = DENSE_TILE
    row = lambda i: (i, 0)
    if split_rows is None:
        body, aliases = _ffn_body, {0: 0}
        out_specs = pl.BlockSpec((t, D_MODEL), row)
        out_shape = jax.ShapeDtypeStruct(h.shape, _F32)
    else:
        p, sm = (r // t for r in split_rows)
        assert p * t == split_rows[0] and sm * t == split_rows[1]
        body, aliases = functools.partial(_ffn_last_body, (p, sm)), {}
        out_specs = (pl.BlockSpec((t, D_MODEL), lambda i: (jnp.minimum(i, p - 1), 0)),
                     pl.BlockSpec((t, D_MODEL), lambda i: (jnp.clip(i - p, 0, sm - 1), 0)))
        out_shape = (jax.ShapeDtypeStruct((split_rows[0], D_MODEL), _F32),
                     jax.ShapeDtypeStruct((split_rows[1], D_MODEL), _F32))
    return pl.pallas_call(
        body,
        grid=(ntok // t,),
        in_specs=[
            pl.BlockSpec((t, D_MODEL), row),
            _const_spec((1, D_MODEL)),
            _const_spec((D_MODEL, 2 * D_FF)),
            _const_spec((D_FF, D_MODEL)),
            _const_spec((1, D_MODEL)),
        ],
        out_specs=out_specs,
        out_shape=out_shape,
        scratch_shapes=[pltpu.VMEM((t, D_FF), _BF)],
        input_output_aliases=aliases,
        compiler_params=_params(("arbitrary",)),
        name="swiglu_ffn" if split_rows is None else "swiglu_ffn_last",
    )(h, gpre, wup, wdown, gpost)


def _mix_ffn_body(real_tiles, h_ref, oa_ref, oatail_ref, ob_ref, obtail_ref, ga_ref, gb_ref, wa_ref, wb_ref, wo_ref,
                  gmix_ref, gpre_ref, wup_ref, wdown_ref, gpost_ref, o_ref, act_ref):
    in_tail = pl.program_id(0) >= real_tiles
    oa = jnp.where(in_tail, oatail_ref[...], oa_ref[...])
    ob = jnp.where(in_tail, obtail_ref[...], ob_ref[...])
    mix = (ga_ref[...].astype(_F32) * _dot(oa, wa_ref[...])
           + gb_ref[...].astype(_F32) * _dot(ob, wb_ref[...]))
    u = _dot(mix.astype(_BF), wo_ref[...])
    ms = jnp.mean(u * u, axis=-1, keepdims=True)
    h = h_ref[...] + u * lax.rsqrt(ms + EPS) * gmix_ref[...]

    ms = jnp.mean(h * h, axis=-1, keepdims=True)
    xn = (h * lax.rsqrt(ms + EPS) * gpre_ref[...]).astype(_BF)
    for c in range(0, D_FF, _FF_CHUNK):
        a = _dot(xn, wup_ref[:, c:c + _FF_CHUNK])
        b = _dot(xn, wup_ref[:, D_FF + c:D_FF + c + _FF_CHUNK])
        act_ref[:, c:c + _FF_CHUNK] = (a * jax.nn.sigmoid(a) * b).astype(_BF)
    u = _dot(act_ref[...], wdown_ref[...])
    ms = jnp.mean(u * u, axis=-1, keepdims=True)
    o_ref[...] = h + u * lax.rsqrt(ms + EPS) * gpost_ref[...]


def _mix_ffn(h, oa, oa_tail, ob, ob_tail, ga, gb, wa, wb, wo, gmix, gpre, wup, wdown, gpost):
    ntok = h.shape[0]
    t = TOKEN_TILE
    real_tiles = (ntok - ob_tail.shape[0]) // t
    assert real_tiles * t + ob_tail.shape[0] == ntok and ob_tail.shape[0] % t == 0
    row = lambda i: (i, 0)
    real = lambda i: (jnp.minimum(i, real_tiles - 1), 0)
    tails = lambda i: (jnp.maximum(i - real_tiles, 0), 0)
    return pl.pallas_call(
        functools.partial(_mix_ffn_body, real_tiles),
        grid=(ntok // t,),
        in_specs=[
            pl.BlockSpec((t, D_MODEL), row),
            pl.BlockSpec((t, Q_COLS), real), pl.BlockSpec((t, Q_COLS), tails),
            pl.BlockSpec((t, Q_COLS), real), pl.BlockSpec((t, Q_COLS), tails),
            pl.BlockSpec((t, D_MODEL), row), pl.BlockSpec((t, D_MODEL), row),
            _const_spec((Q_COLS, D_MODEL)), _const_spec((Q_COLS, D_MODEL)),
            _const_spec((D_MODEL, D_MODEL)), _const_spec((1, D_MODEL)), _const_spec((1, D_MODEL)),
            _const_spec((D_MODEL, 2 * D_FF)), _const_spec((D_FF, D_MODEL)), _const_spec((1, D_MODEL)),
        ],
        out_specs=pl.BlockSpec((t, D_MODEL), row),
        out_shape=jax.ShapeDtypeStruct(h.shape, _F32),
        scratch_shapes=[pltpu.VMEM((t, D_FF), _BF)],
        input_output_aliases={0: 0},
        compiler_params=_params(("parallel",)),
        name="mix_ffn",
    )(h, oa, oa_tail, ob, ob_tail, ga, gb, wa, wb, wo, gmix, gpre, wup, wdown, gpost)


def _rope_tables(layout, ntok):
    n_real, b1, n1, b2, n2 = layout
    idx = np.zeros((ntok,), np.int64)
    idx[:b1 * n1] = np.arange(b1 * n1) % n1
    idx[b1 * n1:n_real] = np.arange(b2 * n2) % n2
    rows = (idx // GRID_W).astype(np.float32)
    cols = (idx % GRID_W).astype(np.float32)
    rows[n_real:] = 0.0
    cols[n_real:] = 0.0
    freqs = ROPE_BASE ** (-jnp.arange(ROPE_FREQS, dtype=_F32) / ROPE_FREQS)
    ang_r = jnp.asarray(rows)[:, None] * freqs[None, :]
    ang_c = jnp.asarray(cols)[:, None] * freqs[None, :]
    cr, sr, cc, sc = jnp.cos(ang_r), jnp.sin(ang_r), jnp.cos(ang_c), jnp.sin(ang_c)
    cos64 = jnp.concatenate([cr, cr, cc, cc], axis=1)
    sin64 = jnp.concatenate([-sr, sr, -sc, sc], axis=1)
    return cos64.T, sin64.T


def _window_bias():
    slopes = 2.0 ** (-8.0 * np.arange(1, N_HEADS + 1, dtype=np.float64) / N_HEADS)
    rel = np.arange(BLOCK)[:, None] - (np.arange(3 * BLOCK) - BLOCK)[None, :]
    dist = np.abs(rel)
    band = dist <= BLOCK
    base = np.where(band[None], -slopes[:, None, None] * dist[None].astype(np.float64) * LOG2E, NEG_INF)
    no_prev = base.copy()
    no_prev[:, :, :BLOCK] = NEG_INF
    no_next = base.copy()
    no_next[:, :, 2 * BLOCK:3 * BLOCK] = NEG_INF
    out = np.stack([base, no_prev, no_next]).reshape(3, N_KV, GROUP * BLOCK, 3 * BLOCK)
    return jnp.asarray(np.swapaxes(out, -1, -2), _F32)


def _rearranged_w_in(w_in):
    o = np.cumsum([0, Q_COLS, KV_COLS, KV_COLS, Q_COLS, KV_COLS, KV_COLS, D_MODEL, D_MODEL])
    qa, ka, va, qb, kb, vb, ga, gb = [w_in[..., o[i]:o[i + 1]] for i in range(8)]

    def pad_v(v):
        z = jnp.zeros(v.shape[:-1] + (V_LANES - HEAD_DIM,), v.dtype)
        return jnp.concatenate([v[..., :HEAD_DIM], z, v[..., HEAD_DIM:], z], axis=-1)

    w_rows = jnp.concatenate([ga, gb], axis=-1).astype(_BF)
    w_t = jnp.swapaxes(jnp.concatenate([ka, kb, pad_v(vb), qb, pad_v(va), qa], axis=-1), -1, -2).astype(_BF)
    return w_rows, w_t


def _tail_features(xt, n_real, nb):
    tail = xt[:, n_real // TOKEN_TILE:]
    tail = jnp.moveaxis(tail, 2, 1).reshape(xt.shape[0], xt.shape[2], -1)[:, :, :nb * N_META]
    return tail.reshape(xt.shape[0], xt.shape[2], nb, N_META)


def kernel(x_prompt, x_sample, meta_tokens, g_mix_pre, g_mix_post, g_ffn_pre, g_ffn_post, w_in, q_norm_b,
           k_norm_b, sink_a, w_branch_a, w_branch_b, w_out, w_ffn_up, w_ffn_down):
    b1, n1, _ = x_prompt.shape
    b2, n2, _ = x_sample.shape
    depth = w_in.shape[0]
    t = TOKEN_TILE
    assert n1 % t == 0 and n2 % t == 0 and (b1 * n1) % n2 == 0 and n1 % GRID_W == 0 and n2 % GRID_W == 0
    assert (b1 * n1) % DENSE_TILE == 0 and (b2 * n2) % DENSE_TILE == 0
    nb = b1 + b2
    n_real = b1 * n1 + b2 * n2
    ntok = -(-(n_real + nb * N_META) // DENSE_TILE) * DENSE_TILE
    tail = ntok - n_real
    assert tail % t == 0
    layout = (n_real, b1, n1, b2, n2)

    h = (x_prompt.reshape(b1 * n1, D_MODEL).astype(_F32), x_sample.reshape(b2 * n2, D_MODEL).astype(_F32),
         jnp.concatenate([jnp.tile(meta_tokens.astype(_F32), (nb, 1)),
                          jnp.zeros((tail - nb * N_META, D_MODEL), _F32)], axis=0))

    cost, sint = _rope_tables(layout, ntok)
    bias = _window_bias()
    w_rows, w_t = _rearranged_w_in(w_in)
    wa, wb, wo = w_branch_a.astype(_BF), w_branch_b.astype(_BF), w_out.astype(_BF)
    wup, wdown = w_ffn_up.astype(_BF), w_ffn_down.astype(_BF)
    row = lambda g: g.reshape(1, -1).astype(_F32)
    col = lambda g: g.reshape(-1, 1).astype(_F32)

    for l in range(depth):
        bound = (HEAD_DIM * SCALE * LOG2E * 1.02) * jnp.max(jnp.abs(q_norm_b[l])) * jnp.max(jnp.abs(k_norm_b[l]))
        unit = (jnp.arange(QK_ROWS - HEAD_DIM) == 0).astype(_F32).reshape(-1, 1)
        proj = _in_proj(h, row(g_mix_pre[l]), w_rows[l], w_t[l], cost, sint, col(q_norm_b[l]), col(k_norm_b[l]),
                        -bound.astype(_F32) * unit, unit)
        if l == 0:
            *proj, h = proj
        qat, qbt, kat, kbt, vat, vbt, ga, gb = proj
        sink = sink_a[l].astype(_F32) * LOG2E
        ka_tail, va_tail = _tail_features(kat, n_real, nb), _tail_features(vat, n_real, nb)
        kameta = jnp.transpose(ka_tail, (2, 0, 1, 3))
        vametat = jnp.transpose(va_tail, (2, 0, 1, 3))
        qa_meta = jnp.transpose(_tail_features(qat, n_real, nb), (2, 3, 0, 1)).reshape(nb * N_META, Q_COLS)
        qa_meta = jnp.pad(qa_meta, ((0, tail - nb * N_META), (0, 0)))

        oa, oa_tail = _window_attention(sink, qat, kat, vat, kameta, vametat, bias, qa_meta,
                                        jnp.zeros((tail, Q_COLS), _BF), ntok, layout)

        kbmeta = jnp.transpose(_tail_features(kbt, n_real, nb), (2, 0, 3, 1))
        vbmetat = jnp.transpose(_tail_features(vbt, n_real, nb), (2, 0, 1, 3))
        qmeta = _tail_features(qbt, n_real, nb).reshape(N_KV, GROUP, QK_ROWS, nb, N_META)
        qmeta = jnp.transpose(qmeta, (3, 0, 2, 1, 4)).reshape(nb, N_KV, QK_ROWS, GROUP * N_META)
        qmeta = jnp.pad(qmeta, ((0, 0), (0, 0), (0, 0), (0, V_LANES - GROUP * N_META)))

        glob = functools.partial(_global_attention, kbt=kbt, vbt=vbt, kmeta=kbmeta, vmetat=vbmetat)
        def mixer_b_maxfree():
            out = None
            tails = jnp.zeros((tail, Q_COLS), _BF)
            for row0, batches, batch0, n in ((0, b1, 0, n1), (b1 * n1, b2, b1, n2)):
                out, tails = _global_fused(qbt, kbt, vbt, kbmeta, vbmetat, qmeta, out, tails, ntok, tq=1024,
                                           q_row0=row0, batches=batches, kv_batch0=batch0, n=n)
            return out, tails

        def mixer_b_general():
            tails = glob(qmeta, prev=None, out_rows=tail, tq=N_META, q_row0=0, q_rows_per_batch=N_META,
                         grid_batches=b1, valid_batches=b1, kv_batch0=0, kv_row0=0, n=n1)
            tails = glob(qmeta, prev=tails, out_rows=tail, tq=N_META, q_row0=b1 * N_META, q_rows_per_batch=N_META,
                         grid_batches=tail // N_META - b1, valid_batches=b2, kv_batch0=b1, kv_row0=b1 * n1, n=n2)
            out = glob(qbt, prev=None, out_rows=ntok, tq=512, q_row0=0, q_rows_per_batch=n1, grid_batches=b1,
                       valid_batches=b1, kv_batch0=0, kv_row0=0, n=n1)
            out = glob(qbt, prev=out, out_rows=ntok, tq=512, q_row0=b1 * n1, q_rows_per_batch=n2, grid_batches=b2,
                       valid_batches=b2, kv_batch0=b1, kv_row0=b1 * n1, n=n2)
            return out, tails

        ob, ob_tail = lax.cond(bound <= SAFE_OFFSET_MAX, mixer_b_maxfree, mixer_b_general)

        if l < depth - 1:
            h = _mix_ffn(h, oa, oa_tail, ob, ob_tail, ga, gb, wa[l], wb[l], wo[l], row(g_mix_post[l]),
                         row(g_ffn_pre[l]), wup[l], wdown[l], row(g_ffn_post[l]))
        else:
            h = _mix(h, oa, oa_tail, ob, ob_tail, ga, gb, wa[l], wb[l], wo[l], row(g_mix_post[l]))
            h = _ffn(h, row(g_ffn_pre[l]), wup[l], wdown[l], row(g_ffn_post[l]), split_rows=(b1 * n1, b2 * n2))

    y_prompt, y_sample = h
    return (y_prompt.reshape(b1, n1, D_MODEL).astype(x_prompt.dtype),
            y_sample.reshape(b2, n2, D_MODEL).astype(x_sample.dtype))
```

```python
import functools
import math

import jax
import jax.numpy as jnp
import numpy as np
from jax import lax
from jax.experimental import pallas as pl
from jax.experimental.pallas import tpu as pltpu

D_MODEL = 1024
HEAD_DIM = 64
N_HEADS = 8
N_KV = 2
GROUP = N_HEADS // N_KV
Q_COLS = N_HEADS * HEAD_DIM
KV_COLS = N_KV * HEAD_DIM
N_META = 16
BLOCK = 128
GRID_W = 64
ROPE_BASE = 10000.0
ROPE_FREQS = HEAD_DIM // 4
D_FF = 2816
EPS = 1e-6
NEG_INF = -1e30
SCALE = HEAD_DIM ** -0.5
LOG2E = math.log2(math.e)

QK_ROWS = HEAD_DIM + 16
SAFE_OFFSET_MAX = 40.0
V_LANES = 128
TOKEN_TILE = 512
DENSE_TILE = 1024
WINDOW_TILE = 1024
VMEM_LIMIT = 56 * 1024 * 1024

_C_GA = 0
_C_GB = _C_GA + D_MODEL
_C_END = _C_GB + D_MODEL
_R_KA = 0
_R_KB = _R_KA + KV_COLS
_R_VB = _R_KB + KV_COLS
_R_QB = _R_VB + N_KV * V_LANES
_R_VA = _R_QB + Q_COLS
_R_QA = _R_VA + N_KV * V_LANES
_R_END = _R_QA + Q_COLS

_BF = jnp.bfloat16
_F32 = jnp.float32


def _dot(a, b):
    return jnp.dot(a, b, preferred_element_type=_F32)


def _dot_nt(a, b):
    return lax.dot_general(a, b, (((1,), (1,)), ((), ())), preferred_element_type=_F32)


def _dot_tn(a, b):
    return lax.dot_general(a, b, (((0,), (0,)), ((), ())), preferred_element_type=_F32)


def _params(sem, vmem=VMEM_LIMIT):
    return pltpu.CompilerParams(dimension_semantics=sem, vmem_limit_bytes=vmem)


def _const_spec(shape):
    nd = len(shape)
    return pl.BlockSpec(shape, lambda *_: (0,) * nd, pipeline_mode=pl.Buffered(1))


def _in_proj_body(h_ref, g_ref, w_ref, wt_ref, cost_ref, sint_ref, qg_ref, kg_ref, qx_ref, kx_ref,
                  qat_ref, qbt_ref, kat_ref, kbt_ref, vat_ref, vbt_ref, ga_ref=None, gb_ref=None):
    f = ROPE_FREQS
    ones_row = (lax.broadcasted_iota(jnp.int32, (V_LANES, 1), 0) == HEAD_DIM).astype(_F32)
    q_extra = jnp.broadcast_to(qx_ref[...], (QK_ROWS - HEAD_DIM, TOKEN_TILE)).astype(_BF)
    k_extra = jnp.broadcast_to(kx_ref[...], (QK_ROWS - HEAD_DIM, TOKEN_TILE)).astype(_BF)

    for cc in range(h_ref.shape[0] // TOKEN_TILE):
        rows = slice(cc * TOKEN_TILE, (cc + 1) * TOKEN_TILE)
        h = h_ref[rows, :]
        ms = jnp.mean(h * h, axis=-1, keepdims=True)
        xn = (h * lax.rsqrt(ms + EPS) * g_ref[...]).astype(_BF)

        if ga_ref is not None:
            ga_ref[rows, :] = jax.nn.sigmoid(_dot(xn, w_ref[:, _C_GA:_C_GA + D_MODEL])).astype(_BF)
            gb_ref[rows, :] = jax.nn.sigmoid(_dot(xn, w_ref[:, _C_GB:_C_GB + D_MODEL])).astype(_BF)

        tt = _dot_nt(wt_ref[...], xn)
        cost = cost_ref[:, rows]
        sint = sint_ref[:, rows]

        def norm_rope(x, gain):
            x = x * lax.rsqrt(jnp.mean(x * x, axis=0, keepdims=True) + EPS) * gain
            partner = jnp.concatenate([x[f:2 * f], x[0:f], x[3 * f:4 * f], x[2 * f:3 * f]], axis=0)
            return x * cost + partner * sint

        for j in range(N_KV):
            kat_ref[j, cc] = tt[_R_KA + j * HEAD_DIM:_R_KA + (j + 1) * HEAD_DIM].astype(_BF)
            k = norm_rope(tt[_R_KB + j * HEAD_DIM:_R_KB + (j + 1) * HEAD_DIM], kg_ref[...])
            kbt_ref[j, cc, :HEAD_DIM, :] = k.astype(_BF)
            kbt_ref[j, cc, HEAD_DIM:, :] = k_extra
            vat_ref[j, cc] = (tt[_R_VA + j * V_LANES:_R_VA + (j + 1) * V_LANES] + ones_row).astype(_BF)
            vbt_ref[j, cc] = (tt[_R_VB + j * V_LANES:_R_VB + (j + 1) * V_LANES] + ones_row).astype(_BF)
        for hd in range(N_HEADS):
            qa = tt[_R_QA + hd * HEAD_DIM:_R_QA + (hd + 1) * HEAD_DIM] * (SCALE * LOG2E)
            qat_ref[hd, cc] = qa.astype(_BF)
            q = norm_rope(tt[_R_QB + hd * HEAD_DIM:_R_QB + (hd + 1) * HEAD_DIM], qg_ref[...])
            qbt_ref[hd, cc, :HEAD_DIM, :] = (q * (SCALE * LOG2E)).astype(_BF)
            qbt_ref[hd, cc, HEAD_DIM:, :] = q_extra


def _in_proj_first_body(blocks, xp_ref, xs_ref, tail_ref, *refs):
    p, sm = blocks
    i = pl.program_id(0)
    *rest, h_ref = refs
    h_ref[...] = jnp.where(i < p, xp_ref[...], jnp.where(i < p + sm, xs_ref[...], tail_ref[...]))
    _in_proj_body(h_ref, *rest)


def _gates_body(h_ref, g_ref, w_ref, ga_ref, gb_ref):
    h = h_ref[...]
    ms = jnp.mean(h * h, axis=-1, keepdims=True)
    xn = (h * lax.rsqrt(ms + EPS) * g_ref[...]).astype(_BF)
    ga_ref[...] = jax.nn.sigmoid(_dot(xn, w_ref[:, _C_GA:_C_GA + D_MODEL])).astype(_BF)
    gb_ref[...] = jax.nn.sigmoid(_dot(xn, w_ref[:, _C_GB:_C_GB + D_MODEL])).astype(_BF)


def _gates(h, g, w):
    ntok = h.shape[0]
    t = DENSE_TILE
    row = lambda i: (i, 0)
    shape = jax.ShapeDtypeStruct((ntok, D_MODEL), _BF)
    return pl.pallas_call(
        _gates_body,
        grid=(ntok // t,),
        in_specs=[pl.BlockSpec((t, D_MODEL), row), _const_spec((1, D_MODEL)), _const_spec((D_MODEL, _C_END))],
        out_specs=(pl.BlockSpec((t, D_MODEL), row), pl.BlockSpec((t, D_MODEL), row)),
        out_shape=(shape, shape),
        compiler_params=_params(("parallel",)),
        name="gates",
    )(h, g, w)


def _in_proj(h, g, w, wt, cost, sint, qg, kg, qx, kx, with_gates=True):
    first = isinstance(h, tuple)
    t = TOKEN_TILE if first else DENSE_TILE
    ntok = sum(x.shape[0] for x in h) if first else h.shape[0]
    per = t // TOKEN_TILE
    nch = ntok // TOKEN_TILE
    row = lambda i: (i, 0)
    chunk = lambda i: (0, i, 0, 0)
    feat = lambda heads, rows: (jax.ShapeDtypeStruct((heads, nch, rows, TOKEN_TILE), _BF),
                                pl.BlockSpec((heads, per, rows, TOKEN_TILE), chunk))
    outs = [
        feat(N_HEADS, HEAD_DIM),
        feat(N_HEADS, QK_ROWS),
        feat(N_KV, HEAD_DIM),
        feat(N_KV, QK_ROWS),
        feat(N_KV, V_LANES),
        feat(N_KV, V_LANES),
        (jax.ShapeDtypeStruct((ntok, D_MODEL), _BF), pl.BlockSpec((t, D_MODEL), row)),
        (jax.ShapeDtypeStruct((ntok, D_MODEL), _BF), pl.BlockSpec((t, D_MODEL), row)),
    ]
    if not with_gates:
        outs = outs[:6]
    if first:
        p, sm, tl = (x.shape[0] // t for x in h)
        body = functools.partial(_in_proj_first_body, (p, sm))
        h_specs = [pl.BlockSpec((t, D_MODEL), lambda i: (jnp.minimum(i, p - 1), 0)),
                   pl.BlockSpec((t, D_MODEL), lambda i: (jnp.clip(i - p, 0, sm - 1), 0)),
                   pl.BlockSpec((t, D_MODEL), lambda i: (jnp.clip(i - p - sm, 0, tl - 1), 0))]
        outs.append((jax.ShapeDtypeStruct((ntok, D_MODEL), _F32), pl.BlockSpec((t, D_MODEL), row)))
        h_args = list(h)
    else:
        body, h_specs, h_args = _in_proj_body, [pl.BlockSpec((t, D_MODEL), row)], [h]
    return pl.pallas_call(
        body,
        grid=(ntok // t,),
        in_specs=h_specs + [
            _const_spec((1, D_MODEL)),
            _const_spec((D_MODEL, _C_END)),
            _const_spec((_R_END, D_MODEL)),
            pl.BlockSpec((HEAD_DIM, t), lambda i: (0, i)),
            pl.BlockSpec((HEAD_DIM, t), lambda i: (0, i)),
            _const_spec((HEAD_DIM, 1)),
            _const_spec((HEAD_DIM, 1)),
            _const_spec((QK_ROWS - HEAD_DIM, 1)),
            _const_spec((QK_ROWS - HEAD_DIM, 1)),
        ],
        out_specs=tuple(o[1] for o in outs),
        out_shape=tuple(o[0] for o in outs),
        compiler_params=_params(("parallel",)),
        name="in_proj_first" if first else ("in_proj" if with_gates else "qkv_proj"),
    )(*h_args, g, w, wt, cost, sint, qg, kg, qx, kx)


_WINDOW_LOOKAHEAD = 3


def _stack_heads(q, j):
    base = j * GROUP * HEAD_DIM
    return jnp.concatenate([q[:, base + g * HEAD_DIM: base + (g + 1) * HEAD_DIM] for g in range(GROUP)], axis=0)


def _sink_rows(sink_ref, j, rows, lanes):
    return jnp.concatenate([jnp.full((rows, lanes), sink_ref[j * GROUP + g], _F32) for g in range(GROUP)], axis=0)


def _window_body(geom, sink_ref, q_ref, kmain_ref, kprev_ref, knext_ref, vmain_ref, vprev_ref, vnext_ref,
                 kmeta_ref, vmeta_ref, bias_ref, qmeta_ref, _tail_ref, o_ref, otail_ref):
    t1, tpb1, tpb2 = geom
    t = pl.program_id(0)
    in_prompt = t < t1
    is_first = jnp.where(in_prompt, t % tpb1 == 0, (t - t1) % tpb2 == 0)
    is_last = jnp.where(in_prompt, t % tpb1 == tpb1 - 1, (t - t1) % tpb2 == tpb2 - 1)
    nblk = WINDOW_TILE // BLOCK
    per_chunk = TOKEN_TILE // BLOCK

    def softmax_values(j, r, s, sm, vwin, vmeta, sink):
        m = jnp.maximum(jnp.maximum(jnp.max(s, axis=0, keepdims=True), jnp.max(sm, axis=0, keepdims=True)), sink)
        acc = (_dot(vwin, jnp.exp2(s - m).astype(_BF))
               + _dot(vmeta, jnp.exp2(sm - m).astype(_BF)))
        l = acc[HEAD_DIM:HEAD_DIM + 1] + jnp.exp2(sink - m)
        o = (acc / l).T.astype(_BF)
        for g in range(GROUP):
            c0 = (j * GROUP + g) * HEAD_DIM
            o_ref[r * BLOCK:(r + 1) * BLOCK, c0:c0 + HEAD_DIM] = o[g * BLOCK:(g + 1) * BLOCK, :HEAD_DIM]

    pending = []
    for j in range(N_KV):
        chunks = range(kmain_ref.shape[1])
        kcat = jnp.concatenate([kprev_ref[j, 0]] + [kmain_ref[j, cc] for cc in chunks] + [knext_ref[j, 0]], axis=1)
        vcat = jnp.concatenate([vprev_ref[j, 0]] + [vmain_ref[j, cc] for cc in chunks] + [vnext_ref[j, 0]], axis=1)
        kmeta = kmeta_ref[0, j]
        vmeta = vmeta_ref[0, j]
        sink = jnp.concatenate([jnp.full((1, BLOCK), sink_ref[j * GROUP + g], _F32) for g in range(GROUP)], axis=1)
        for r in range(nblk):
            variant = 0
            if r == 0:
                variant = jnp.where(is_first, 1, 0)
            if r == nblk - 1:
                variant = jnp.where(is_last, 2, variant)
            c0 = (r % per_chunk) * BLOCK
            qt = jnp.concatenate([q_ref[j * GROUP + g, r // per_chunk][:, c0:c0 + BLOCK] for g in range(GROUP)],
                                 axis=1)
            s = _dot_tn(kcat[:, r * BLOCK:(r + 3) * BLOCK], qt) + bias_ref[variant, j]
            sm = _dot_tn(kmeta, qt)
            pending.append((j, r, s, sm, vcat[:, r * BLOCK:(r + 3) * BLOCK], vmeta, sink))
            if len(pending) > _WINDOW_LOOKAHEAD:
                softmax_values(*pending.pop(0))
    for item in pending:
        softmax_values(*item)

    @pl.when(is_first)
    def _():
        q_all = qmeta_ref[...]
        for j in range(N_KV):
            qs = _stack_heads(q_all, j)
            sink = _sink_rows(sink_ref, j, N_META, 1)
            sm = _dot(qs, kmeta_ref[0, j])
            sf = _dot(qs, kmain_ref[j, 0][:, :BLOCK])
            m = jnp.maximum(jnp.maximum(sm.max(axis=1, keepdims=True), sf.max(axis=1, keepdims=True)), sink)
            acc = (_dot_nt(jnp.exp2(sm - m).astype(_BF), vmeta_ref[0, j])
                   + _dot_nt(jnp.exp2(sf - m).astype(_BF), vmain_ref[j, 0][:, :BLOCK]))
            l = acc[:, HEAD_DIM:HEAD_DIM + 1] + jnp.exp2(sink - m)
            o = (acc[:, :HEAD_DIM] / l).astype(_BF)
            for g in range(GROUP):
                c0 = (j * GROUP + g) * HEAD_DIM
                otail_ref[:, c0:c0 + HEAD_DIM] = o[g * N_META:(g + 1) * N_META]


def _window_attention(sink, qat, kat, vat, kmeta, vmetat, bias, q_meta, tail, out_rows, layout):
    n_real, b1, n1, b2, n2 = layout
    t = WINDOW_TILE
    per = t // TOKEN_TILE
    nch = kat.shape[1]
    assert n1 % t == 0 and n2 % t == 0
    t1 = b1 * n1 // t
    tpb1, tpb2 = n1 // t, n2 // t
    sub = TOKEN_TILE // BLOCK

    def bid(i):
        return jnp.where(i < t1, i // tpb1, b1 + (i - t1) // tpb2)

    main = lambda i: (0, i, 0, 0)
    prev = lambda i: (0, jnp.maximum(i * per - 1, 0), 0, sub - 1)
    nxt = lambda i: (0, jnp.minimum((i + 1) * per, nch - 1), 0, 0)
    return pl.pallas_call(
        functools.partial(_window_body, (t1, tpb1, tpb2)),
        grid=(n_real // t,),
        in_specs=[
            pl.BlockSpec(memory_space=pltpu.SMEM),
            pl.BlockSpec((N_HEADS, per, HEAD_DIM, TOKEN_TILE), main),
            pl.BlockSpec((N_KV, per, HEAD_DIM, TOKEN_TILE), main),
            pl.BlockSpec((N_KV, 1, HEAD_DIM, BLOCK), prev),
            pl.BlockSpec((N_KV, 1, HEAD_DIM, BLOCK), nxt),
            pl.BlockSpec((N_KV, per, V_LANES, TOKEN_TILE), main),
            pl.BlockSpec((N_KV, 1, V_LANES, BLOCK), prev),
            pl.BlockSpec((N_KV, 1, V_LANES, BLOCK), nxt),
            pl.BlockSpec((1, N_KV, HEAD_DIM, N_META), lambda i: (bid(i), 0, 0, 0)),
            pl.BlockSpec((1, N_KV, V_LANES, N_META), lambda i: (bid(i), 0, 0, 0)),
            _const_spec((3, N_KV, 3 * BLOCK, GROUP * BLOCK)),
            pl.BlockSpec((N_META, Q_COLS), lambda i: (bid(i), 0)),
            pl.BlockSpec(memory_space=pl.ANY),
        ],
        out_specs=(pl.BlockSpec((t, Q_COLS), lambda i: (i, 0)),
                   pl.BlockSpec((N_META, Q_COLS), lambda i: (bid(i), 0))),
        out_shape=(jax.ShapeDtypeStruct((out_rows, Q_COLS), _BF), jax.ShapeDtypeStruct(tail.shape, _BF)),
        input_output_aliases={12: 1},
        compiler_params=_params(("arbitrary",)),
        name="window_attn",
    )(sink, qat, kat, kat, kat, vat, vat, vat, kmeta, vmetat, bias, q_meta, tail)


_KEY_UNROLL = 4
_QUERY_BLOCK = 256
_SCORE_LOOKAHEAD = 3


def _global_body(nvalid, nchunks, tq, q_ref, kt_ref, vt_ref, kmeta_ref, vmetat_ref, _prev_ref, o_ref,
                 qt_ref, s0_ref, s1_ref, m_ref, acc_ref):
    b = pl.program_id(0)

    cols = qt_ref.shape[1]
    blocks = [slice(c0, min(c0 + _QUERY_BLOCK, cols)) for c0 in range(0, cols, _QUERY_BLOCK)]

    def scores(c, s_ref):
        for sl in blocks:
            s_ref[:, sl] = _dot_tn(kt_ref[0, c], qt_ref[:, sl])

    def softmax_pv(c, s_ref):
        for sl in blocks:
            s = s_ref[:, sl]
            m_prev = m_ref[:, sl]
            m_new = jnp.maximum(m_prev, jnp.max(s, axis=0, keepdims=True))
            p = jnp.exp2(s - m_new).astype(_BF)
            acc_ref[:, sl] = jnp.exp2(m_prev - m_new) * acc_ref[:, sl] + _dot(vt_ref[0, c], p)
            m_ref[:, sl] = m_new

    @pl.when(b < nvalid)
    def _():
        _stack_queries(q_ref, qt_ref)
        sm = _dot(kmeta_ref[0, 0], qt_ref[...])
        m0 = jnp.max(sm, axis=0, keepdims=True)
        m_ref[...] = m0
        acc_ref[...] = _dot(vmetat_ref[0, 0], jnp.exp2(sm - m0).astype(_BF))
        bufs = (s0_ref, s1_ref)
        scores(0, s0_ref)

        unroll = _KEY_UNROLL if nchunks > _KEY_UNROLL else 2

        def group(i, carry):
            c0 = unroll * i
            for u in range(unroll):
                scores(c0 + u + 1, bufs[(u + 1) % 2])
                softmax_pv(c0 + u, bufs[u % 2])
            return carry

        full = (nchunks - 1) // unroll
        lax.fori_loop(0, full, group, 0)
        for c in range(full * unroll, nchunks):
            if c + 1 < nchunks:
                scores(c + 1, bufs[(c + 1) % 2])
            softmax_pv(c, bufs[c % 2])
        _write_output(acc_ref, o_ref, tq)

    @pl.when(b >= nvalid)
    def _():
        o_ref[...] = jnp.zeros(o_ref.shape, o_ref.dtype)


def _maxfree_chunks(kt_ref, vt_ref, q_ref, acc_ref, cs):
    cols = q_ref.shape[1]
    blocks = [slice(c0, min(c0 + _QUERY_BLOCK, cols)) for c0 in range(0, cols, _QUERY_BLOCK)]

    def values(c, sl, s):
        acc_ref[:, sl] += _dot(vt_ref[0, c], jnp.exp2(s).astype(_BF))

    pending = []
    for c in cs:
        for sl in blocks:
            pending.append((c, sl, _dot_tn(kt_ref[0, c], q_ref[:, sl])))
            if len(pending) > _SCORE_LOOKAHEAD:
                values(*pending.pop(0))
    for item in pending:
        values(*item)


def _maxfree_attend(nchunks, kt_ref, vt_ref, kmeta_ref, vmetat_ref, q_ref, acc_ref):
    acc_ref[...] = _dot(vmetat_ref[0, 0], jnp.exp2(_dot(kmeta_ref[0, 0], q_ref[...])).astype(_BF))

    def group(i, carry):
        _maxfree_chunks(kt_ref, vt_ref, q_ref, acc_ref, [_KEY_UNROLL * i + u for u in range(_KEY_UNROLL)])
        return carry

    full = nchunks // _KEY_UNROLL
    lax.fori_loop(0, full, group, 0)
    if full * _KEY_UNROLL < nchunks:
        _maxfree_chunks(kt_ref, vt_ref, q_ref, acc_ref, range(full * _KEY_UNROLL, nchunks))


def _global_fused_body(nchunks, tq, has_prev, q_ref, kt_ref, vt_ref, kmeta_ref, vmetat_ref, qm_ref, *refs):
    o_ref, otail_ref, qt_ref, acc_ref, accm_ref = refs[2 if has_prev else 1:]
    _stack_queries(q_ref, qt_ref)
    _maxfree_attend(nchunks, kt_ref, vt_ref, kmeta_ref, vmetat_ref, qt_ref, acc_ref)
    _write_output(acc_ref, o_ref, tq)

    @pl.when(pl.program_id(2) == 0)
    def _():
        _maxfree_attend(nchunks, kt_ref, vt_ref, kmeta_ref, vmetat_ref, qm_ref.at[0, 0], accm_ref)
        _write_output(accm_ref, otail_ref, N_META)


def _global_fused(qbt, kbt, vbt, kmeta, vmetat, qmeta, prev, prev_tail, out_rows, *, tq, q_row0, batches,
                  kv_batch0, n):
    t = TOKEN_TILE
    nchunks = n // t
    qt = n // tq
    cols = GROUP * tq
    assert tq % t == 0 and q_row0 % tq == 0 and n % tq == 0 and q_row0 % n == 0
    kv = lambda b, j, i: (j, q_row0 // n + b, 0, 0)
    meta = lambda b, j, i: (kv_batch0 + b, j, 0, 0)
    in_specs = [
        pl.BlockSpec((GROUP, tq // t, QK_ROWS, t), lambda b, j, i: (j, (q_row0 + b * n) // tq + i, 0, 0)),
        pl.BlockSpec((1, nchunks, QK_ROWS, t), kv),
        pl.BlockSpec((1, nchunks, V_LANES, t), kv),
        pl.BlockSpec((1, 1, N_META, QK_ROWS), meta),
        pl.BlockSpec((1, 1, V_LANES, N_META), meta),
        pl.BlockSpec((1, 1, QK_ROWS, V_LANES), meta),
        pl.BlockSpec(memory_space=pl.ANY),
    ]
    args = [qbt, kbt, vbt, kmeta, vmetat, qmeta, prev_tail]
    aliases = {6: 1}
    if prev is not None:
        in_specs.insert(6, pl.BlockSpec(memory_space=pl.ANY))
        args.insert(6, prev)
        aliases = {6: 0, 7: 1}
    return pl.pallas_call(
        functools.partial(_global_fused_body, nchunks, tq, prev is not None),
        grid=(batches, N_KV, qt),
        in_specs=in_specs,
        out_specs=(pl.BlockSpec((tq, GROUP * HEAD_DIM), lambda b, j, i: (q_row0 // tq + b * qt + i, j)),
                   pl.BlockSpec((N_META, GROUP * HEAD_DIM), lambda b, j, i: (kv_batch0 + b, j))),
        out_shape=(jax.ShapeDtypeStruct((out_rows, Q_COLS), _BF),
                   jax.ShapeDtypeStruct(prev_tail.shape, _BF)),
        scratch_shapes=[pltpu.VMEM((QK_ROWS, cols), _BF), pltpu.VMEM((V_LANES, cols), _F32),
                        pltpu.VMEM((V_LANES, V_LANES), _F32)],
        input_output_aliases=aliases,
        compiler_params=_params(("parallel", "parallel", "arbitrary")),
        name=f"global_attn_fused_tq{tq}_n{n}",
    )(*args)


def _stack_queries(q_ref, qt_ref):
    if q_ref.shape[0] == GROUP:
        qt_ref[...] = jnp.concatenate([q_ref[g, cc] for g in range(GROUP) for cc in range(q_ref.shape[1])], axis=1)
    else:
        qt_ref[...] = q_ref[0, 0]


def _write_output(acc_ref, o_ref, tq):
    acc = acc_ref[...]
    o = (acc / acc[HEAD_DIM:HEAD_DIM + 1]).T.astype(_BF)
    for g in range(GROUP):
        o_ref[:, g * HEAD_DIM:(g + 1) * HEAD_DIM] = o[g * tq:(g + 1) * tq, :HEAD_DIM]


def _global_attention(q, kbt, vbt, kmeta, vmetat, prev, out_rows, *, tq, q_row0, q_rows_per_batch, grid_batches,
                      valid_batches, kv_batch0, kv_row0, n):
    t = TOKEN_TILE
    nchunks = n // t
    qt = q_rows_per_batch // tq
    assert q_row0 % tq == 0 and kv_row0 % n == 0 and q_rows_per_batch % tq == 0
    kvb = lambda b: jnp.minimum(b, valid_batches - 1)
    if tq == N_META:
        cols = V_LANES
        q_spec = pl.BlockSpec((1, 1, QK_ROWS, cols), lambda b, j, i: (kv_batch0 + kvb(b), j, 0, 0))
    else:
        cols = GROUP * tq
        assert tq % t == 0
        q_spec = pl.BlockSpec((GROUP, tq // t, QK_ROWS, t),
                              lambda b, j, i: (j, (q_row0 + b * q_rows_per_batch) // tq + i, 0, 0))
    in_specs = [
        q_spec,
        pl.BlockSpec((1, nchunks, QK_ROWS, t), lambda b, j, i: (j, kv_row0 // n + kvb(b), 0, 0)),
        pl.BlockSpec((1, nchunks, V_LANES, t), lambda b, j, i: (j, kv_row0 // n + kvb(b), 0, 0)),
        pl.BlockSpec((1, 1, N_META, QK_ROWS), lambda b, j, i: (kv_batch0 + kvb(b), j, 0, 0)),
        pl.BlockSpec((1, 1, V_LANES, N_META), lambda b, j, i: (kv_batch0 + kvb(b), j, 0, 0)),
    ]
    args = [q, kbt, vbt, kmeta, vmetat]
    aliases = {}
    body = functools.partial(_global_body, valid_batches, nchunks, tq)
    if prev is None:
        body = functools.partial(_global_body_noprev, body)
    else:
        in_specs.append(pl.BlockSpec(memory_space=pl.ANY))
        args.append(prev)
        aliases = {5: 0}
    return pl.pallas_call(
        body,
        grid=(grid_batches, N_KV, qt),
        in_specs=in_specs,
        out_specs=pl.BlockSpec((tq, GROUP * HEAD_DIM), lambda b, j, i: (q_row0 // tq + b * qt + i, j)),
        out_shape=jax.ShapeDtypeStruct((out_rows, Q_COLS), _BF),
        scratch_shapes=[pltpu.VMEM((QK_ROWS, cols), _BF), pltpu.VMEM((t, cols), _F32), pltpu.VMEM((t, cols), _F32),
                        pltpu.VMEM((1, cols), _F32), pltpu.VMEM((V_LANES, cols), _F32)],
        input_output_aliases=aliases,
        compiler_params=_params(("parallel", "parallel", "arbitrary")),
        name=f"global_attn_tq{tq}_n{n}",
    )(*args)


def _global_body_noprev(body, q_ref, kt_ref, vt_ref, kmeta_ref, vmetat_ref, o_ref, *scratch):
    body(q_ref, kt_ref, vt_ref, kmeta_ref, vmetat_ref, None, o_ref, *scratch)


def _mix_body(real_tiles, h_ref, oa_ref, oatail_ref, ob_ref, obtail_ref, ga_ref, gb_ref, wa_ref, wb_ref, wo_ref,
              g_ref, o_ref):
    in_tail = pl.program_id(0) >= real_tiles
    oa = jnp.where(in_tail, oatail_ref[...], oa_ref[...])
    ob = jnp.where(in_tail, obtail_ref[...], ob_ref[...])
    mix = (ga_ref[...].astype(_F32) * _dot(oa, wa_ref[...])
           + gb_ref[...].astype(_F32) * _dot(ob, wb_ref[...]))
    u = _dot(mix.astype(_BF), wo_ref[...])
    ms = jnp.mean(u * u, axis=-1, keepdims=True)
    o_ref[...] = h_ref[...] + u * lax.rsqrt(ms + EPS) * g_ref[...]


def _mix(h, oa, oa_tail, ob, ob_tail, ga, gb, wa, wb, wo, g):
    ntok = h.shape[0]
    t = DENSE_TILE
    real_tiles = (ntok - ob_tail.shape[0]) // t
    assert real_tiles * t + ob_tail.shape[0] == ntok and ob_tail.shape[0] % t == 0
    row = lambda i: (i, 0)
    return pl.pallas_call(
        functools.partial(_mix_body, real_tiles),
        grid=(ntok // t,),
        in_specs=[
            pl.BlockSpec((t, D_MODEL), row),
            pl.BlockSpec((t, Q_COLS), lambda i: (jnp.minimum(i, real_tiles - 1), 0)),
            pl.BlockSpec((t, Q_COLS), lambda i: (jnp.maximum(i - real_tiles, 0), 0)),
            pl.BlockSpec((t, Q_COLS), lambda i: (jnp.minimum(i, real_tiles - 1), 0)),
            pl.BlockSpec((t, Q_COLS), lambda i: (jnp.maximum(i - real_tiles, 0), 0)),
            pl.BlockSpec((t, D_MODEL), row), pl.BlockSpec((t, D_MODEL), row),
            _const_spec((Q_COLS, D_MODEL)), _const_spec((Q_COLS, D_MODEL)),
            _const_spec((D_MODEL, D_MODEL)), _const_spec((1, D_MODEL)),
        ],
        out_specs=pl.BlockSpec((t, D_MODEL), row),
        out_shape=jax.ShapeDtypeStruct(h.shape, _F32),
        input_output_aliases={0: 0},
        compiler_params=_params(("parallel",)),
        name="branch_mix",
    )(h, oa, oa_tail, ob, ob_tail, ga, gb, wa, wb, wo, g)


_FF_CHUNK = 256


def _ffn_body(h_ref, gpre_ref, wup_ref, wdown_ref, gpost_ref, o_ref, act_ref):
    h = h_ref[...]
    ms = jnp.mean(h * h, axis=-1, keepdims=True)
    xn = (h * lax.rsqrt(ms + EPS) * gpre_ref[...]).astype(_BF)
    for c in range(0, D_FF, _FF_CHUNK):
        a = _dot(xn, wup_ref[:, c:c + _FF_CHUNK])
        b = _dot(xn, wup_ref[:, D_FF + c:D_FF + c + _FF_CHUNK])
        act_ref[:, c:c + _FF_CHUNK] = (a * jax.nn.sigmoid(a) * b).astype(_BF)
    u = _dot(act_ref[...], wdown_ref[...])
    ms = jnp.mean(u * u, axis=-1, keepdims=True)
    o_ref[...] = h + u * lax.rsqrt(ms + EPS) * gpost_ref[...]


def _ffn_last_body(blocks, h_ref, gpre_ref, wup_ref, wdown_ref, gpost_ref, yp_ref, ys_ref, act_ref):
    p, sm = blocks
    i = pl.program_id(0)

    @pl.when(i < p)
    def _():
        _ffn_body(h_ref, gpre_ref, wup_ref, wdown_ref, gpost_ref, yp_ref, act_ref)

    @pl.when(jnp.logical_and(i >= p, i < p + sm))
    def _():
        _ffn_body(h_ref, gpre_ref, wup_ref, wdown_ref, gpost_ref, ys_ref, act_ref)


def _ffn(h, gpre, wup, wdown, gpost, split_rows=None):
    ntok = h.shape[0]
    t = DENSE_TILE
    row = lambda i: (i, 0)
    if split_rows is None:
        body, aliases = _ffn_body, {0: 0}
        out_specs = pl.BlockSpec((t, D_MODEL), row)
        out_shape = jax.ShapeDtypeStruct(h.shape, _F32)
    else:
        p, sm = (r // t for r in split_rows)
        assert p * t == split_rows[0] and sm * t == split_rows[1]
        body, aliases = functools.partial(_ffn_last_body, (p, sm)), {}
        out_specs = (pl.BlockSpec((t, D_MODEL), lambda i: (jnp.minimum(i, p - 1), 0)),
                     pl.BlockSpec((t, D_MODEL), lambda i: (jnp.clip(i - p, 0, sm - 1), 0)))
        out_shape = (jax.ShapeDtypeStruct((split_rows[0], D_MODEL), _F32),
                     jax.ShapeDtypeStruct((split_rows[1], D_MODEL), _F32))
    return pl.pallas_call(
        body,
        grid=(ntok // t,),
        in_specs=[
            pl.BlockSpec((t, D_MODEL), row),
            _const_spec((1, D_MODEL)),
            _const_spec((D_MODEL, 2 * D_FF)),
            _const_spec((D_FF, D_MODEL)),
            _const_spec((1, D_MODEL)),
        ],
        out_specs=out_specs,
        out_shape=out_shape,
        scratch_shapes=[pltpu.VMEM((t, D_FF), _BF)],
        input_output_aliases=aliases,
        compiler_params=_params(("arbitrary",)),
        name="swiglu_ffn" if split_rows is None else "swiglu_ffn_last",
    )(h, gpre, wup, wdown, gpost)


def _rope_tables(layout, ntok):
    n_real, b1, n1, b2, n2 = layout
    idx = np.zeros((ntok,), np.int64)
    idx[:b1 * n1] = np.arange(b1 * n1) % n1
    idx[b1 * n1:n_real] = np.arange(b2 * n2) % n2
    rows = (idx // GRID_W).astype(np.float32)
    cols = (idx % GRID_W).astype(np.float32)
    rows[n_real:] = 0.0
    cols[n_real:] = 0.0
    freqs = ROPE_BASE ** (-jnp.arange(ROPE_FREQS, dtype=_F32) / ROPE_FREQS)
    ang_r = jnp.asarray(rows)[:, None] * freqs[None, :]
    ang_c = jnp.asarray(cols)[:, None] * freqs[None, :]
    cr, sr, cc, sc = jnp.cos(ang_r), jnp.sin(ang_r), jnp.cos(ang_c), jnp.sin(ang_c)
    cos64 = jnp.concatenate([cr, cr, cc, cc], axis=1)
    sin64 = jnp.concatenate([-sr, sr, -sc, sc], axis=1)
    return cos64.T, sin64.T


def _window_bias():
    slopes = 2.0 ** (-8.0 * np.arange(1, N_HEADS + 1, dtype=np.float64) / N_HEADS)
    rel = np.arange(BLOCK)[:, None] - (np.arange(3 * BLOCK) - BLOCK)[None, :]
    dist = np.abs(rel)
    band = dist <= BLOCK
    base = np.where(band[None], -slopes[:, None, None] * dist[None].astype(np.float64) * LOG2E, NEG_INF)
    no_prev = base.copy()
    no_prev[:, :, :BLOCK] = NEG_INF
    no_next = base.copy()
    no_next[:, :, 2 * BLOCK:3 * BLOCK] = NEG_INF
    out = np.stack([base, no_prev, no_next]).reshape(3, N_KV, GROUP * BLOCK, 3 * BLOCK)
    return jnp.asarray(np.swapaxes(out, -1, -2), _F32)


def _rearranged_w_in(w_in):
    o = np.cumsum([0, Q_COLS, KV_COLS, KV_COLS, Q_COLS, KV_COLS, KV_COLS, D_MODEL, D_MODEL])
    qa, ka, va, qb, kb, vb, ga, gb = [w_in[..., o[i]:o[i + 1]] for i in range(8)]

    def pad_v(v):
        z = jnp.zeros(v.shape[:-1] + (V_LANES - HEAD_DIM,), v.dtype)
        return jnp.concatenate([v[..., :HEAD_DIM], z, v[..., HEAD_DIM:], z], axis=-1)

    w_rows = jnp.concatenate([ga, gb], axis=-1).astype(_BF)
    w_t = jnp.swapaxes(jnp.concatenate([ka, kb, pad_v(vb), qb, pad_v(va), qa], axis=-1), -1, -2).astype(_BF)
    return w_rows, w_t


def _tail_features(xt, n_real, nb):
    tail = xt[:, n_real // TOKEN_TILE:]
    tail = jnp.moveaxis(tail, 2, 1).reshape(xt.shape[0], xt.shape[2], -1)[:, :, :nb * N_META]
    return tail.reshape(xt.shape[0], xt.shape[2], nb, N_META)


def kernel(x_prompt, x_sample, meta_tokens, g_mix_pre, g_mix_post, g_ffn_pre, g_ffn_post, w_in, q_norm_b,
           k_norm_b, sink_a, w_branch_a, w_branch_b, w_out, w_ffn_up, w_ffn_down):
    b1, n1, _ = x_prompt.shape
    b2, n2, _ = x_sample.shape
    depth = w_in.shape[0]
    t = TOKEN_TILE
    assert n1 % t == 0 and n2 % t == 0 and (b1 * n1) % n2 == 0 and n1 % GRID_W == 0 and n2 % GRID_W == 0
    assert (b1 * n1) % DENSE_TILE == 0 and (b2 * n2) % DENSE_TILE == 0
    nb = b1 + b2
    n_real = b1 * n1 + b2 * n2
    ntok = -(-(n_real + nb * N_META) // DENSE_TILE) * DENSE_TILE
    tail = ntok - n_real
    assert tail % t == 0
    layout = (n_real, b1, n1, b2, n2)

    h = (x_prompt.reshape(b1 * n1, D_MODEL).astype(_F32), x_sample.reshape(b2 * n2, D_MODEL).astype(_F32),
         jnp.concatenate([jnp.tile(meta_tokens.astype(_F32), (nb, 1)),
                          jnp.zeros((tail - nb * N_META, D_MODEL), _F32)], axis=0))

    cost, sint = _rope_tables(layout, ntok)
    bias = _window_bias()
    w_rows, w_t = _rearranged_w_in(w_in)
    wa, wb, wo = w_branch_a.astype(_BF), w_branch_b.astype(_BF), w_out.astype(_BF)
    wup, wdown = w_ffn_up.astype(_BF), w_ffn_down.astype(_BF)
    row = lambda g: g.reshape(1, -1).astype(_F32)
    col = lambda g: g.reshape(-1, 1).astype(_F32)

    for l in range(depth):
        bound = (HEAD_DIM * SCALE * LOG2E * 1.02) * jnp.max(jnp.abs(q_norm_b[l])) * jnp.max(jnp.abs(k_norm_b[l]))
        unit = (jnp.arange(QK_ROWS - HEAD_DIM) == 0).astype(_F32).reshape(-1, 1)
        proj = _in_proj(h, row(g_mix_pre[l]), w_rows[l], w_t[l], cost, sint, col(q_norm_b[l]), col(k_norm_b[l]),
                        -bound.astype(_F32) * unit, unit, with_gates=(l == 0))
        if l == 0:
            *proj, h = proj
            qat, qbt, kat, kbt, vat, vbt, ga, gb = proj
        else:
            qat, qbt, kat, kbt, vat, vbt = proj
            ga, gb = _gates(h, row(g_mix_pre[l]), w_rows[l])
        sink = sink_a[l].astype(_F32) * LOG2E
        ka_tail, va_tail = _tail_features(kat, n_real, nb), _tail_features(vat, n_real, nb)
        kameta = jnp.transpose(ka_tail, (2, 0, 1, 3))
        vametat = jnp.transpose(va_tail, (2, 0, 1, 3))
        qa_meta = jnp.transpose(_tail_features(qat, n_real, nb), (2, 3, 0, 1)).reshape(nb * N_META, Q_COLS)
        qa_meta = jnp.pad(qa_meta, ((0, tail - nb * N_META), (0, 0)))

        oa, oa_tail = _window_attention(sink, qat, kat, vat, kameta, vametat, bias, qa_meta,
                                        jnp.zeros((tail, Q_COLS), _BF), ntok, layout)

        kbmeta = jnp.transpose(_tail_features(kbt, n_real, nb), (2, 0, 3, 1))
        vbmetat = jnp.transpose(_tail_features(vbt, n_real, nb), (2, 0, 1, 3))
        qmeta = _tail_features(qbt, n_real, nb).reshape(N_KV, GROUP, QK_ROWS, nb, N_META)
        qmeta = jnp.transpose(qmeta, (3, 0, 2, 1, 4)).reshape(nb, N_KV, QK_ROWS, GROUP * N_META)
        qmeta = jnp.pad(qmeta, ((0, 0), (0, 0), (0, 0), (0, V_LANES - GROUP * N_META)))

        glob = functools.partial(_global_attention, kbt=kbt, vbt=vbt, kmeta=kbmeta, vmetat=vbmetat)
        def mixer_b_maxfree():
            out = None
            tails = jnp.zeros((tail, Q_COLS), _BF)
            for row0, batches, batch0, n in ((0, b1, 0, n1), (b1 * n1, b2, b1, n2)):
                out, tails = _global_fused(qbt, kbt, vbt, kbmeta, vbmetat, qmeta, out, tails, ntok, tq=1024,
                                           q_row0=row0, batches=batches, kv_batch0=batch0, n=n)
            return out, tails

        def mixer_b_general():
            tails = glob(qmeta, prev=None, out_rows=tail, tq=N_META, q_row0=0, q_rows_per_batch=N_META,
                         grid_batches=b1, valid_batches=b1, kv_batch0=0, kv_row0=0, n=n1)
            tails = glob(qmeta, prev=tails, out_rows=tail, tq=N_META, q_row0=b1 * N_META, q_rows_per_batch=N_META,
                         grid_batches=tail // N_META - b1, valid_batches=b2, kv_batch0=b1, kv_row0=b1 * n1, n=n2)
            out = glob(qbt, prev=None, out_rows=ntok, tq=512, q_row0=0, q_rows_per_batch=n1, grid_batches=b1,
                       valid_batches=b1, kv_batch0=0, kv_row0=0, n=n1)
            out = glob(qbt, prev=out, out_rows=ntok, tq=512, q_row0=b1 * n1, q_rows_per_batch=n2, grid_batches=b2,
                       valid_batches=b2, kv_batch0=b1, kv_row0=b1 * n1, n=n2)
            return out, tails

        ob, ob_tail = lax.cond(bound <= SAFE_OFFSET_MAX, mixer_b_maxfree, mixer_b_general)

        h = _mix(h, oa, oa_tail, ob, ob_tail, ga, gb, wa[l], wb[l], wo[l], row(g_mix_post[l]))
        h = _ffn(h, row(g_ffn_pre[l]), wup[l], wdown[l], row(g_ffn_post[l]),
                 split_rows=(b1 * n1, b2 * n2) if l == depth - 1 else None)

    y_prompt, y_sample = h
    return (y_prompt.reshape(b1, n1, D_MODEL).astype(x_prompt.dtype),
            y_sample.reshape(b2, n2, D_MODEL).astype(x_sample.dtype))
```
